```python
import jax
import jax.numpy as jnp
from jax import lax
import numpy as np

D_MODEL = 2048
BATCH = 1
SEQ = 16384
DEPTH = 1
DEC_BATCH = 32
DEC_SEQ = 4
PAST_LEN = 16384
PAGE_SIZE = 128

HEAD_DIM = 128
ATT_GROUPS = ((128, 1), (512, 4), (2048, 16))
H_G = 4
N_ATT_HEADS = H_G * len(ATT_GROUPS)
ATT_W = N_ATT_HEADS * HEAD_DIM
ATT_OUT = H_G * HEAD_DIM
ROT_DIM = HEAD_DIM // 4
ROPE_THETA = 500000.0
Q_BLOCK = 128

HG_HEADS = 8
HG_DK = 128
HG_DV = 128
HG_WK = HG_HEADS * HG_DK
HG_WV = HG_HEADS * HG_DV
HG_CHUNK = 64

D_FF = 5632
CONV_W = 3

N_MOD = 6
EPS = 1e-6

IN_SIZES = (ATT_W, ATT_W, ATT_W, HG_WK, HG_WK, HG_WV, HG_WV, D_MODEL, D_MODEL)
IN_TOTAL = sum(IN_SIZES)
IN_SPLITS = tuple(int(s) for s in np.cumsum(IN_SIZES)[:-1])

kernel_name = 'hybrid_dilated_attn_hgrn2_convffn_step'


def rms_norm(x, w):
    xf = x.astype(jnp.float32)
    y = xf * lax.rsqrt(jnp.mean(xf * xf, axis=-1, keepdims=True) + EPS)
    return (y * w.astype(jnp.float32)).astype(x.dtype)


def partial_rope(x, pos):
    half = ROT_DIM // 2
    inv_freq = ROPE_THETA ** (-jnp.arange(half, dtype=jnp.float32) * 2.0 / ROT_DIM)
    ang = pos.astype(jnp.float32)[:, None] * inv_freq[None, :]
    cos = jnp.cos(ang)[None, :, None, :]
    sin = jnp.sin(ang)[None, :, None, :]
    xr = x[..., :ROT_DIM].astype(jnp.float32)
    x1, x2 = xr[..., :half], xr[..., half:]
    rot = jnp.concatenate([x1 * cos - x2 * sin, x2 * cos + x1 * sin], axis=-1).astype(x.dtype)
    return jnp.concatenate([rot, x[..., ROT_DIM:]], axis=-1)


def dilated_attention(q_groups, ctx_groups, prefix_lens, start):
    B, T = q_groups[0].shape[:2]
    qb = min(Q_BLOCK, T)
    n_blk = -(-T // qb)
    pad = n_blk * qb - T
    qs = [jnp.pad(q, ((0, 0), (0, pad), (0, 0), (0, 0))) for q in q_groups]
    scale = HEAD_DIM ** -0.5

    def block(n):
        t = n * qb + jnp.arange(qb, dtype=jnp.int32)
        lses, outs = [], []
        for (win, dil), q, ctx, P in zip(ATT_GROUPS, qs, ctx_groups, prefix_lens):
            j = jnp.arange(win // dil + 1, dtype=jnp.int32)
            row = P + t[:, None] - dil * j[None, :]
            valid = (row >= 0) & (row + (start - P) >= 0)
            kv = jnp.take(ctx, jnp.clip(row, 0, ctx.shape[1] - 1), axis=1).astype(jnp.float32)
            qblk = lax.dynamic_slice_in_dim(q, n * qb, qb, axis=1).astype(jnp.float32)
            s = jnp.einsum('bqhd,bqkhd->bhqk', qblk, kv[:, :, :, 0]) * scale
            s = jnp.where(valid[None, None], s, -jnp.inf)
            m = jnp.max(s, axis=-1)
            p = jnp.exp(s - m[..., None])
            l = jnp.sum(p, axis=-1)
            o = jnp.einsum('bhqk,bqkhd->bqhd', p, kv[:, :, :, 1]) / jnp.transpose(l, (0, 2, 1))[..., None]
            lses.append(m + jnp.log(l))
            outs.append(o)
        wts = jax.nn.softmax(jnp.stack(lses), axis=0)
        wts = jnp.transpose(wts, (0, 1, 3, 2))[..., None]
        return jnp.sum(wts * jnp.stack(outs), axis=0).astype(q_groups[0].dtype)

    out = lax.map(block, jnp.arange(n_blk, dtype=jnp.int32))
    return jnp.moveaxis(out, 0, 1).reshape(B, n_blk * qb, H_G, HEAD_DIM)[:, :T]


def hgrn2_recurrence(q, k, v, logf, s0):
    B, T, H, DK = q.shape
    DV = v.shape[-1]
    C = min(HG_CHUNK, T)
    n = -(-T // C)
    pad = n * C - T

    def chunks(a):
        a = jnp.pad(a, ((0, 0), (0, pad), (0, 0), (0, 0)))
        return jnp.moveaxis(a.reshape(B, n, C, H, a.shape[-1]), 1, 0)

    causal = jnp.tril(jnp.ones((C, C), dtype=bool))

    def step(S, inp):
        qc, kc, vc, gc = inp
        b = jnp.cumsum(gc, axis=1)
        diff = b[:, :, None] - b[:, None]
        decay = jnp.exp(jnp.where(causal[None, :, :, None, None], diff, -jnp.inf))
        att = jnp.einsum('bthd,btshd,bshd->bhts', qc, decay, kc)
        o = (jnp.einsum('bhts,bshv->bthv', att, vc)
             + jnp.einsum('bthd,bhdv->bthv', qc * jnp.exp(b), S))
        b_last = b[:, -1]
        S = (jnp.exp(b_last)[..., None] * S
             + jnp.einsum('bshd,bshv->bhdv', kc * jnp.exp(b_last[:, None] - b), vc))
        return S, o

    S, o = lax.scan(step, s0, (chunks(q), chunks(k), chunks(v), chunks(logf)))
    o = jnp.moveaxis(o, 0, 1).reshape(B, n * C, H, DV)[:, :T]
    return o, S


def conv_ffn(h, buf, w_a, w_b, conv_w, conv_b, w_down):
    T = h.shape[1]
    a = h @ w_a
    ctx = jnp.concatenate([buf.astype(a.dtype), a], axis=1)
    u = conv_b + sum(ctx[:, i:i + T] * conv_w[i] for i in range(CONV_W))
    y = jax.nn.silu(u) * (h @ w_b)
    return y @ w_down, ctx[:, T:]


def decoder_layer(x, c, start, kv_prefixes, hg_s0, conv_buf, lower_bound,
                  w_ada, b_ada, norm1_w, w_in, hg_norm_w, w_pa, w_pb, w_o,
                  norm2_w, w_ffn_a, w_ffn_b, conv_w, conv_b, w_ffn_down):
    B, T, _ = x.shape
    mod = jax.nn.silu(c) @ w_ada + b_ada
    sh1, sc1, g1, sh2, sc2, g2 = jnp.split(mod[:, None, :], N_MOD, axis=-1)

    h = rms_norm(x, norm1_w) * (1 + sc1) + sh1
    z = h @ w_in
    qa, ka, va, qh, fh, ih, gh, ga, gb = jnp.split(z, IN_SPLITS, axis=-1)

    pos = start + jnp.arange(T, dtype=jnp.int32)
    qa = partial_rope(qa.reshape(B, T, N_ATT_HEADS, HEAD_DIM), pos)
    ka = partial_rope(ka.reshape(B, T, N_ATT_HEADS, HEAD_DIM), pos)
    kv = jnp.stack([ka, va.reshape(B, T, N_ATT_HEADS, HEAD_DIM)], axis=2)
    q_groups, ctxs, prefix_lens, new_kv = [], [], [], []
    for g, (win, dil) in enumerate(ATT_GROUPS):
        hs = slice(g * H_G, (g + 1) * H_G)
        kv_g = kv[:, :, :, hs]
        prefix = kv_prefixes[g]
        if prefix is None:
            ctx, P, keep = kv_g, 0, min(win, T)
        else:
            ctx = jnp.concatenate([prefix.astype(kv_g.dtype), kv_g], axis=1)
            P = prefix.shape[1]
            keep = P
        q_groups.append(qa[:, :, hs])
        ctxs.append(ctx)
        prefix_lens.append(P)
        new_kv.append(ctx[:, ctx.shape[1] - keep:])
    o_att = dilated_attention(q_groups, ctxs, prefix_lens, start).reshape(B, T, ATT_OUT)

    qf = jax.nn.silu(qh.astype(jnp.float32)).reshape(B, T, HG_HEADS, HG_DK)
    fg = lower_bound + (1.0 - lower_bound) * jax.nn.sigmoid(fh.astype(jnp.float32))
    kf = (1.0 - fg).reshape(B, T, HG_HEADS, HG_DK)
    logf = jnp.log(fg).reshape(B, T, HG_HEADS, HG_DK)
    vf = ih.astype(jnp.float32).reshape(B, T, HG_HEADS, HG_DV)
    o_hg, s_new = hgrn2_recurrence(qf, kf, vf, logf, hg_s0.astype(jnp.float32))
    o_hg = rms_norm(o_hg, hg_norm_w) * jax.nn.silu(gh.astype(jnp.float32).reshape(B, T, HG_HEADS, HG_DV))
    o_hg = o_hg.reshape(B, T, HG_WV).astype(x.dtype)

    y_mix = jax.nn.sigmoid(ga) * (o_att @ w_pa) + jax.nn.sigmoid(gb) * (o_hg @ w_pb)
    x = x + g1 * (y_mix @ w_o)

    h2 = rms_norm(x, norm2_w) * (1 + sc2) + sh2
    f, new_buf = conv_ffn(h2, conv_buf, w_ffn_a, w_ffn_b, conv_w, conv_b, w_ffn_down)
    x = x + g2 * f
    return x, (new_kv[0], new_kv[1], new_kv[2], s_new, new_buf)


def setup_inputs(seed: int = 0) -> dict:
    key = jax.random.key(seed)
    keys = jax.random.split(key, 32)

    def nrm(i, shape, scale=1.0):
        return jax.random.normal(keys[i], shape, jnp.float32) * scale

    d = D_MODEL
    l0 = min(ATT_GROUPS[0][0], PAST_LEN)
    l1 = min(ATT_GROUPS[1][0], PAST_LEN)
    l2 = min(ATT_GROUPS[2][0], PAST_LEN)
    return {
        'x_prompt': nrm(0, (BATCH, SEQ, d)),
        'x_sample': nrm(1, (DEC_BATCH, DEC_SEQ, d)),
        'cache_kv_w128': nrm(2, (DEPTH, DEC_BATCH, l0, 2, H_G, HEAD_DIM)),
        'cache_kv_w512': nrm(3, (DEPTH, DEC_BATCH, l1, 2, H_G, HEAD_DIM)),
        'cache_kv_w2048': nrm(4, (DEPTH, DEC_BATCH, l2, 2, H_G, HEAD_DIM)),
        'state_hgrn': nrm(5, (DEPTH, DEC_BATCH, HG_HEADS, HG_DK, HG_DV), 0.5),
        'state_conv': nrm(6, (DEPTH, DEC_BATCH, CONV_W - 1, D_FF)),
        'c_prompt': nrm(7, (BATCH, d)),
        'c_sample': nrm(8, (DEC_BATCH, d)),
        'w_ada': nrm(9, (DEPTH, d, N_MOD * d), 0.5 * d ** -0.5),
        'b_ada': nrm(10, (DEPTH, N_MOD * d), 0.02),
        'norm1_w': 1.0 + nrm(11, (DEPTH, d), 0.02),
        'w_in': nrm(12, (DEPTH, d, IN_TOTAL), d ** -0.5),
        'hg_lb': nrm(13, (DEPTH + 1, HG_WK), 0.1),
        'hg_norm_w': 1.0 + nrm(14, (DEPTH, HG_DV), 0.02),
        'w_pa': nrm(15, (DEPTH, ATT_OUT, d), ATT_OUT ** -0.5),
        'w_pb': nrm(16, (DEPTH, HG_WV, d), HG_WV ** -0.5),
        'w_o': nrm(17, (DEPTH, d, d), d ** -0.5),
        'norm2_w': 1.0 + nrm(18, (DEPTH, d), 0.02),
        'w_ffn_a': nrm(19, (DEPTH, d, D_FF), d ** -0.5),
        'w_ffn_b': nrm(20, (DEPTH, d, D_FF), d ** -0.5),
        'conv_w': nrm(21, (DEPTH, CONV_W, D_FF), CONV_W ** -0.5),
        'conv_b': nrm(22, (DEPTH, D_FF), 0.02),
        'w_ffn_down': nrm(23, (DEPTH, D_FF, d), D_FF ** -0.5),
        'norm_f_w': 1.0 + nrm(24, (d,), 0.02),
    }


def reference(x_prompt, x_sample, cache_kv_w128, cache_kv_w512, cache_kv_w2048, state_hgrn, state_conv,
              c_prompt, c_sample, w_ada, b_ada, norm1_w, w_in, hg_lb, hg_norm_w, w_pa, w_pb, w_o,
              norm2_w, w_ffn_a, w_ffn_b, conv_w, conv_b, w_ffn_down, norm_f_w):
    lower = jnp.cumsum(jax.nn.softmax(hg_lb.astype(jnp.float32), axis=0), axis=0)
    xp, xs = x_prompt, x_sample
    states_p, states_s = [], []
    for l in range(DEPTH):
        lw = (lower[l], w_ada[l], b_ada[l], norm1_w[l], w_in[l], hg_norm_w[l], w_pa[l], w_pb[l], w_o[l],
              norm2_w[l], w_ffn_a[l], w_ffn_b[l], conv_w[l], conv_b[l], w_ffn_down[l])
        hg0 = jnp.zeros((xp.shape[0], HG_HEADS, HG_DK, HG_DV), jnp.float32)
        conv0 = jnp.zeros((xp.shape[0], CONV_W - 1, D_FF), xp.dtype)
        xp, st_p = decoder_layer(xp, c_prompt, 0, (None, None, None), hg0, conv0, *lw)
        xs, st_s = decoder_layer(xs, c_sample, PAST_LEN,
                                 (cache_kv_w128[l], cache_kv_w512[l], cache_kv_w2048[l]),
                                 state_hgrn[l], state_conv[l], *lw)
        states_p.append(st_p)
        states_s.append(st_s)
    y_prompt = rms_norm(xp, norm_f_w)
    y_sample = rms_norm(xs, norm_f_w)

    def stack(states, i):
        return jnp.stack([s[i] for s in states])

    return (y_prompt, y_sample,
            stack(states_p, 0), stack(states_p, 1), stack(states_p, 2), stack(states_p, 3), stack(states_p, 4),
            stack(states_s, 0), stack(states_s, 1), stack(states_s, 2), stack(states_s, 3), stack(states_s, 4))
```

```python
import functools

import numpy as np
import jax
import jax.numpy as jnp
from jax import lax
from jax.experimental import pallas as pl
from jax.experimental.pallas import tpu as pltpu

F32 = jnp.float32
BF16 = jnp.bfloat16

D_MODEL = 2048
SEQ = 16384
DEC_BATCH = 32
DEC_SEQ = 4
PAST_LEN = 16384

HEAD_DIM = 128
ATT_GROUPS = ((128, 1), (512, 4), (2048, 16))
H_G = 4
N_ATT_HEADS = H_G * len(ATT_GROUPS)
ATT_W = N_ATT_HEADS * HEAD_DIM
ATT_OUT = H_G * HEAD_DIM
ROT_DIM = HEAD_DIM // 4
ROPE_THETA = 500000.0
ATT_SCALE = HEAD_DIM ** -0.5

HG_HEADS = 8
HG_DK = 128
HG_DV = 128
HG_WK = HG_HEADS * HG_DK
HG_WV = HG_HEADS * HG_DV

D_FF = 5632
CONV_W = 3
N_MOD = 6
EPS = 1e-6

IN_SIZES = (ATT_W, ATT_W, ATT_W, HG_WK, HG_WK, HG_WV, HG_WV, D_MODEL, D_MODEL)
IN_TOTAL = sum(IN_SIZES)
IN_OFFS = tuple(int(s) for s in np.cumsum((0,) + IN_SIZES)[:-1])
OFF_QA, OFF_KA, OFF_VA, OFF_QH, OFF_FH, OFF_IH, OFF_GH, OFF_GA, OFF_GB = IN_OFFS

V7X_VMEM_BYTES = 64 * 1024 * 1024
SUBLANES = 8
LANES = 128
DEC_PAD = SUBLANES

COL_BLK = 512
HG_CHUNK = 128
ATT_QB = 128
ATT_SB = 2048


def _vmem_limit(nbytes):
    return int(min(V7X_VMEM_BYTES - 8 * 1024 * 1024, max(nbytes, 16 * 1024 * 1024)))


def _silu(x):
    return x * jax.nn.sigmoid(x)


def _ada_kernel(c_ref, w_ref, b_ref, o_ref):
    c = c_ref[...]
    s = _silu(c).astype(BF16)
    o_ref[...] = jnp.dot(s, w_ref[...].astype(BF16), preferred_element_type=F32) + b_ref[...]


def _ada(c_all, w_ada, b_ada):
    rows, d = c_all.shape
    n = w_ada.shape[1]
    tn = 1024
    return pl.pallas_call(
        _ada_kernel,
        out_shape=jax.ShapeDtypeStruct((rows, n), F32),
        grid=(n // tn,),
        in_specs=[pl.BlockSpec((rows, d), lambda j: (0, 0)),
                  pl.BlockSpec((d, tn), lambda j: (0, j)),
                  pl.BlockSpec((1, tn), lambda j: (0, j))],
        out_specs=pl.BlockSpec((rows, tn), lambda j: (0, j)),
        compiler_params=pltpu.CompilerParams(
            dimension_semantics=("arbitrary",),
            vmem_limit_bytes=_vmem_limit(2 * d * tn * 4 + d * tn * 2 + 4 * rows * (d + tn) * 4)),
        name="ada",
    )(c_all, w_ada, b_ada.reshape(1, n))


def _norm_mod_rows(x_ref, nw_ref, sc_ref, sh_ref, h_ref, tm, per_row):
    step = min(tm, 128)

    def body(r, carry):
        rs = pl.ds(pl.multiple_of(r * step, step), step)
        x = x_ref[rs, :]
        ms = jnp.mean(x * x, axis=-1, keepdims=True)
        y = x * lax.rsqrt(ms + EPS) * nw_ref[...]
        if per_row:
            h = y * (1.0 + sc_ref[rs, :]) + sh_ref[rs, :]
        else:
            h = y * (1.0 + sc_ref[...]) + sh_ref[...]
        h_ref[rs, :] = h.astype(BF16)
        return carry

    lax.fori_loop(0, tm // step, body, 0)


def _mod_spec(per_row, tm, d):
    if per_row:
        return pl.BlockSpec((tm, d), lambda i, j: (i, 0))
    return pl.BlockSpec((1, d), lambda i, j: (0, 0))


def _inproj_kernel(x_ref, sh_ref, sc_ref, nw_ref, w_ref, cos_ref, s1_ref, s2_ref, z_ref, h_scr,
                   *, tm, per_row, n_rope_blocks):
    j = pl.program_id(1)

    @pl.when(j == 0)
    def _():
        _norm_mod_rows(x_ref, nw_ref, sc_ref, sh_ref, h_scr, tm, per_row)

    acc = jnp.dot(h_scr[...], w_ref[...], preferred_element_type=F32)

    @pl.when(j < n_rope_blocks)
    def _():
        c = cos_ref[...]
        s1 = s1_ref[...]
        s2 = s2_ref[...]
        for hh in range(COL_BLK // HEAD_DIM):
            sl = slice(hh * HEAD_DIM, (hh + 1) * HEAD_DIM)
            blk = acc[:, sl]
            z_ref[:, sl] = (blk * c + pltpu.roll(blk, HEAD_DIM - ROT_DIM // 2, 1) * s1
                            + pltpu.roll(blk, ROT_DIM // 2, 1) * s2)

    @pl.when(j >= n_rope_blocks)
    def _():
        z_ref[...] = acc


def _inproj(x2d, sh, sc, nw, w_in_bf, rope, tm):
    rows, d = x2d.shape
    n = w_in_bf.shape[1]
    per_row = sh.shape[0] != 1
    cos_t, s1_t, s2_t = rope
    tn = COL_BLK
    kern = functools.partial(_inproj_kernel, tm=tm, per_row=per_row,
                             n_rope_blocks=(2 * ATT_W) // tn)
    tab = pl.BlockSpec((tm, HEAD_DIM), lambda i, j: (i, 0))
    vm = (2 * tm * d * 4 + tm * d * 2 + 2 * d * tn * 2 + 6 * tm * HEAD_DIM * 4 + 4 * tm * tn * 4
          + (4 * tm * d * 4 if per_row else 0) + 4 * 1024 * 1024)
    return pl.pallas_call(
        kern,
        out_shape=jax.ShapeDtypeStruct((rows, n), F32),
        grid=(rows // tm, n // tn),
        in_specs=[pl.BlockSpec((tm, d), lambda i, j: (i, 0)),
                  _mod_spec(per_row, tm, d), _mod_spec(per_row, tm, d),
                  pl.BlockSpec((1, d), lambda i, j: (0, 0)),
                  pl.BlockSpec((d, tn), lambda i, j: (0, j)),
                  tab, tab, tab],
        out_specs=pl.BlockSpec((tm, tn), lambda i, j: (i, j)),
        scratch_shapes=[pltpu.VMEM((tm, d), BF16)],
        compiler_params=pltpu.CompilerParams(
            dimension_semantics=("arbitrary", "arbitrary"),
            vmem_limit_bytes=_vmem_limit(vm)),
        name="inproj",
    )(x2d, sh, sc, nw, w_in_bf, cos_t, s1_t, s2_t)


def _rope_tables(pos):
    half = ROT_DIM // 2
    inv_freq = ROPE_THETA ** (-jnp.arange(half, dtype=F32) * 2.0 / ROT_DIM)
    ang = pos.astype(F32)[:, None] * inv_freq[None, :]
    cos, sin = jnp.cos(ang), jnp.sin(ang)
    rows = pos.shape[0]
    ones = jnp.ones((rows, HEAD_DIM - ROT_DIM), F32)
    zer = jnp.zeros((rows, HEAD_DIM - ROT_DIM), F32)
    zh = jnp.zeros((rows, half), F32)
    c = jnp.concatenate([cos, cos, ones], axis=1)
    s1 = jnp.concatenate([-sin, zh, zer], axis=1)
    s2 = jnp.concatenate([zh, sin, zer], axis=1)
    return c, s1, s2


def _attn_tile(q, ka, kb, va, vb, valid):
    k = jnp.concatenate([ka, kb], axis=0).astype(BF16)
    v = jnp.concatenate([va, vb], axis=0).astype(BF16)
    s = lax.dot_general((q * ATT_SCALE).astype(BF16), k, (((1,), (1,)), ((), ())),
                        preferred_element_type=F32)
    s = jnp.where(valid, s, -jnp.inf)
    m = jnp.max(s, axis=1, keepdims=True)
    p = jnp.exp(s - m)
    l = jnp.sum(p, axis=1, keepdims=True)
    o = jnp.dot(p.astype(BF16), v, preferred_element_type=F32) / l
    return o, m + jnp.log(l)


def _attn_kernel(*refs):
    q_refs = refs[0:3]
    kc_refs = refs[3:6]
    kp_refs = refs[6:9]
    vc_refs = refs[9:12]
    vp_refs = refs[12:15]
    o_ref = refs[15]
    og_scr, lse_scr = refs[16], refs[17]
    n = pl.program_id(0)

    row = lax.broadcasted_iota(jnp.int32, (ATT_QB, 2 * ATT_QB), 0)
    col = lax.broadcasted_iota(jnp.int32, (ATT_QB, 2 * ATT_QB), 1)
    band = (col >= row) & (col <= row + ATT_QB)
    band_first = band & (col >= jnp.where(n > 0, 0, ATT_QB))

    for g, (_, dil) in enumerate(ATT_GROUPS):
        q_ref, kc_ref, kp_ref, vc_ref, vp_ref = q_refs[g], kc_refs[g], kp_refs[g], vc_refs[g], vp_refs[g]
        nq = ATT_SB // (dil * ATT_QB)

        def rows(start, count):
            if dil == 1:
                return pl.ds(pl.multiple_of(start, ATT_QB), count)
            return pl.ds(start, count, stride=dil)

        def put(g_, tok0, o, lse):
            og_scr[g_, rows(tok0, ATT_QB), :] = o
            lse_scr[g_, rows(tok0, ATT_QB), :] = jnp.broadcast_to(lse, (ATT_QB, HEAD_DIM))

        def first_body(r, carry):
            o, lse = _attn_tile(q_ref[rows(r, ATT_QB), :],
                                kp_ref[rows(r, ATT_QB), :], kc_ref[rows(r, ATT_QB), :],
                                vp_ref[rows(r, ATT_QB), :], vc_ref[rows(r, ATT_QB), :],
                                band_first)
            put(g, r, o, lse)
            return carry

        lax.fori_loop(0, dil, first_body, 0)

        if nq > 1:
            def rest_body(t, carry):
                r = t // (nq - 1)
                u = t % (nq - 1) + 1
                q0 = r + dil * (u * ATT_QB)
                k0 = r + dil * ((u - 1) * ATT_QB)
                k1 = r + dil * (u * ATT_QB)
                o, lse = _attn_tile(q_ref[rows(q0, ATT_QB), :],
                                    kc_ref[rows(k0, ATT_QB), :], kc_ref[rows(k1, ATT_QB), :],
                                    vc_ref[rows(k0, ATT_QB), :], vc_ref[rows(k1, ATT_QB), :],
                                    band)
                put(g, q0, o, lse)
                return carry

            lax.fori_loop(0, dil * (nq - 1), rest_body, 0)

    def merge_body(c, carry):
        rs = pl.ds(pl.multiple_of(c * 256, 256), 256)
        l0, l1, l2 = lse_scr[0, rs, :], lse_scr[1, rs, :], lse_scr[2, rs, :]
        m = jnp.maximum(jnp.maximum(l0, l1), l2)
        e0, e1, e2 = jnp.exp(l0 - m), jnp.exp(l1 - m), jnp.exp(l2 - m)
        num = e0 * og_scr[0, rs, :] + e1 * og_scr[1, rs, :] + e2 * og_scr[2, rs, :]
        o_ref[rs, :] = num / (e0 + e1 + e2)
        return carry

    lax.fori_loop(0, ATT_SB // 256, merge_body, 0)


def _attention_prompt(z):
    t = z.shape[0]
    nsb = t // ATT_SB
    in_specs, args = [], []

    def cur(colblk):
        return pl.BlockSpec((ATT_SB, HEAD_DIM), lambda n, h, c=colblk: (n, c + h))

    def prev(colblk, dil):
        rows_p = dil * ATT_QB
        per = ATT_SB // rows_p
        return pl.BlockSpec((rows_p, HEAD_DIM),
                            lambda n, h, c=colblk, per=per: (jnp.maximum(n * per - 1, 0), c + h))

    for base in (OFF_QA,):
        for g in range(3):
            in_specs.append(cur(base // HEAD_DIM + g * H_G)); args.append(z)
    for base in (OFF_KA, OFF_VA):
        for g in range(3):
            in_specs.append(cur(base // HEAD_DIM + g * H_G)); args.append(z)
        for g, (_, dil) in enumerate(ATT_GROUPS):
            in_specs.append(prev(base // HEAD_DIM + g * H_G, dil)); args.append(z)
    blk = ATT_SB * HEAD_DIM * 4
    prev_rows = sum(d * ATT_QB for _, d in ATT_GROUPS)
    vm = 2 * (9 * blk + 2 * prev_rows * HEAD_DIM * 4) + 2 * blk + 6 * blk + 8 * 1024 * 1024
    return pl.pallas_call(
        _attn_kernel,
        out_shape=jax.ShapeDtypeStruct((t, ATT_OUT), F32),
        grid=(nsb, H_G),
        in_specs=in_specs,
        out_specs=pl.BlockSpec((ATT_SB, HEAD_DIM), lambda n, h: (n, h)),
        scratch_shapes=[pltpu.VMEM((3, ATT_SB, HEAD_DIM), F32),
                        pltpu.VMEM((3, ATT_SB, HEAD_DIM), F32)],
        compiler_params=pltpu.CompilerParams(
            dimension_semantics=("arbitrary", "arbitrary"),
            vmem_limit_bytes=_vmem_limit(vm)),
        name="attn_prompt",
    )(*args)


def _attn_dec_kernel(q_ref, kn_ref, vn_ref, c0_ref, c1_ref, c2_ref, o_ref):
    caches = (c0_ref, c1_ref, c2_ref)
    zpad = jnp.zeros((LANES - DEC_PAD, HEAD_DIM), F32)
    for h in range(H_G):
        outs, lses = [], []
        for g, (win, dil) in enumerate(ATT_GROUPS):
            c_ref = caches[g]
            p_len = c_ref.shape[1]
            hs = slice((g * H_G + h) * HEAD_DIM, (g * H_G + h + 1) * HEAD_DIM)
            q = (q_ref[0, :, hs] * ATT_SCALE).astype(BF16)
            k = jnp.concatenate([c_ref[0, :, h * HEAD_DIM:(h + 1) * HEAD_DIM], kn_ref[0, :, hs], zpad],
                                axis=0).astype(BF16)
            v = jnp.concatenate([c_ref[0, :, (H_G + h) * HEAD_DIM:(H_G + h + 1) * HEAD_DIM],
                                 vn_ref[0, :, hs], zpad], axis=0).astype(BF16)
            s = lax.dot_general(q, k, (((1,), (1,)), ((), ())), preferred_element_type=F32)
            tq = lax.broadcasted_iota(jnp.int32, s.shape, 0) & (DEC_SEQ - 1)
            col = lax.broadcasted_iota(jnp.int32, s.shape, 1)
            delta = p_len + tq - col
            valid = ((delta >= 0) & ((delta & (dil - 1)) == 0) & (delta <= win)
                     & (col < p_len + DEC_SEQ))
            s = jnp.where(valid, s, -jnp.inf)
            m = jnp.max(s, axis=1, keepdims=True)
            p = jnp.exp(s - m)
            l = jnp.sum(p, axis=1, keepdims=True)
            outs.append(jnp.dot(p.astype(BF16), v, preferred_element_type=F32) / l)
            lses.append(m + jnp.log(l))
        m = jnp.maximum(jnp.maximum(lses[0], lses[1]), lses[2])
        e = [jnp.exp(x - m) for x in lses]
        num = e[0] * outs[0] + e[1] * outs[1] + e[2] * outs[2]
        o_ref[0, :, h * HEAD_DIM:(h + 1) * HEAD_DIM] = num / (e[0] + e[1] + e[2])


def _attention_decode(z8, caches):
    b = z8.shape[0]
    qspec = lambda c: pl.BlockSpec((1, DEC_PAD, ATT_W), lambda i, c=c: (i, 0, c))
    cspecs = [pl.BlockSpec((1, c.shape[1], c.shape[2]), lambda i: (i, 0, 0)) for c in caches]
    vm = 2 * sum(c.shape[1] * c.shape[2] * 4 for c in caches) + 16 * 1024 * 1024
    return pl.pallas_call(
        _attn_dec_kernel,
        out_shape=jax.ShapeDtypeStruct((b, DEC_PAD, ATT_OUT), F32),
        grid=(b,),
        in_specs=[qspec(0), qspec(1), qspec(2)] + cspecs,
        out_specs=pl.BlockSpec((1, DEC_PAD, ATT_OUT), lambda i: (i, 0, 0)),
        compiler_params=pltpu.CompilerParams(
            dimension_semantics=("arbitrary",),
            vmem_limit_bytes=_vmem_limit(vm)),
        name="attn_decode",
    )(z8, z8, z8, *caches)


def _hgrn_consts(c):
    t = np.arange(c)[:, None]
    u = np.arange(c)[None, :]
    seg = [(u <= t)]
    cross = []
    m = c // 2
    while m >= 1:
        blk_t, off_t = t // (2 * m), t % (2 * m)
        piv = blk_t * 2 * m + m - 1
        upper = off_t >= m
        seg.append(np.where(upper, (u > piv) & (u <= t), (u > t) & (u <= piv)))
        cross.append((t // (2 * m) == u // (2 * m)) & (t % (2 * m) >= m) & (u % (2 * m) < m))
        m //= 2
    seg.append(u > t)
    cross.append(t == u)
    return (np.stack(seg).astype(np.float32).reshape(-1, c),
            np.stack(cross).astype(np.float32))


def _hgrn_kernel(q_ref, f_ref, i_ref, g_ref, lb_ref, nw_ref, seg_ref, cross_ref, o_ref, s_out_ref, s_scr,
                 *, tb, n_lvl):
    n = pl.program_id(1)
    c = HG_CHUNK

    @pl.when(n == 0)
    def _():
        s_scr[...] = jnp.zeros_like(s_scr)

    lb = lb_ref[...]
    mx = jnp.maximum(lb[0:1, :], lb[1:2, :])
    e0, e1 = jnp.exp(lb[0:1, :] - mx), jnp.exp(lb[1:2, :] - mx)
    lower = e0 / (e0 + e1)
    nw = nw_ref[...]
    seg = seg_ref[...]

    def chunk(ci, carry):
        rs = pl.ds(pl.multiple_of(ci * c, c), c)
        q = _silu(q_ref[rs, :])
        fg = lower + (1.0 - lower) * jax.nn.sigmoid(f_ref[rs, :])
        k = 1.0 - fg
        v = i_ref[rs, :].astype(BF16)
        lg = jnp.log(fg)
        lg_hi = lg.astype(BF16)
        lg_lo = (lg - lg_hi.astype(F32)).astype(BF16)
        dsum = (jnp.dot(seg, lg_hi, preferred_element_type=F32)
                + jnp.dot(seg, lg_lo, preferred_element_type=F32))
        w = jnp.exp(dsum)
        att = cross_ref[n_lvl] * lax.dot_general(q.astype(BF16), k.astype(BF16), (((1,), (1,)), ((), ())),
                                                 preferred_element_type=F32)
        for lv in range(n_lvl):
            wl = w[(1 + lv) * c:(2 + lv) * c]
            att = att + cross_ref[lv] * lax.dot_general(
                (q * wl).astype(BF16), (k * wl).astype(BF16), (((1,), (1,)), ((), ())),
                preferred_element_type=F32)
        s_old = s_scr[...]
        o = (jnp.dot(att.astype(BF16), v, preferred_element_type=F32)
             + jnp.dot((q * w[0:c]).astype(BF16), s_old.astype(BF16), preferred_element_type=F32))
        k_end_t = (k * w[(n_lvl + 1) * c:(n_lvl + 2) * c]).T.astype(BF16)
        decay_col = jnp.broadcast_to(w[c - 1:c, :], (HG_DK, HG_DV)).T
        s_scr[...] = decay_col * s_old + jnp.dot(k_end_t, v, preferred_element_type=F32)
        ms = jnp.mean(o * o, axis=-1, keepdims=True)
        o_ref[rs, :] = o * lax.rsqrt(ms + EPS) * nw * _silu(g_ref[rs, :])
        return carry

    lax.fori_loop(0, tb // c, chunk, 0)

    @pl.when(n == pl.num_programs(1) - 1)
    def _():
        s_out_ref[0] = s_scr[...]


def _hgrn_prompt(z, hg_lb, hg_norm_w, tb):
    t = z.shape[0]
    seg_np, cross_np = _hgrn_consts(HG_CHUNK)
    n_lvl = cross_np.shape[0] - 1
    seg = jnp.asarray(seg_np, dtype=BF16)
    cross = jnp.asarray(cross_np, dtype=F32)
    col = lambda off: pl.BlockSpec((tb, HG_DK), lambda h, n, o=off // HG_DK: (n, o + h))
    kern = functools.partial(_hgrn_kernel, tb=tb, n_lvl=n_lvl)
    vm = 2 * 5 * tb * HG_DK * 4 + 2 * seg_np.size * 2 + 2 * cross_np.size * 4 + 16 * 1024 * 1024
    return pl.pallas_call(
        kern,
        out_shape=(jax.ShapeDtypeStruct((t, HG_WV), F32),
                   jax.ShapeDtypeStruct((HG_HEADS, HG_DK, HG_DV), F32)),
        grid=(HG_HEADS, t // tb),
        in_specs=[col(OFF_QH), col(OFF_FH), col(OFF_IH), col(OFF_GH),
                  pl.BlockSpec((2, HG_DK), lambda h, n: (0, h)),
                  pl.BlockSpec((1, HG_DV), lambda h, n: (0, 0)),
                  pl.BlockSpec(seg_np.shape, lambda h, n: (0, 0)),
                  pl.BlockSpec(cross_np.shape, lambda h, n: (0, 0, 0))],
        out_specs=(pl.BlockSpec((tb, HG_DV), lambda h, n: (n, h)),
                   pl.BlockSpec((1, HG_DK, HG_DV), lambda h, n: (h, 0, 0))),
        scratch_shapes=[pltpu.VMEM((HG_DK, HG_DV), F32)],
        compiler_params=pltpu.CompilerParams(
            dimension_semantics=("arbitrary", "arbitrary"),
            vmem_limit_bytes=_vmem_limit(vm)),
        name="hgrn_prompt",
    )(z, z, z, z, hg_lb, hg_norm_w, seg, cross)


def _hgrn_dec_kernel(q_ref, f_ref, i_ref, g_ref, lb_ref, nw_ref, s_ref, o_ref, s_out_ref, *, heads):
    lb = lb_ref[...]
    mx = jnp.maximum(lb[0:1, :], lb[1:2, :])
    e0, e1 = jnp.exp(lb[0:1, :] - mx), jnp.exp(lb[1:2, :] - mx)
    lower = e0 / (e0 + e1)
    nw = nw_ref[...]
    zpad = jnp.zeros((HG_DK - DEC_PAD, HG_DK), F32)

    def cols(x):
        return jnp.concatenate([x, zpad], axis=0).T

    for h in range(heads):
        hs = slice(h * HG_DK, (h + 1) * HG_DK)
        q = _silu(q_ref[0, :, hs])
        fg = lower[:, hs] + (1.0 - lower[:, hs]) * jax.nn.sigmoid(f_ref[0, :, hs])
        v = i_ref[0, :, hs]
        q_t, f_t, k_t = cols(q), cols(fg), cols(1.0 - fg)
        s = s_ref[0, h]
        row = lax.broadcasted_iota(jnp.int32, (DEC_PAD, HG_DV), 0)
        o = jnp.zeros((DEC_PAD, HG_DV), F32)
        for t in range(DEC_SEQ):
            bc = lambda a: jnp.broadcast_to(a[:, t:t + 1], (HG_DK, HG_DV))
            s = bc(f_t) * s + bc(k_t) * v[t:t + 1, :]
            o = jnp.where(row == t, jnp.sum(s * bc(q_t), axis=0, keepdims=True), o)
        ms = jnp.mean(o * o, axis=-1, keepdims=True)
        o_ref[0, :, hs] = o * lax.rsqrt(ms + EPS) * nw * _silu(g_ref[0, :, hs])
        s_out_ref[0, h] = s


def _hgrn_decode(z8, hg_lb, hg_norm_w, state):
    b = z8.shape[0]
    heads = COL_BLK // HG_DK
    nhb = HG_HEADS // heads
    col = lambda off: pl.BlockSpec((1, DEC_PAD, COL_BLK), lambda i, j, o=off // COL_BLK: (i, 0, o + j))
    kern = functools.partial(_hgrn_dec_kernel, heads=heads)
    sspec = pl.BlockSpec((1, heads, HG_DK, HG_DV), lambda i, j: (i, j, 0, 0))
    return pl.pallas_call(
        kern,
        out_shape=(jax.ShapeDtypeStruct((b, DEC_PAD, HG_WV), F32),
                   jax.ShapeDtypeStruct(state.shape, F32)),
        grid=(b, nhb),
        in_specs=[col(OFF_QH), col(OFF_FH), col(OFF_IH), col(OFF_GH),
                  pl.BlockSpec((2, COL_BLK), lambda i, j: (0, j)),
                  pl.BlockSpec((1, HG_DV), lambda i, j: (0, 0)),
                  sspec],
        out_specs=(pl.BlockSpec((1, DEC_PAD, COL_BLK), lambda i, j: (i, 0, j)), sspec),
        compiler_params=pltpu.CompilerParams(
            dimension_semantics=("arbitrary", "arbitrary"),
            vmem_limit_bytes=_vmem_limit(0)),
        name="hgrn_decode",
    )(z8, z8, z8, z8, hg_lb, hg_norm_w, state)


def _merge_kernel(att_ref, hg_ref, ga_ref, gb_ref, x_ref, g1_ref, wpa_ref, wpb_ref, wo_ref, o_ref,
                  att_scr, hg_scr, acc_scr):
    n = pl.program_id(1)

    @pl.when(n == 0)
    def _():
        att_scr[...] = att_ref[...].astype(BF16)
        hg_scr[...] = hg_ref[...].astype(BF16)

    ya = jnp.dot(att_scr[...], wpa_ref[...], preferred_element_type=F32)
    yb = jnp.dot(hg_scr[...], wpb_ref[...], preferred_element_type=F32)
    ymix = jax.nn.sigmoid(ga_ref[...]) * ya + jax.nn.sigmoid(gb_ref[...]) * yb
    part = jnp.dot(ymix.astype(BF16), wo_ref[...], preferred_element_type=F32)

    @pl.when(n == 0)
    def _():
        acc_scr[...] = part

    @pl.when(n > 0)
    def _():
        acc_scr[...] += part

    @pl.when(n == pl.num_programs(1) - 1)
    def _():
        o_ref[...] = x_ref[...] + g1_ref[...] * acc_scr[...]


def _merge(att, hg, z, x2d, g1, wpa_bf, wpb_bf, wo_bf, tm):
    rows, d = x2d.shape
    per_row = g1.shape[0] != 1
    tn = COL_BLK
    nn = d // tn
    vm = (2 * tm * (ATT_OUT + HG_WV + 2 * tn + 2 * d) * 4 + tm * (ATT_OUT + HG_WV) * 2 + tm * d * 4
          + 2 * (ATT_OUT * tn + HG_WV * tn + tn * d) * 2 + (2 * tm * d * 4 if per_row else 0)
          + 6 * tm * tn * 4 + 4 * 1024 * 1024)
    return pl.pallas_call(
        _merge_kernel,
        out_shape=jax.ShapeDtypeStruct((rows, d), F32),
        grid=(rows // tm, nn),
        in_specs=[pl.BlockSpec((tm, ATT_OUT), lambda i, n: (i, 0)),
                  pl.BlockSpec((tm, HG_WV), lambda i, n: (i, 0)),
                  pl.BlockSpec((tm, tn), lambda i, n: (i, OFF_GA // tn + n)),
                  pl.BlockSpec((tm, tn), lambda i, n: (i, OFF_GB // tn + n)),
                  pl.BlockSpec((tm, d), lambda i, n: (i, 0)),
                  _mod_spec(per_row, tm, d),
                  pl.BlockSpec((ATT_OUT, tn), lambda i, n: (0, n)),
                  pl.BlockSpec((HG_WV, tn), lambda i, n: (0, n)),
                  pl.BlockSpec((tn, d), lambda i, n: (n, 0))],
        out_specs=pl.BlockSpec((tm, d), lambda i, n: (i, 0)),
        scratch_shapes=[pltpu.VMEM((tm, ATT_OUT), BF16), pltpu.VMEM((tm, HG_WV), BF16),
                        pltpu.VMEM((tm, d), F32)],
        compiler_params=pltpu.CompilerParams(
            dimension_semantics=("arbitrary", "arbitrary"),
            vmem_limit_bytes=_vmem_limit(vm)),
        name="merge",
    )(att, hg, z, z, x2d, g1, wpa_bf, wpb_bf, wo_bf)


def _ffn_kernel(x_ref, sh_ref, sc_ref, g2_ref, nw_ref, nf_ref, wa_ref, wb_ref, wd_ref, cw_ref, cb_ref,
                p1_ref, p2_ref, y_ref, tail_ref, h_scr, acc_scr, carry_scr,
                *, tm, ta, per_row, seq_rows):
    i = pl.program_id(0)
    f = pl.program_id(1)

    @pl.when(f == 0)
    def _():
        _norm_mod_rows(x_ref, nw_ref, sc_ref, sh_ref, h_scr, tm, per_row)

    h = h_scr[...]
    a = jnp.dot(h, wa_ref[...], preferred_element_type=F32)
    tail_ref[...] = a[tm - ta:, :]
    row = lax.broadcasted_iota(jnp.int32, a.shape, 0)
    r1 = pltpu.roll(a, 1, 0)
    r2 = pltpu.roll(a, 2, 0)
    if seq_rows is None:
        @pl.when(i == 0)
        def _():
            carry_scr[f] = p1_ref[...]
        prev = carry_scr[f]
        a1 = jnp.where(row == 0, prev[SUBLANES - 1:SUBLANES, :], r1)
        a2 = jnp.where(row == 0, prev[SUBLANES - 2:SUBLANES - 1, :],
                       jnp.where(row == 1, prev[SUBLANES - 1:SUBLANES, :], r2))
        carry_scr[f] = a[tm - SUBLANES:, :]
    else:
        t = row & (seq_rows - 1)
        a1 = jnp.where(t == 0, p1_ref[...], r1)
        a2 = jnp.where(t <= 1, p2_ref[...], r2)
    cw = cw_ref[...]
    u = cb_ref[...] + a2 * cw[0:1, :] + a1 * cw[1:2, :] + a * cw[2:3, :]
    y = _silu(u) * jnp.dot(h, wb_ref[...], preferred_element_type=F32)
    part = jnp.dot(y.astype(BF16), wd_ref[...], preferred_element_type=F32)

    @pl.when(f == 0)
    def _():
        acc_scr[...] = part

    @pl.when(f > 0)
    def _():
        acc_scr[...] += part

    @pl.when(f == pl.num_programs(1) - 1)
    def _():
        step = min(tm, 128)

        def body(r, carry):
            rs = pl.ds(pl.multiple_of(r * step, step), step)
            g2 = g2_ref[rs, :] if per_row else g2_ref[...]
            x2 = x_ref[rs, :] + g2 * acc_scr[rs, :]
            ms = jnp.mean(x2 * x2, axis=-1, keepdims=True)
            y_ref[rs, :] = x2 * lax.rsqrt(ms + EPS) * nf_ref[...]
            return carry

        lax.fori_loop(0, tm // step, body, 0)


def _ffn(x1, sh, sc, g2, nw, nf, wa_bf, wb_bf, wd_bf, conv_w, conv_b, p1, p2, tm, seq_rows):
    rows, d = x1.shape
    dff = wa_bf.shape[1]
    per_row = sh.shape[0] != 1
    tf = COL_BLK
    nf_blk = dff // tf
    ta = SUBLANES if seq_rows is None else tm
    prow = SUBLANES if seq_rows is None else tm
    kern = functools.partial(_ffn_kernel, tm=tm, ta=ta, per_row=per_row, seq_rows=seq_rows)
    pspec = pl.BlockSpec((prow, tf), lambda i, f: (0 if seq_rows is None else i, f))
    vm = (4 * tm * d * 4 + tm * d * 2 + tm * d * 4 + 2 * 3 * d * tf * 2 + 10 * tm * tf * 4
          + (6 * tm * d * 4 if per_row else 0) + 8 * 1024 * 1024)
    return pl.pallas_call(
        kern,
        out_shape=(jax.ShapeDtypeStruct((rows, d), F32),
                   jax.ShapeDtypeStruct((ta * (rows // tm), dff), F32)),
        grid=(rows // tm, nf_blk),
        in_specs=[pl.BlockSpec((tm, d), lambda i, f: (i, 0)),
                  _mod_spec(per_row, tm, d), _mod_spec(per_row, tm, d), _mod_spec(per_row, tm, d),
                  pl.BlockSpec((1, d), lambda i, f: (0, 0)),
                  pl.BlockSpec((1, d), lambda i, f: (0, 0)),
                  pl.BlockSpec((d, tf), lambda i, f: (0, f)),
                  pl.BlockSpec((d, tf), lambda i, f: (0, f)),
                  pl.BlockSpec((tf, d), lambda i, f: (f, 0)),
                  pl.BlockSpec((CONV_W, tf), lambda i, f: (0, f)),
                  pl.BlockSpec((1, tf), lambda i, f: (0, f)),
                  pspec, pspec],
        out_specs=(pl.BlockSpec((tm, d), lambda i, f: (i, 0)),
                   pl.BlockSpec((ta, tf), lambda i, f: (i, f))),
        scratch_shapes=[pltpu.VMEM((tm, d), BF16), pltpu.VMEM((tm, d), F32),
                        pltpu.VMEM((nf_blk, SUBLANES, tf), F32)],
        compiler_params=pltpu.CompilerParams(
            dimension_semantics=("arbitrary", "arbitrary"),
            vmem_limit_bytes=_vmem_limit(vm)),
        name="ffn",
    )(x1, sh, sc, g2, nw, nf, wa_bf, wb_bf, wd_bf, conv_w, conv_b, p1, p2)


def kernel(x_prompt, x_sample, cache_kv_w128, cache_kv_w512, cache_kv_w2048, state_hgrn, state_conv,
           c_prompt, c_sample, w_ada, b_ada, norm1_w, w_in, hg_lb, hg_norm_w, w_pa, w_pb, w_o,
           norm2_w, w_ffn_a, w_ffn_b, conv_w, conv_b, w_ffn_down, norm_f_w):
    d = D_MODEL
    bp, t, _ = x_prompt.shape
    bs, ts, _ = x_sample.shape
    assert bp == 1 and ts == DEC_SEQ and w_ada.shape[0] == 1
    rows_s = bs * ts

    w_in_bf = w_in[0].astype(BF16)
    wpa_bf, wpb_bf, wo_bf = w_pa[0].astype(BF16), w_pb[0].astype(BF16), w_o[0].astype(BF16)
    wa_bf, wb_bf, wd_bf = w_ffn_a[0].astype(BF16), w_ffn_b[0].astype(BF16), w_ffn_down[0].astype(BF16)
    nw1, nw2, nwf = norm1_w[0].reshape(1, d), norm2_w[0].reshape(1, d), norm_f_w.reshape(1, d)
    hg_nw = hg_norm_w[0].reshape(1, HG_DV)
    cw, cb = conv_w[0], conv_b[0].reshape(1, D_FF)

    n_seq = bp + bs
    pad_rows = -n_seq % SUBLANES
    c_all = jnp.concatenate([c_prompt, c_sample, jnp.zeros((pad_rows, d), F32)], axis=0)
    mod = _ada(c_all, w_ada[0], b_ada[0])
    mod_p = [mod[0:bp, k * d:(k + 1) * d] for k in range(N_MOD)]
    mod_s = [jnp.repeat(mod[bp:n_seq, k * d:(k + 1) * d], ts, axis=0) for k in range(N_MOD)]

    xp = x_prompt.reshape(t, d)
    rope_p = _rope_tables(jnp.arange(t, dtype=jnp.int32))
    zp = _inproj(xp, mod_p[0], mod_p[1], nw1, w_in_bf, rope_p, tm=1024)
    att_p = _attention_prompt(zp)
    hg_p, s_p = _hgrn_prompt(zp, hg_lb, hg_nw, tb=512)
    x1p = _merge(att_p, hg_p, zp, xp, mod_p[2], wpa_bf, wpb_bf, wo_bf, tm=512)
    conv0 = jnp.zeros((SUBLANES, D_FF), F32)
    yp, tail_p = _ffn(x1p, mod_p[3], mod_p[4], mod_p[5], nw2, nwf, wa_bf, wb_bf, wd_bf, cw, cb,
                      conv0, conv0, tm=512, seq_rows=None)

    kv_p = []
    for g, (win, _) in enumerate(ATT_GROUPS):
        keep = min(win, t)
        ks = zp[t - keep:, OFF_KA + g * ATT_OUT:OFF_KA + (g + 1) * ATT_OUT]
        vs = zp[t - keep:, OFF_VA + g * ATT_OUT:OFF_VA + (g + 1) * ATT_OUT]
        kv_p.append(jnp.stack([ks.reshape(keep, H_G, HEAD_DIM), vs.reshape(keep, H_G, HEAD_DIM)],
                              axis=1)[None, None])
    hgrn_p = s_p[None, None]
    conv_p = tail_p[tail_p.shape[0] - (CONV_W - 1):][None, None]

    xs = x_sample.reshape(rows_s, d)
    pos_s = PAST_LEN + (jnp.arange(rows_s, dtype=jnp.int32) % ts)
    rope_s = _rope_tables(pos_s)
    zs = _inproj(xs, mod_s[0], mod_s[1], nw1, w_in_bf, rope_s, tm=rows_s)
    zs8 = jnp.pad(zs.reshape(bs, ts, IN_TOTAL), ((0, 0), (0, DEC_PAD - ts), (0, 0)))
    caches = [c[0].reshape(bs, c.shape[2], 2 * ATT_OUT)
              for c in (cache_kv_w128, cache_kv_w512, cache_kv_w2048)]
    att_s = _attention_decode(zs8, caches)[:, :ts].reshape(rows_s, ATT_OUT)
    hg_s8, s_s = _hgrn_decode(zs8, hg_lb, hg_nw, state_hgrn[0])
    hg_s = hg_s8[:, :ts].reshape(rows_s, HG_WV)
    x1s = _merge(att_s, hg_s, zs, xs, mod_s[2], wpa_bf, wpb_bf, wo_bf, tm=rows_s)
    buf = state_conv[0]
    zrow = jnp.zeros((bs, 1, D_FF), F32)
    p1 = jnp.concatenate([buf[:, 1:2], zrow, zrow, zrow], axis=1).reshape(rows_s, D_FF)
    p2 = jnp.concatenate([buf[:, 0:1], buf[:, 1:2], zrow, zrow], axis=1).reshape(rows_s, D_FF)
    ys, a_s = _ffn(x1s, mod_s[3], mod_s[4], mod_s[5], nw2, nwf, wa_bf, wb_bf, wd_bf, cw, cb,
                   p1, p2, tm=rows_s, seq_rows=ts)

    kv_s = []
    for g in range(len(ATT_GROUPS)):
        kn = zs[:, OFF_KA + g * ATT_OUT:OFF_KA + (g + 1) * ATT_OUT].reshape(bs, ts, 1, H_G, HEAD_DIM)
        vn = zs[:, OFF_VA + g * ATT_OUT:OFF_VA + (g + 1) * ATT_OUT].reshape(bs, ts, 1, H_G, HEAD_DIM)
        new = jnp.concatenate([kn, vn], axis=2)
        old = (cache_kv_w128, cache_kv_w512, cache_kv_w2048)[g][0]
        kv_s.append(jnp.concatenate([old[:, ts:], new], axis=1)[None])
    hgrn_s = s_s[None]
    conv_s = a_s.reshape(bs, ts, D_FF)[:, ts - (CONV_W - 1):][None]

    return (yp.reshape(bp, t, d), ys.reshape(bs, ts, d),
            kv_p[0], kv_p[1], kv_p[2], hgrn_p, conv_p,
            kv_s[0], kv_s[1], kv_s[2], hgrn_s, conv_s)
```

```python
import functools

import numpy as np
import jax
import jax.numpy as jnp
from jax import lax
from jax.experimental import pallas as pl
from jax.experimental.pallas import tpu as pltpu

F32 = jnp.float32
BF16 = jnp.bfloat16

D_MODEL = 2048
SEQ = 16384
DEC_BATCH = 32
DEC_SEQ = 4
PAST_LEN = 16384

HEAD_DIM = 128
ATT_GROUPS = ((128, 1), (512, 4), (2048, 16))
H_G = 4
N_ATT_HEADS = H_G * len(ATT_GROUPS)
ATT_W = N_ATT_HEADS * HEAD_DIM
ATT_OUT = H_G * HEAD_DIM
ROT_DIM = HEAD_DIM // 4
ROPE_THETA = 500000.0
ATT_SCALE = HEAD_DIM ** -0.5

HG_HEADS = 8
HG_DK = 128
HG_DV = 128
HG_WK = HG_HEADS * HG_DK
HG_WV = HG_HEADS * HG_DV

D_FF = 5632
CONV_W = 3
N_MOD = 6
EPS = 1e-6

IN_SIZES = (ATT_W, ATT_W, ATT_W, HG_WK, HG_WK, HG_WV, HG_WV, D_MODEL, D_MODEL)
IN_TOTAL = sum(IN_SIZES)
IN_OFFS = tuple(int(s) for s in np.cumsum((0,) + IN_SIZES)[:-1])
OFF_QA, OFF_KA, OFF_VA, OFF_QH, OFF_FH, OFF_IH, OFF_GH, OFF_GA, OFF_GB = IN_OFFS

V7X_VMEM_BYTES = 64 * 1024 * 1024
SUBLANES = 8
LANES = 128
DEC_PAD = SUBLANES

COL_BLK = 512
HG_CHUNK = 128
ATT_QB = 128
ATT_SB = 2048


def _vmem_limit(nbytes):
    return int(min(V7X_VMEM_BYTES - 8 * 1024 * 1024, max(nbytes, 16 * 1024 * 1024)))


def _silu(x):
    return x * jax.nn.sigmoid(x)


def _unroll_for(trips, max_unroll=4):
    return next(u for u in range(min(max_unroll, trips), 0, -1) if trips % u == 0)


def _ada_kernel(c_ref, w_ref, b_ref, o_ref):
    c = c_ref[...]
    s = _silu(c).astype(BF16)
    o_ref[...] = jnp.dot(s, w_ref[...].astype(BF16), preferred_element_type=F32) + b_ref[...]


def _ada(c_all, w_ada, b_ada):
    rows, d = c_all.shape
    n = w_ada.shape[1]
    tn = 1024
    return pl.pallas_call(
        _ada_kernel,
        out_shape=jax.ShapeDtypeStruct((rows, n), F32),
        grid=(n // tn,),
        in_specs=[pl.BlockSpec((rows, d), lambda j: (0, 0)),
                  pl.BlockSpec((d, tn), lambda j: (0, j)),
                  pl.BlockSpec((1, tn), lambda j: (0, j))],
        out_specs=pl.BlockSpec((rows, tn), lambda j: (0, j)),
        compiler_params=pltpu.CompilerParams(
            dimension_semantics=("arbitrary",),
            vmem_limit_bytes=_vmem_limit(2 * d * tn * 4 + d * tn * 2 + 4 * rows * (d + tn) * 4)),
        name="ada",
    )(c_all, w_ada, b_ada.reshape(1, n))


def _norm_mod_rows(x_ref, nw_ref, sc_ref, sh_ref, h_ref, tm, per_row):
    step = min(tm, 128)

    def body(r, carry):
        rs = pl.ds(pl.multiple_of(r * step, step), step)
        x = x_ref[rs, :]
        ms = jnp.mean(x * x, axis=-1, keepdims=True)
        y = x * lax.rsqrt(ms + EPS) * nw_ref[...]
        if per_row:
            h = y * (1.0 + sc_ref[rs, :]) + sh_ref[rs, :]
        else:
            h = y * (1.0 + sc_ref[...]) + sh_ref[...]
        h_ref[rs, :] = h.astype(BF16)
        return carry

    lax.fori_loop(0, tm // step, body, 0)


def _mod_spec(per_row, tm, d):
    if per_row:
        return pl.BlockSpec((tm, d), lambda i, j: (i, 0))
    return pl.BlockSpec((1, d), lambda i, j: (0, 0))


def _inproj_kernel(x_ref, sh_ref, sc_ref, nw_ref, w_ref, cos_ref, s1_ref, s2_ref, z_ref, h_scr,
                   *, tm, per_row, n_rope_blocks):
    j = pl.program_id(1)

    @pl.when(j == 0)
    def _():
        _norm_mod_rows(x_ref, nw_ref, sc_ref, sh_ref, h_scr, tm, per_row)

    z_ref[...] = jnp.dot(h_scr[...], w_ref[...], preferred_element_type=F32)

    @pl.when(j < n_rope_blocks)
    def _():
        step = min(tm, 256)

        def body(r, carry):
            rs = pl.ds(pl.multiple_of(r * step, step), step)
            c, s1, s2 = cos_ref[rs, :], s1_ref[rs, :], s2_ref[rs, :]
            for hh in range(COL_BLK // HEAD_DIM):
                sl = slice(hh * HEAD_DIM, (hh + 1) * HEAD_DIM)
                blk = z_ref[rs, sl]
                z_ref[rs, sl] = (blk * c + pltpu.roll(blk, HEAD_DIM - ROT_DIM // 2, 1) * s1
                                 + pltpu.roll(blk, ROT_DIM // 2, 1) * s2)
            return carry

        lax.fori_loop(0, tm // step, body, 0)


def _inproj(x2d, sh, sc, nw, w_in_bf, rope, tm):
    rows, d = x2d.shape
    n = w_in_bf.shape[1]
    per_row = sh.shape[0] != 1
    cos_t, s1_t, s2_t = rope
    tn = COL_BLK
    kern = functools.partial(_inproj_kernel, tm=tm, per_row=per_row,
                             n_rope_blocks=(2 * ATT_W) // tn)
    tab = pl.BlockSpec((tm, HEAD_DIM), lambda i, j: (i, 0))
    vm = (2 * tm * d * 4 + tm * d * 2 + 2 * d * tn * 2 + 6 * tm * HEAD_DIM * 4 + 4 * tm * tn * 4
          + (4 * tm * d * 4 if per_row else 0) + 4 * 1024 * 1024)
    return pl.pallas_call(
        kern,
        out_shape=jax.ShapeDtypeStruct((rows, n), F32),
        grid=(rows // tm, n // tn),
        in_specs=[pl.BlockSpec((tm, d), lambda i, j: (i, 0)),
                  _mod_spec(per_row, tm, d), _mod_spec(per_row, tm, d),
                  pl.BlockSpec((1, d), lambda i, j: (0, 0)),
                  pl.BlockSpec((d, tn), lambda i, j: (0, j)),
                  tab, tab, tab],
        out_specs=pl.BlockSpec((tm, tn), lambda i, j: (i, j)),
        scratch_shapes=[pltpu.VMEM((tm, d), BF16)],
        compiler_params=pltpu.CompilerParams(
            dimension_semantics=("arbitrary", "arbitrary"),
            vmem_limit_bytes=_vmem_limit(vm)),
        name="inproj",
    )(x2d, sh, sc, nw, w_in_bf, cos_t, s1_t, s2_t)


def _rope_tables(pos):
    half = ROT_DIM // 2
    inv_freq = ROPE_THETA ** (-jnp.arange(half, dtype=F32) * 2.0 / ROT_DIM)
    ang = pos.astype(F32)[:, None] * inv_freq[None, :]
    cos, sin = jnp.cos(ang), jnp.sin(ang)
    rows = pos.shape[0]
    ones = jnp.ones((rows, HEAD_DIM - ROT_DIM), F32)
    zer = jnp.zeros((rows, HEAD_DIM - ROT_DIM), F32)
    zh = jnp.zeros((rows, half), F32)
    c = jnp.concatenate([cos, cos, ones], axis=1)
    s1 = jnp.concatenate([-sin, zh, zer], axis=1)
    s2 = jnp.concatenate([zh, sin, zer], axis=1)
    return c, s1, s2


def _attn_tile(q, ka, kb, va, vb, valid):
    k = jnp.concatenate([ka, kb], axis=0).astype(BF16)
    v = jnp.concatenate([va, vb], axis=0).astype(BF16)
    s = lax.dot_general((q * ATT_SCALE).astype(BF16), k, (((1,), (1,)), ((), ())),
                        preferred_element_type=F32)
    s = jnp.where(valid, s, -jnp.inf)
    m = jnp.max(s, axis=1, keepdims=True)
    p = jnp.exp(s - m)
    l = jnp.sum(p, axis=1, keepdims=True)
    o = jnp.dot(p.astype(BF16), v, preferred_element_type=F32) / l
    return o, m + jnp.log(l)


def _attn_kernel(*refs):
    q_refs = refs[0:3]
    kc_refs = refs[3:6]
    kp_refs = refs[6:9]
    vc_refs = refs[9:12]
    vp_refs = refs[12:15]
    o_ref = refs[15]
    og_scr, lse_scr = refs[16], refs[17]
    n = pl.program_id(0)

    row = lax.broadcasted_iota(jnp.int32, (ATT_QB, 2 * ATT_QB), 0)
    col = lax.broadcasted_iota(jnp.int32, (ATT_QB, 2 * ATT_QB), 1)
    band = (col >= row) & (col <= row + ATT_QB)
    band_first = band & (col >= jnp.where(n > 0, 0, ATT_QB))

    for g, (_, dil) in enumerate(ATT_GROUPS):
        q_ref, kc_ref, kp_ref, vc_ref, vp_ref = q_refs[g], kc_refs[g], kp_refs[g], vc_refs[g], vp_refs[g]
        nq = ATT_SB // (dil * ATT_QB)

        def rows(start, count):
            if dil == 1:
                return pl.ds(pl.multiple_of(start, ATT_QB), count)
            return pl.ds(start, count, stride=dil)

        def put(g_, tok0, o, lse):
            og_scr[g_, rows(tok0, ATT_QB), :] = o
            lse_scr[g_, rows(tok0, ATT_QB), :] = jnp.broadcast_to(lse, (ATT_QB, HEAD_DIM))

        def first_body(r, carry):
            o, lse = _attn_tile(q_ref[rows(r, ATT_QB), :],
                                kp_ref[rows(r, ATT_QB), :], kc_ref[rows(r, ATT_QB), :],
                                vp_ref[rows(r, ATT_QB), :], vc_ref[rows(r, ATT_QB), :],
                                band_first)
            put(g, r, o, lse)
            return carry

        lax.fori_loop(0, dil, first_body, 0, unroll=_unroll_for(dil))

        if nq > 1:
            def rest_body(t, carry):
                r = t // (nq - 1)
                u = t % (nq - 1) + 1
                q0 = r + dil * (u * ATT_QB)
                k0 = r + dil * ((u - 1) * ATT_QB)
                k1 = r + dil * (u * ATT_QB)
                o, lse = _attn_tile(q_ref[rows(q0, ATT_QB), :],
                                    kc_ref[rows(k0, ATT_QB), :], kc_ref[rows(k1, ATT_QB), :],
                                    vc_ref[rows(k0, ATT_QB), :], vc_ref[rows(k1, ATT_QB), :],
                                    band)
                put(g, q0, o, lse)
                return carry

            lax.fori_loop(0, dil * (nq - 1), rest_body, 0, unroll=_unroll_for(dil * (nq - 1)))

    def merge_body(c, carry):
        rs = pl.ds(pl.multiple_of(c * 256, 256), 256)
        l0, l1, l2 = lse_scr[0, rs, :], lse_scr[1, rs, :], lse_scr[2, rs, :]
        m = jnp.maximum(jnp.maximum(l0, l1), l2)
        e0, e1, e2 = jnp.exp(l0 - m), jnp.exp(l1 - m), jnp.exp(l2 - m)
        num = e0 * og_scr[0, rs, :] + e1 * og_scr[1, rs, :] + e2 * og_scr[2, rs, :]
        o_ref[rs, :] = num / (e0 + e1 + e2)
        return carry

    lax.fori_loop(0, ATT_SB // 256, merge_body, 0)


def _attention_prompt(z):
    t = z.shape[0]
    nsb = t // ATT_SB
    in_specs, args = [], []

    def cur(colblk):
        return pl.BlockSpec((ATT_SB, HEAD_DIM), lambda n, h, c=colblk: (n, c + h))

    def prev(colblk, dil):
        rows_p = dil * ATT_QB
        per = ATT_SB // rows_p
        return pl.BlockSpec((rows_p, HEAD_DIM),
                            lambda n, h, c=colblk, per=per: (jnp.maximum(n * per - 1, 0), c + h))

    for base in (OFF_QA,):
        for g in range(3):
            in_specs.append(cur(base // HEAD_DIM + g * H_G)); args.append(z)
    for base in (OFF_KA, OFF_VA):
        for g in range(3):
            in_specs.append(cur(base // HEAD_DIM + g * H_G)); args.append(z)
        for g, (_, dil) in enumerate(ATT_GROUPS):
            in_specs.append(prev(base // HEAD_DIM + g * H_G, dil)); args.append(z)
    blk = ATT_SB * HEAD_DIM * 4
    prev_rows = sum(d * ATT_QB for _, d in ATT_GROUPS)
    vm = 2 * (9 * blk + 2 * prev_rows * HEAD_DIM * 4) + 2 * blk + 6 * blk + 8 * 1024 * 1024
    return pl.pallas_call(
        _attn_kernel,
        out_shape=jax.ShapeDtypeStruct((t, ATT_OUT), F32),
        grid=(nsb, H_G),
        in_specs=in_specs,
        out_specs=pl.BlockSpec((ATT_SB, HEAD_DIM), lambda n, h: (n, h)),
        scratch_shapes=[pltpu.VMEM((3, ATT_SB, HEAD_DIM), F32),
                        pltpu.VMEM((3, ATT_SB, HEAD_DIM), F32)],
        compiler_params=pltpu.CompilerParams(
            dimension_semantics=("arbitrary", "arbitrary"),
            vmem_limit_bytes=_vmem_limit(vm)),
        name="attn_prompt",
    )(*args)


def _attn_dec_kernel(q_ref, kn_ref, vn_ref, c0_ref, c1_ref, c2_ref, o_ref, n0_ref, n1_ref, n2_ref):
    caches = (c0_ref, c1_ref, c2_ref)
    news = (n0_ref, n1_ref, n2_ref)
    rows_per_pos = 2 * H_G
    zpad = jnp.zeros((LANES - DEC_PAD, HEAD_DIM), F32)

    for g in range(len(ATT_GROUPS)):
        c_ref, n_ref = caches[g], news[g]
        n_rows = c_ref.shape[1]
        keep = n_rows - DEC_SEQ * rows_per_pos
        n_ref[0, 0:keep, :] = c_ref[0, DEC_SEQ * rows_per_pos:n_rows, :]
        for t in range(DEC_SEQ):
            for h in range(H_G):
                hs = slice((g * H_G + h) * HEAD_DIM, (g * H_G + h + 1) * HEAD_DIM)
                r = keep + t * rows_per_pos + h
                n_ref[0, r:r + 1, :] = kn_ref[0, t:t + 1, hs]
                n_ref[0, r + H_G:r + H_G + 1, :] = vn_ref[0, t:t + 1, hs]

    for h in range(H_G):
        outs, lses = [], []
        for g, (win, dil) in enumerate(ATT_GROUPS):
            c_ref = caches[g]
            p_len = c_ref.shape[1] // rows_per_pos
            hs = slice((g * H_G + h) * HEAD_DIM, (g * H_G + h + 1) * HEAD_DIM)
            q = (q_ref[0, :, hs] * ATT_SCALE).astype(BF16)
            k = jnp.concatenate([c_ref[0, pl.ds(h, p_len, stride=rows_per_pos), :], kn_ref[0, :, hs], zpad],
                                axis=0).astype(BF16)
            v = jnp.concatenate([c_ref[0, pl.ds(H_G + h, p_len, stride=rows_per_pos), :],
                                 vn_ref[0, :, hs], zpad], axis=0).astype(BF16)
            s = lax.dot_general(q, k, (((1,), (1,)), ((), ())), preferred_element_type=F32)
            tq = lax.broadcasted_iota(jnp.int32, s.shape, 0) & (DEC_SEQ - 1)
            col = lax.broadcasted_iota(jnp.int32, s.shape, 1)
            delta = p_len + tq - col
            valid = ((delta >= 0) & ((delta & (dil - 1)) == 0) & (delta <= win)
                     & (col < p_len + DEC_SEQ))
            s = jnp.where(valid, s, -jnp.inf)
            m = jnp.max(s, axis=1, keepdims=True)
            p = jnp.exp(s - m)
            l = jnp.sum(p, axis=1, keepdims=True)
            outs.append(jnp.dot(p.astype(BF16), v, preferred_element_type=F32) / l)
            lses.append(m + jnp.log(l))
        m = jnp.maximum(jnp.maximum(lses[0], lses[1]), lses[2])
        e = [jnp.exp(x - m) for x in lses]
        num = e[0] * outs[0] + e[1] * outs[1] + e[2] * outs[2]
        o_ref[0, :, h * HEAD_DIM:(h + 1) * HEAD_DIM] = num / (e[0] + e[1] + e[2])


def _attention_decode(z8, caches):
    b = z8.shape[0]
    qspec = lambda c: pl.BlockSpec((1, DEC_PAD, ATT_W), lambda i, c=c: (i, 0, c))
    cspecs = [pl.BlockSpec((1, c.shape[1], c.shape[2]), lambda i: (i, 0, 0)) for c in caches]
    vm = 4 * sum(c.shape[1] * c.shape[2] * 4 for c in caches) + 12 * 1024 * 1024
    return pl.pallas_call(
        _attn_dec_kernel,
        out_shape=[jax.ShapeDtypeStruct((b, DEC_PAD, ATT_OUT), F32)]
        + [jax.ShapeDtypeStruct(c.shape, F32) for c in caches],
        grid=(b,),
        in_specs=[qspec(0), qspec(1), qspec(2)] + cspecs,
        out_specs=[pl.BlockSpec((1, DEC_PAD, ATT_OUT), lambda i: (i, 0, 0))] + cspecs,
        compiler_params=pltpu.CompilerParams(
            dimension_semantics=("arbitrary",),
            vmem_limit_bytes=_vmem_limit(vm)),
        name="attn_decode",
    )(z8, z8, z8, *caches)


def _hgrn_consts(c):
    t = np.arange(c)[:, None]
    u = np.arange(c)[None, :]
    seg = [(u <= t)]
    cross = []
    m = c // 2
    while m >= 1:
        blk_t, off_t = t // (2 * m), t % (2 * m)
        piv = blk_t * 2 * m + m - 1
        upper = off_t >= m
        seg.append(np.where(upper, (u > piv) & (u <= t), (u > t) & (u <= piv)))
        cross.append((t // (2 * m) == u // (2 * m)) & (t % (2 * m) >= m) & (u % (2 * m) < m))
        m //= 2
    seg.append(u > t)
    cross.append(t == u)
    return (np.stack(seg).astype(np.float32).reshape(-1, c),
            np.stack(cross).astype(np.float32))


def _hgrn_kernel(q_ref, f_ref, i_ref, g_ref, lb_ref, nw_ref, seg_ref, cross_ref, o_ref, s_out_ref, s_scr,
                 *, tb, n_lvl):
    n = pl.program_id(1)
    c = HG_CHUNK

    @pl.when(n == 0)
    def _():
        s_scr[...] = jnp.zeros_like(s_scr)

    lb = lb_ref[...]
    mx = jnp.maximum(lb[0:1, :], lb[1:2, :])
    e0, e1 = jnp.exp(lb[0:1, :] - mx), jnp.exp(lb[1:2, :] - mx)
    lower = e0 / (e0 + e1)
    nw = nw_ref[...]
    seg = seg_ref[...]

    chunks = range(tb // c)
    nt = (((1,), (1,)), ((), ()))
    qs, ks, vs, ws = [], [], [], []
    for ch in chunks:
        rs = slice(ch * c, (ch + 1) * c)
        fg = lower + (1.0 - lower) * jax.nn.sigmoid(f_ref[rs, :])
        lg = jnp.log(fg)
        lg_hi = lg.astype(BF16)
        lg_lo = (lg - lg_hi.astype(F32)).astype(BF16)
        dsum = (jnp.dot(seg, lg_hi, preferred_element_type=F32)
                + jnp.dot(seg, lg_lo, preferred_element_type=F32))
        ws.append(jnp.exp(dsum))
        qs.append(_silu(q_ref[rs, :]))
        ks.append(1.0 - fg)
        vs.append(i_ref[rs, :].astype(BF16))
    atts = [cross_ref[n_lvl] * lax.dot_general(qs[ch].astype(BF16), ks[ch].astype(BF16), nt,
                                               preferred_element_type=F32) for ch in chunks]
    for lv in range(n_lvl):
        for ch in chunks:
            wl = ws[ch][(1 + lv) * c:(2 + lv) * c]
            atts[ch] = atts[ch] + cross_ref[lv] * lax.dot_general(
                (qs[ch] * wl).astype(BF16), (ks[ch] * wl).astype(BF16), nt, preferred_element_type=F32)
    o_intra = [jnp.dot(atts[ch].astype(BF16), vs[ch], preferred_element_type=F32) for ch in chunks]
    q_dec = [(qs[ch] * ws[ch][0:c]).astype(BF16) for ch in chunks]
    k_end_t = [(ks[ch] * ws[ch][(n_lvl + 1) * c:(n_lvl + 2) * c]).T.astype(BF16) for ch in chunks]
    decay_col = [jnp.broadcast_to(ws[ch][c - 1:c, :], (HG_DK, HG_DV)).T for ch in chunks]
    kv = [jnp.dot(k_end_t[ch], vs[ch], preferred_element_type=F32) for ch in chunks]
    s_cur = s_scr[...]
    for ch in chunks:
        rs = slice(ch * c, (ch + 1) * c)
        o = o_intra[ch] + jnp.dot(q_dec[ch], s_cur.astype(BF16), preferred_element_type=F32)
        s_cur = decay_col[ch] * s_cur + kv[ch]
        ms = jnp.mean(o * o, axis=-1, keepdims=True)
        o_ref[rs, :] = o * lax.rsqrt(ms + EPS) * nw * _silu(g_ref[rs, :])
    s_scr[...] = s_cur

    @pl.when(n == pl.num_programs(1) - 1)
    def _():
        s_out_ref[0] = s_scr[...]


def _hgrn_prompt(z, hg_lb, hg_norm_w, tb):
    t = z.shape[0]
    seg_np, cross_np = _hgrn_consts(HG_CHUNK)
    n_lvl = cross_np.shape[0] - 1
    seg = jnp.asarray(seg_np, dtype=BF16)
    cross = jnp.asarray(cross_np, dtype=F32)
    col = lambda off: pl.BlockSpec((tb, HG_DK), lambda h, n, o=off // HG_DK: (n, o + h))
    kern = functools.partial(_hgrn_kernel, tb=tb, n_lvl=n_lvl)
    vm = 2 * 5 * tb * HG_DK * 4 + 2 * seg_np.size * 2 + 2 * cross_np.size * 4 + 16 * 1024 * 1024
    return pl.pallas_call(
        kern,
        out_shape=(jax.ShapeDtypeStruct((t, HG_WV), F32),
                   jax.ShapeDtypeStruct((HG_HEADS, HG_DK, HG_DV), F32)),
        grid=(HG_HEADS, t // tb),
        in_specs=[col(OFF_QH), col(OFF_FH), col(OFF_IH), col(OFF_GH),
                  pl.BlockSpec((2, HG_DK), lambda h, n: (0, h)),
                  pl.BlockSpec((1, HG_DV), lambda h, n: (0, 0)),
                  pl.BlockSpec(seg_np.shape, lambda h, n: (0, 0)),
                  pl.BlockSpec(cross_np.shape, lambda h, n: (0, 0, 0))],
        out_specs=(pl.BlockSpec((tb, HG_DV), lambda h, n: (n, h)),
                   pl.BlockSpec((1, HG_DK, HG_DV), lambda h, n: (h, 0, 0))),
        scratch_shapes=[pltpu.VMEM((HG_DK, HG_DV), F32)],
        compiler_params=pltpu.CompilerParams(
            dimension_semantics=("arbitrary", "arbitrary"),
            vmem_limit_bytes=_vmem_limit(vm)),
        name="hgrn_prompt",
    )(z, z, z, z, hg_lb, hg_norm_w, seg, cross)


def _hgrn_dec_kernel(q_ref, f_ref, i_ref, g_ref, lb_ref, nw_ref, s_ref, o_ref, s_out_ref, *, heads):
    lb = lb_ref[...]
    mx = jnp.maximum(lb[0:1, :], lb[1:2, :])
    e0, e1 = jnp.exp(lb[0:1, :] - mx), jnp.exp(lb[1:2, :] - mx)
    lower = e0 / (e0 + e1)
    nw = nw_ref[...]
    zpad = jnp.zeros((HG_DK - DEC_PAD, HG_DK), F32)

    def cols(x):
        return jnp.concatenate([x, zpad], axis=0).T

    for h in range(heads):
        hs = slice(h * HG_DK, (h + 1) * HG_DK)
        q = _silu(q_ref[0, :, hs])
        fg = lower[:, hs] + (1.0 - lower[:, hs]) * jax.nn.sigmoid(f_ref[0, :, hs])
        v = i_ref[0, :, hs]
        q_t, f_t, k_t = cols(q), cols(fg), cols(1.0 - fg)
        s = s_ref[0, h]
        row = lax.broadcasted_iota(jnp.int32, (DEC_PAD, HG_DV), 0)
        o = jnp.zeros((DEC_PAD, HG_DV), F32)
        for t in range(DEC_SEQ):
            bc = lambda a: jnp.broadcast_to(a[:, t:t + 1], (HG_DK, HG_DV))
            s = bc(f_t) * s + bc(k_t) * v[t:t + 1, :]
            o = jnp.where(row == t, jnp.sum(s * bc(q_t), axis=0, keepdims=True), o)
        ms = jnp.mean(o * o, axis=-1, keepdims=True)
        o_ref[0, :, hs] = o * lax.rsqrt(ms + EPS) * nw * _silu(g_ref[0, :, hs])
        s_out_ref[0, h] = s


def _hgrn_decode(z8, hg_lb, hg_norm_w, state):
    b = z8.shape[0]
    heads = COL_BLK // HG_DK
    nhb = HG_HEADS // heads
    col = lambda off: pl.BlockSpec((1, DEC_PAD, COL_BLK), lambda i, j, o=off // COL_BLK: (i, 0, o + j))
    kern = functools.partial(_hgrn_dec_kernel, heads=heads)
    sspec = pl.BlockSpec((1, heads, HG_DK, HG_DV), lambda i, j: (i, j, 0, 0))
    return pl.pallas_call(
        kern,
        out_shape=(jax.ShapeDtypeStruct((b, DEC_PAD, HG_WV), F32),
                   jax.ShapeDtypeStruct(state.shape, F32)),
        grid=(b, nhb),
        in_specs=[col(OFF_QH), col(OFF_FH), col(OFF_IH), col(OFF_GH),
                  pl.BlockSpec((2, COL_BLK), lambda i, j: (0, j)),
                  pl.BlockSpec((1, HG_DV), lambda i, j: (0, 0)),
                  sspec],
        out_specs=(pl.BlockSpec((1, DEC_PAD, COL_BLK), lambda i, j: (i, 0, j)), sspec),
        compiler_params=pltpu.CompilerParams(
            dimension_semantics=("arbitrary", "arbitrary"),
            vmem_limit_bytes=_vmem_limit(0)),
        name="hgrn_decode",
    )(z8, z8, z8, z8, hg_lb, hg_norm_w, state)


def _merge_kernel(att_ref, hg_ref, ga_ref, gb_ref, x_ref, g1_ref, wpa_ref, wpb_ref, wo_ref, o_ref,
                  att_scr, hg_scr, acc_scr):
    n = pl.program_id(1)

    @pl.when(n == 0)
    def _():
        att_scr[...] = att_ref[...].astype(BF16)
        hg_scr[...] = hg_ref[...].astype(BF16)
        acc_scr[...] = jnp.zeros_like(acc_scr)

    ya = jnp.dot(att_scr[...], wpa_ref[...], preferred_element_type=F32)
    yb = jnp.dot(hg_scr[...], wpb_ref[...], preferred_element_type=F32)
    ymix = jax.nn.sigmoid(ga_ref[...]) * ya + jax.nn.sigmoid(gb_ref[...]) * yb
    acc_scr[...] += jnp.dot(ymix.astype(BF16), wo_ref[...], preferred_element_type=F32)

    @pl.when(n == pl.num_programs(1) - 1)
    def _():
        o_ref[...] = x_ref[...] + g1_ref[...] * acc_scr[...]


def _merge(att, hg, z, x2d, g1, wpa_bf, wpb_bf, wo_bf, tm):
    rows, d = x2d.shape
    per_row = g1.shape[0] != 1
    tn = COL_BLK
    nn = d // tn
    vm = (2 * tm * (ATT_OUT + HG_WV + 2 * tn + 2 * d) * 4 + tm * (ATT_OUT + HG_WV) * 2 + tm * d * 4
          + 2 * (ATT_OUT * tn + HG_WV * tn + tn * d) * 2 + (2 * tm * d * 4 if per_row else 0)
          + 6 * tm * tn * 4 + 4 * 1024 * 1024)
    return pl.pallas_call(
        _merge_kernel,
        out_shape=jax.ShapeDtypeStruct((rows, d), F32),
        grid=(rows // tm, nn),
        in_specs=[pl.BlockSpec((tm, ATT_OUT), lambda i, n: (i, 0)),
                  pl.BlockSpec((tm, HG_WV), lambda i, n: (i, 0)),
                  pl.BlockSpec((tm, tn), lambda i, n: (i, OFF_GA // tn + n)),
                  pl.BlockSpec((tm, tn), lambda i, n: (i, OFF_GB // tn + n)),
                  pl.BlockSpec((tm, d), lambda i, n: (i, 0)),
                  _mod_spec(per_row, tm, d),
                  pl.BlockSpec((ATT_OUT, tn), lambda i, n: (0, n)),
                  pl.BlockSpec((HG_WV, tn), lambda i, n: (0, n)),
                  pl.BlockSpec((tn, d), lambda i, n: (n, 0))],
        out_specs=pl.BlockSpec((tm, d), lambda i, n: (i, 0)),
        scratch_shapes=[pltpu.VMEM((tm, ATT_OUT), BF16), pltpu.VMEM((tm, HG_WV), BF16),
                        pltpu.VMEM((tm, d), F32)],
        compiler_params=pltpu.CompilerParams(
            dimension_semantics=("arbitrary", "arbitrary"),
            vmem_limit_bytes=_vmem_limit(vm)),
        name="merge",
    )(att, hg, z, z, x2d, g1, wpa_bf, wpb_bf, wo_bf)


def _ffn_kernel(x_ref, sh_ref, sc_ref, g2_ref, nw_ref, nf_ref, wa_ref, wb_ref, wd_ref, cw_ref, cb_ref,
                p1_ref, p2_ref, y_ref, tail_ref, h_scr, acc_scr, carry_scr,
                *, tm, ta, per_row, seq_rows):
    i = pl.program_id(0)
    f = pl.program_id(1)

    @pl.when(f == 0)
    def _():
        _norm_mod_rows(x_ref, nw_ref, sc_ref, sh_ref, h_scr, tm, per_row)
        acc_scr[...] = jnp.zeros_like(acc_scr)

    h = h_scr[...]
    a = jnp.dot(h, wa_ref[...], preferred_element_type=F32)
    tail_ref[...] = a[tm - ta:, :]
    row = lax.broadcasted_iota(jnp.int32, a.shape, 0)
    r1 = pltpu.roll(a, 1, 0)
    r2 = pltpu.roll(a, 2, 0)
    if seq_rows is None:
        @pl.when(i == 0)
        def _():
            carry_scr[f] = p1_ref[...]
        prev = carry_scr[f]
        a1 = jnp.where(row == 0, prev[SUBLANES - 1:SUBLANES, :], r1)
        a2 = jnp.where(row == 0, prev[SUBLANES - 2:SUBLANES - 1, :],
                       jnp.where(row == 1, prev[SUBLANES - 1:SUBLANES, :], r2))
        carry_scr[f] = a[tm - SUBLANES:, :]
    else:
        t = row & (seq_rows - 1)
        a1 = jnp.where(t == 0, p1_ref[...], r1)
        a2 = jnp.where(t <= 1, p2_ref[...], r2)
    cw = cw_ref[...]
    u = cb_ref[...] + a2 * cw[0:1, :] + a1 * cw[1:2, :] + a * cw[2:3, :]
    y = _silu(u) * jnp.dot(h, wb_ref[...], preferred_element_type=F32)
    acc_scr[...] += jnp.dot(y.astype(BF16), wd_ref[...], preferred_element_type=F32)

    @pl.when(f == pl.num_programs(1) - 1)
    def _():
        step = min(tm, 128)

        def body(r, carry):
            rs = pl.ds(pl.multiple_of(r * step, step), step)
            g2 = g2_ref[rs, :] if per_row else g2_ref[...]
            x2 = x_ref[rs, :] + g2 * acc_scr[rs, :]
            ms = jnp.mean(x2 * x2, axis=-1, keepdims=True)
            y_ref[rs, :] = x2 * lax.rsqrt(ms + EPS) * nf_ref[...]
            return carry

        lax.fori_loop(0, tm // step, body, 0)


def _ffn(x1, sh, sc, g2, nw, nf, wa_bf, wb_bf, wd_bf, conv_w, conv_b, p1, p2, tm, seq_rows):
    rows, d = x1.shape
    dff = wa_bf.shape[1]
    per_row = sh.shape[0] != 1
    tf = COL_BLK
    nf_blk = dff // tf
    ta = SUBLANES if seq_rows is None else tm
    prow = SUBLANES if seq_rows is None else tm
    kern = functools.partial(_ffn_kernel, tm=tm, ta=ta, per_row=per_row, seq_rows=seq_rows)
    pspec = pl.BlockSpec((prow, tf), lambda i, f: (0 if seq_rows is None else i, f))
    vm = (4 * tm * d * 4 + tm * d * 2 + tm * d * 4 + 2 * 3 * d * tf * 2 + 10 * tm * tf * 4
          + (6 * tm * d * 4 if per_row else 0) + 8 * 1024 * 1024)
    return pl.pallas_call(
        kern,
        out_shape=(jax.ShapeDtypeStruct((rows, d), F32),
                   jax.ShapeDtypeStruct((ta * (rows // tm), dff), F32)),
        grid=(rows // tm, nf_blk),
        in_specs=[pl.BlockSpec((tm, d), lambda i, f: (i, 0)),
                  _mod_spec(per_row, tm, d), _mod_spec(per_row, tm, d), _mod_spec(per_row, tm, d),
                  pl.BlockSpec((1, d), lambda i, f: (0, 0)),
                  pl.BlockSpec((1, d), lambda i, f: (0, 0)),
                  pl.BlockSpec((d, tf), lambda i, f: (0, f)),
                  pl.BlockSpec((d, tf), lambda i, f: (0, f)),
                  pl.BlockSpec((tf, d), lambda i, f: (f, 0)),
                  pl.BlockSpec((CONV_W, tf), lambda i, f: (0, f)),
                  pl.BlockSpec((1, tf), lambda i, f: (0, f)),
                  pspec, pspec],
        out_specs=(pl.BlockSpec((tm, d), lambda i, f: (i, 0)),
                   pl.BlockSpec((ta, tf), lambda i, f: (i, f))),
        scratch_shapes=[pltpu.VMEM((tm, d), BF16), pltpu.VMEM((tm, d), F32),
                        pltpu.VMEM((nf_blk, SUBLANES, tf), F32)],
        compiler_params=pltpu.CompilerParams(
            dimension_semantics=("arbitrary", "arbitrary"),
            vmem_limit_bytes=_vmem_limit(vm)),
        name="ffn",
    )(x1, sh, sc, g2, nw, nf, wa_bf, wb_bf, wd_bf, conv_w, conv_b, p1, p2)


def kernel(x_prompt, x_sample, cache_kv_w128, cache_kv_w512, cache_kv_w2048, state_hgrn, state_conv,
           c_prompt, c_sample, w_ada, b_ada, norm1_w, w_in, hg_lb, hg_norm_w, w_pa, w_pb, w_o,
           norm2_w, w_ffn_a, w_ffn_b, conv_w, conv_b, w_ffn_down, norm_f_w):
    d = D_MODEL
    bp, t, _ = x_prompt.shape
    bs, ts, _ = x_sample.shape
    assert bp == 1 and ts == DEC_SEQ and w_ada.shape[0] == 1
    rows_s = bs * ts

    w_in_bf = w_in[0].astype(BF16)
    wpa_bf, wpb_bf, wo_bf = w_pa[0].astype(BF16), w_pb[0].astype(BF16), w_o[0].astype(BF16)
    wa_bf, wb_bf, wd_bf = w_ffn_a[0].astype(BF16), w_ffn_b[0].astype(BF16), w_ffn_down[0].astype(BF16)
    nw1, nw2, nwf = norm1_w[0].reshape(1, d), norm2_w[0].reshape(1, d), norm_f_w.reshape(1, d)
    hg_nw = hg_norm_w[0].reshape(1, HG_DV)
    cw, cb = conv_w[0], conv_b[0].reshape(1, D_FF)

    n_seq = bp + bs
    pad_rows = -n_seq % SUBLANES
    c_all = jnp.concatenate([c_prompt, c_sample, jnp.zeros((pad_rows, d), F32)], axis=0)
    mod = _ada(c_all, w_ada[0], b_ada[0])
    mod_p = [mod[0:bp, k * d:(k + 1) * d] for k in range(N_MOD)]
    mod_s = [jnp.repeat(mod[bp:n_seq, k * d:(k + 1) * d], ts, axis=0) for k in range(N_MOD)]

    xp = x_prompt.reshape(t, d)
    rope_p = _rope_tables(jnp.arange(t, dtype=jnp.int32))
    zp = _inproj(xp, mod_p[0], mod_p[1], nw1, w_in_bf, rope_p, tm=1024)
    att_p = _attention_prompt(zp)
    hg_p, s_p = _hgrn_prompt(zp, hg_lb, hg_nw, tb=512)
    x1p = _merge(att_p, hg_p, zp, xp, mod_p[2], wpa_bf, wpb_bf, wo_bf, tm=512)
    conv0 = jnp.zeros((SUBLANES, D_FF), F32)
    yp, tail_p = _ffn(x1p, mod_p[3], mod_p[4], mod_p[5], nw2, nwf, wa_bf, wb_bf, wd_bf, cw, cb,
                      conv0, conv0, tm=512, seq_rows=None)

    kv_p = []
    for g, (win, _) in enumerate(ATT_GROUPS):
        keep = min(win, t)
        ks = zp[t - keep:, OFF_KA + g * ATT_OUT:OFF_KA + (g + 1) * ATT_OUT]
        vs = zp[t - keep:, OFF_VA + g * ATT_OUT:OFF_VA + (g + 1) * ATT_OUT]
        kv_p.append(jnp.stack([ks.reshape(keep, H_G, HEAD_DIM), vs.reshape(keep, H_G, HEAD_DIM)],
                              axis=1)[None, None])
    hgrn_p = s_p[None, None]
    conv_p = tail_p[tail_p.shape[0] - (CONV_W - 1):][None, None]

    xs = x_sample.reshape(rows_s, d)
    pos_s = PAST_LEN + (jnp.arange(rows_s, dtype=jnp.int32) % ts)
    rope_s = _rope_tables(pos_s)
    zs = _inproj(xs, mod_s[0], mod_s[1], nw1, w_in_bf, rope_s, tm=rows_s)
    zs8 = jnp.pad(zs.reshape(bs, ts, IN_TOTAL), ((0, 0), (0, DEC_PAD - ts), (0, 0)))
    cache_in = (cache_kv_w128, cache_kv_w512, cache_kv_w2048)
    caches = [c.reshape(bs, c.shape[2] * 2 * H_G, HEAD_DIM) for c in cache_in]
    att_s8, *new_caches = _attention_decode(zs8, caches)
    att_s = att_s8[:, :ts].reshape(rows_s, ATT_OUT)
    kv_s = [n.reshape(c.shape) for n, c in zip(new_caches, cache_in)]
    hg_s8, s_s = _hgrn_decode(zs8, hg_lb, hg_nw, state_hgrn[0])
    hg_s = hg_s8[:, :ts].reshape(rows_s, HG_WV)
    x1s = _merge(att_s, hg_s, zs, xs, mod_s[2], wpa_bf, wpb_bf, wo_bf, tm=rows_s)
    buf = state_conv[0]
    zrow = jnp.zeros((bs, 1, D_FF), F32)
    p1 = jnp.concatenate([buf[:, 1:2], zrow, zrow, zrow], axis=1).reshape(rows_s, D_FF)
    p2 = jnp.concatenate([buf[:, 0:1], buf[:, 1:2], zrow, zrow], axis=1).reshape(rows_s, D_FF)
    ys, a_s = _ffn(x1s, mod_s[3], mod_s[4], mod_s[5], nw2, nwf, wa_bf, wb_bf, wd_bf, cw, cb,
                   p1, p2, tm=rows_s, seq_rows=ts)

    hgrn_s = s_s[None]
    conv_s = a_s.reshape(bs, ts, D_FF)[:, ts - (CONV_W - 1):][None]

    return (yp.reshape(bp, t, d), ys.reshape(bs, ts, d),
            kv_p[0], kv_p[1], kv_p[2], hgrn_p, conv_p,
            kv_s[0], kv_s[1], kv_s[2], hgrn_s, conv_s)
```

```python
import functools

import numpy as np
import jax
import jax.numpy as jnp
from jax import lax
from jax.experimental import pallas as pl
from jax.experimental.pallas import tpu as pltpu

F32 = jnp.float32
BF16 = jnp.bfloat16

D_MODEL = 2048
SEQ = 16384
DEC_BATCH = 32
DEC_SEQ = 4
PAST_LEN = 16384

HEAD_DIM = 128
ATT_GROUPS = ((128, 1), (512, 4), (2048, 16))
H_G = 4
N_ATT_HEADS = H_G * len(ATT_GROUPS)
ATT_W = N_ATT_HEADS * HEAD_DIM
ATT_OUT = H_G * HEAD_DIM
ROT_DIM = HEAD_DIM // 4
ROPE_THETA = 500000.0
ATT_SCALE = HEAD_DIM ** -0.5

HG_HEADS = 8
HG_DK = 128
HG_DV = 128
HG_WK = HG_HEADS * HG_DK
HG_WV = HG_HEADS * HG_DV

D_FF = 5632
CONV_W = 3
N_MOD = 6
EPS = 1e-6

IN_SIZES = (ATT_W, ATT_W, ATT_W, HG_WK, HG_WK, HG_WV, HG_WV, D_MODEL, D_MODEL)
IN_TOTAL = sum(IN_SIZES)
IN_OFFS = tuple(int(s) for s in np.cumsum((0,) + IN_SIZES)[:-1])
OFF_QA, OFF_KA, OFF_VA, OFF_QH, OFF_FH, OFF_IH, OFF_GH, OFF_GA, OFF_GB = IN_OFFS

V7X_VMEM_BYTES = 64 * 1024 * 1024
SUBLANES = 8
LANES = 128
DEC_PAD = SUBLANES

COL_BLK = 512
FFN_SPLIT = 256
HG_CHUNK = 128
ATT_QB = 128
ATT_SB = 2048


def _vmem_limit(nbytes):
    return int(min(V7X_VMEM_BYTES - 8 * 1024 * 1024, max(nbytes, 16 * 1024 * 1024)))


def _silu(x):
    return x * jax.nn.sigmoid(x)


def _unroll_for(trips, max_unroll=6):
    return next(u for u in range(min(max_unroll, trips), 0, -1) if trips % u == 0)


def _ada_kernel(c_ref, w_ref, b_ref, o_ref):
    c = c_ref[...]
    s = _silu(c).astype(BF16)
    o_ref[...] = jnp.dot(s, w_ref[...].astype(BF16), preferred_element_type=F32) + b_ref[...]


def _ada(c_all, w_ada, b_ada):
    rows, d = c_all.shape
    n = w_ada.shape[1]
    tn = 1024
    return pl.pallas_call(
        _ada_kernel,
        out_shape=jax.ShapeDtypeStruct((rows, n), F32),
        grid=(n // tn,),
        in_specs=[pl.BlockSpec((rows, d), lambda j: (0, 0)),
                  pl.BlockSpec((d, tn), lambda j: (0, j)),
                  pl.BlockSpec((1, tn), lambda j: (0, j))],
        out_specs=pl.BlockSpec((rows, tn), lambda j: (0, j)),
        compiler_params=pltpu.CompilerParams(
            dimension_semantics=("arbitrary",),
            vmem_limit_bytes=_vmem_limit(2 * d * tn * 4 + d * tn * 2 + 4 * rows * (d + tn) * 4)),
        name="ada",
    )(c_all, w_ada, b_ada.reshape(1, n))


def _norm_mod_rows(x_ref, nw_ref, sc_ref, sh_ref, h_ref, tm, per_row):
    step = min(tm, 128)

    def body(r, carry):
        rs = pl.ds(pl.multiple_of(r * step, step), step)
        x = x_ref[rs, :]
        ms = jnp.mean(x * x, axis=-1, keepdims=True)
        y = x * lax.rsqrt(ms + EPS) * nw_ref[...]
        if per_row:
            h = y * (1.0 + sc_ref[rs, :]) + sh_ref[rs, :]
        else:
            h = y * (1.0 + sc_ref[...]) + sh_ref[...]
        h_ref[rs, :] = h.astype(BF16)
        return carry

    lax.fori_loop(0, tm // step, body, 0)


def _mod_spec(per_row, tm, d):
    if per_row:
        return pl.BlockSpec((tm, d), lambda i, j: (i, 0))
    return pl.BlockSpec((1, d), lambda i, j: (0, 0))


def _inproj_kernel(x_ref, sh_ref, sc_ref, nw_ref, w_ref, cos_ref, s1_ref, s2_ref, z_ref, h_scr,
                   *, tm, per_row, n_rope_blocks):
    j = pl.program_id(1)

    @pl.when(j == 0)
    def _():
        _norm_mod_rows(x_ref, nw_ref, sc_ref, sh_ref, h_scr, tm, per_row)

    z_ref[...] = jnp.dot(h_scr[...], w_ref[...], preferred_element_type=F32)

    @pl.when(j < n_rope_blocks)
    def _():
        step = min(tm, 256)

        def body(r, carry):
            rs = pl.ds(pl.multiple_of(r * step, step), step)
            c, s1, s2 = cos_ref[rs, :], s1_ref[rs, :], s2_ref[rs, :]
            for hh in range(COL_BLK // HEAD_DIM):
                sl = slice(hh * HEAD_DIM, (hh + 1) * HEAD_DIM)
                blk = z_ref[rs, sl]
                z_ref[rs, sl] = (blk * c + pltpu.roll(blk, HEAD_DIM - ROT_DIM // 2, 1) * s1
                                 + pltpu.roll(blk, ROT_DIM // 2, 1) * s2)
            return carry

        lax.fori_loop(0, tm // step, body, 0)


def _inproj(x2d, sh, sc, nw, w_in_bf, rope, tm):
    rows, d = x2d.shape
    n = w_in_bf.shape[1]
    per_row = sh.shape[0] != 1
    cos_t, s1_t, s2_t = rope
    tn = COL_BLK
    kern = functools.partial(_inproj_kernel, tm=tm, per_row=per_row,
                             n_rope_blocks=(2 * ATT_W) // tn)
    tab = pl.BlockSpec((tm, HEAD_DIM), lambda i, j: (i, 0))
    vm = (2 * tm * d * 4 + tm * d * 2 + 2 * d * tn * 2 + 6 * tm * HEAD_DIM * 4 + 4 * tm * tn * 4
          + (4 * tm * d * 4 if per_row else 0) + 4 * 1024 * 1024)
    return pl.pallas_call(
        kern,
        out_shape=jax.ShapeDtypeStruct((rows, n), F32),
        grid=(rows // tm, n // tn),
        in_specs=[pl.BlockSpec((tm, d), lambda i, j: (i, 0)),
                  _mod_spec(per_row, tm, d), _mod_spec(per_row, tm, d),
                  pl.BlockSpec((1, d), lambda i, j: (0, 0)),
                  pl.BlockSpec((d, tn), lambda i, j: (0, j)),
                  tab, tab, tab],
        out_specs=pl.BlockSpec((tm, tn), lambda i, j: (i, j)),
        scratch_shapes=[pltpu.VMEM((tm, d), BF16)],
        compiler_params=pltpu.CompilerParams(
            dimension_semantics=("arbitrary", "arbitrary"),
            vmem_limit_bytes=_vmem_limit(vm)),
        name="inproj",
    )(x2d, sh, sc, nw, w_in_bf, cos_t, s1_t, s2_t)


def _rope_tables(pos):
    half = ROT_DIM // 2
    inv_freq = ROPE_THETA ** (-np.arange(half, dtype=np.float64) * 2.0 / ROT_DIM)
    ang = np.asarray(pos, dtype=np.float64)[:, None] * inv_freq[None, :]
    cos, sin = jnp.asarray(np.cos(ang), dtype=F32), jnp.asarray(np.sin(ang), dtype=F32)
    rows = pos.shape[0]
    ones = jnp.ones((rows, HEAD_DIM - ROT_DIM), F32)
    zer = jnp.zeros((rows, HEAD_DIM - ROT_DIM), F32)
    zh = jnp.zeros((rows, half), F32)
    c = jnp.concatenate([cos, cos, ones], axis=1)
    s1 = jnp.concatenate([-sin, zh, zer], axis=1)
    s2 = jnp.concatenate([zh, sin, zer], axis=1)
    return c, s1, s2


def _attn_tile(q, ka, kb, va, vb, valid):
    k = jnp.concatenate([ka, kb], axis=0).astype(BF16)
    v = jnp.concatenate([va, vb], axis=0).astype(BF16)
    s = lax.dot_general((q * ATT_SCALE).astype(BF16), k, (((1,), (1,)), ((), ())),
                        preferred_element_type=F32)
    s = jnp.where(valid, s, -jnp.inf)
    m = jnp.max(s, axis=1, keepdims=True)
    p = jnp.exp(s - m)
    l = jnp.sum(p, axis=1, keepdims=True)
    o = jnp.dot(p.astype(BF16), v, preferred_element_type=F32) / l
    return o, m + jnp.log(l)


def _attn_kernel(*refs):
    q_refs = refs[0:3]
    kc_refs = refs[3:6]
    kp_refs = refs[6:9]
    vc_refs = refs[9:12]
    vp_refs = refs[12:15]
    o_ref = refs[15]
    og_scr, lse_scr = refs[16], refs[17]
    n = pl.program_id(0)

    row = lax.broadcasted_iota(jnp.int32, (ATT_QB, 2 * ATT_QB), 0)
    col = lax.broadcasted_iota(jnp.int32, (ATT_QB, 2 * ATT_QB), 1)
    band = (col >= row) & (col <= row + ATT_QB)
    band_first = band & (col >= jnp.where(n > 0, 0, ATT_QB))

    for g, (_, dil) in enumerate(ATT_GROUPS):
        q_ref, kc_ref, kp_ref, vc_ref, vp_ref = q_refs[g], kc_refs[g], kp_refs[g], vc_refs[g], vp_refs[g]
        nq = ATT_SB // (dil * ATT_QB)

        def rows(start, count):
            if dil == 1:
                return pl.ds(pl.multiple_of(start, ATT_QB), count)
            return pl.ds(start, count, stride=dil)

        def put(g_, tok0, o, lse):
            og_scr[g_, rows(tok0, ATT_QB), :] = o
            lse_scr[g_, rows(tok0, ATT_QB), :] = jnp.broadcast_to(lse, (ATT_QB, HEAD_DIM))

        def first_body(r, carry):
            o, lse = _attn_tile(q_ref[rows(r, ATT_QB), :],
                                kp_ref[rows(r, ATT_QB), :], kc_ref[rows(r, ATT_QB), :],
                                vp_ref[rows(r, ATT_QB), :], vc_ref[rows(r, ATT_QB), :],
                                band_first)
            put(g, r, o, lse)
            return carry

        lax.fori_loop(0, dil, first_body, 0, unroll=_unroll_for(dil))

        if nq > 1:
            def rest_body(t, carry):
                r = t // (nq - 1)
                u = t % (nq - 1) + 1
                q0 = r + dil * (u * ATT_QB)
                k0 = r + dil * ((u - 1) * ATT_QB)
                k1 = r + dil * (u * ATT_QB)
                o, lse = _attn_tile(q_ref[rows(q0, ATT_QB), :],
                                    kc_ref[rows(k0, ATT_QB), :], kc_ref[rows(k1, ATT_QB), :],
                                    vc_ref[rows(k0, ATT_QB), :], vc_ref[rows(k1, ATT_QB), :],
                                    band)
                put(g, q0, o, lse)
                return carry

            lax.fori_loop(0, dil * (nq - 1), rest_body, 0, unroll=_unroll_for(dil * (nq - 1)))

    def merge_body(c, carry):
        rs = pl.ds(pl.multiple_of(c * 256, 256), 256)
        l0, l1, l2 = lse_scr[0, rs, :], lse_scr[1, rs, :], lse_scr[2, rs, :]
        m = jnp.maximum(jnp.maximum(l0, l1), l2)
        e0, e1, e2 = jnp.exp(l0 - m), jnp.exp(l1 - m), jnp.exp(l2 - m)
        num = e0 * og_scr[0, rs, :] + e1 * og_scr[1, rs, :] + e2 * og_scr[2, rs, :]
        o_ref[rs, :] = num / (e0 + e1 + e2)
        return carry

    lax.fori_loop(0, ATT_SB // 256, merge_body, 0)


def _attention_prompt(z):
    t = z.shape[0]
    nsb = t // ATT_SB
    in_specs, args = [], []

    def cur(colblk):
        return pl.BlockSpec((ATT_SB, HEAD_DIM), lambda n, h, c=colblk: (n, c + h))

    def prev(colblk, dil):
        rows_p = dil * ATT_QB
        per = ATT_SB // rows_p
        return pl.BlockSpec((rows_p, HEAD_DIM),
                            lambda n, h, c=colblk, per=per: (jnp.maximum(n * per - 1, 0), c + h))

    for base in (OFF_QA,):
        for g in range(3):
            in_specs.append(cur(base // HEAD_DIM + g * H_G)); args.append(z)
    for base in (OFF_KA, OFF_VA):
        for g in range(3):
            in_specs.append(cur(base // HEAD_DIM + g * H_G)); args.append(z)
        for g, (_, dil) in enumerate(ATT_GROUPS):
            in_specs.append(prev(base // HEAD_DIM + g * H_G, dil)); args.append(z)
    blk = ATT_SB * HEAD_DIM * 4
    prev_rows = sum(d * ATT_QB for _, d in ATT_GROUPS)
    vm = 2 * (9 * blk + 2 * prev_rows * HEAD_DIM * 4) + 2 * blk + 6 * blk + 8 * 1024 * 1024
    return pl.pallas_call(
        _attn_kernel,
        out_shape=jax.ShapeDtypeStruct((t, ATT_OUT), F32),
        grid=(nsb, H_G),
        in_specs=in_specs,
        out_specs=pl.BlockSpec((ATT_SB, HEAD_DIM), lambda n, h: (n, h)),
        scratch_shapes=[pltpu.VMEM((3, ATT_SB, HEAD_DIM), F32),
                        pltpu.VMEM((3, ATT_SB, HEAD_DIM), F32)],
        compiler_params=pltpu.CompilerParams(
            dimension_semantics=("arbitrary", "arbitrary"),
            vmem_limit_bytes=_vmem_limit(vm)),
        name="attn_prompt",
    )(*args)


def _attn_dec_kernel(q_ref, kn_ref, vn_ref, c0_ref, c1_ref, c2_ref, o_ref, n0_ref, n1_ref, n2_ref):
    caches = (c0_ref, c1_ref, c2_ref)
    news = (n0_ref, n1_ref, n2_ref)
    rows_per_pos = 2 * H_G
    zpad = jnp.zeros((LANES - DEC_PAD, HEAD_DIM), F32)

    for g in range(len(ATT_GROUPS)):
        c_ref, n_ref = caches[g], news[g]
        n_rows = c_ref.shape[1]
        keep = n_rows - DEC_SEQ * rows_per_pos
        n_ref[0, 0:keep, :] = c_ref[0, DEC_SEQ * rows_per_pos:n_rows, :]
        for t in range(DEC_SEQ):
            for h in range(H_G):
                hs = slice((g * H_G + h) * HEAD_DIM, (g * H_G + h + 1) * HEAD_DIM)
                r = keep + t * rows_per_pos + h
                n_ref[0, r:r + 1, :] = kn_ref[0, t:t + 1, hs]
                n_ref[0, r + H_G:r + H_G + 1, :] = vn_ref[0, t:t + 1, hs]

    for h in range(H_G):
        outs, lses = [], []
        for g, (win, dil) in enumerate(ATT_GROUPS):
            c_ref = caches[g]
            p_len = c_ref.shape[1] // rows_per_pos
            hs = slice((g * H_G + h) * HEAD_DIM, (g * H_G + h + 1) * HEAD_DIM)
            q = (q_ref[0, :, hs] * ATT_SCALE).astype(BF16)
            k = jnp.concatenate([c_ref[0, pl.ds(h, p_len, stride=rows_per_pos), :], kn_ref[0, :, hs], zpad],
                                axis=0).astype(BF16)
            v = jnp.concatenate([c_ref[0, pl.ds(H_G + h, p_len, stride=rows_per_pos), :],
                                 vn_ref[0, :, hs], zpad], axis=0).astype(BF16)
            s = lax.dot_general(q, k, (((1,), (1,)), ((), ())), preferred_element_type=F32)
            tq = lax.broadcasted_iota(jnp.int32, s.shape, 0) & (DEC_SEQ - 1)
            col = lax.broadcasted_iota(jnp.int32, s.shape, 1)
            delta = p_len + tq - col
            valid = ((delta >= 0) & ((delta & (dil - 1)) == 0) & (delta <= win)
                     & (col < p_len + DEC_SEQ))
            s = jnp.where(valid, s, -jnp.inf)
            m = jnp.max(s, axis=1, keepdims=True)
            p = jnp.exp(s - m)
            l = jnp.sum(p, axis=1, keepdims=True)
            outs.append(jnp.dot(p.astype(BF16), v, preferred_element_type=F32) / l)
            lses.append(m + jnp.log(l))
        m = jnp.maximum(jnp.maximum(lses[0], lses[1]), lses[2])
        e = [jnp.exp(x - m) for x in lses]
        num = e[0] * outs[0] + e[1] * outs[1] + e[2] * outs[2]
        o_ref[0, :, h * HEAD_DIM:(h + 1) * HEAD_DIM] = num / (e[0] + e[1] + e[2])


def _attention_decode(z8, caches):
    b = z8.shape[0]
    qspec = lambda c: pl.BlockSpec((1, DEC_PAD, ATT_W), lambda i, c=c: (i, 0, c))
    cspecs = [pl.BlockSpec((1, c.shape[1], c.shape[2]), lambda i: (i, 0, 0)) for c in caches]
    vm = 4 * sum(c.shape[1] * c.shape[2] * 4 for c in caches) + 12 * 1024 * 1024
    return pl.pallas_call(
        _attn_dec_kernel,
        out_shape=[jax.ShapeDtypeStruct((b, DEC_PAD, ATT_OUT), F32)]
        + [jax.ShapeDtypeStruct(c.shape, F32) for c in caches],
        grid=(b,),
        in_specs=[qspec(0), qspec(1), qspec(2)] + cspecs,
        out_specs=[pl.BlockSpec((1, DEC_PAD, ATT_OUT), lambda i: (i, 0, 0))] + cspecs,
        compiler_params=pltpu.CompilerParams(
            dimension_semantics=("arbitrary",),
            vmem_limit_bytes=_vmem_limit(vm)),
        name="attn_decode",
    )(z8, z8, z8, *caches)


def _hgrn_consts(c):
    t = np.arange(c)[:, None]
    u = np.arange(c)[None, :]
    seg = [(u <= t)]
    cross = []
    m = c // 2
    while m >= 1:
        blk_t, off_t = t // (2 * m), t % (2 * m)
        piv = blk_t * 2 * m + m - 1
        upper = off_t >= m
        seg.append(np.where(upper, (u > piv) & (u <= t), (u > t) & (u <= piv)))
        cross.append((t // (2 * m) == u // (2 * m)) & (t % (2 * m) >= m) & (u % (2 * m) < m))
        m //= 2
    seg.append(u > t)
    cross.append(t == u)
    return (np.stack(seg).astype(np.float32).reshape(-1, c),
            np.stack(cross).astype(np.float32))


def _hgrn_kernel(q_ref, f_ref, i_ref, g_ref, lb_ref, nw_ref, seg_ref, cross_ref, o_ref, s_out_ref, s_scr,
                 *, tb, n_lvl):
    n = pl.program_id(1)
    c = HG_CHUNK

    @pl.when(n == 0)
    def _():
        s_scr[...] = jnp.zeros_like(s_scr)

    lb = lb_ref[...]
    mx = jnp.maximum(lb[0:1, :], lb[1:2, :])
    e0, e1 = jnp.exp(lb[0:1, :] - mx), jnp.exp(lb[1:2, :] - mx)
    lower = e0 / (e0 + e1)
    nw = nw_ref[...]
    seg = seg_ref[...]

    chunks = range(tb // c)
    nt = (((1,), (1,)), ((), ()))
    qs, ks, vs, ws = [], [], [], []
    for ch in chunks:
        rs = slice(ch * c, (ch + 1) * c)
        fg = lower + (1.0 - lower) * jax.nn.sigmoid(f_ref[rs, :])
        lg = jnp.log(fg)
        lg_hi = lg.astype(BF16)
        lg_lo = (lg - lg_hi.astype(F32)).astype(BF16)
        dsum = (jnp.dot(seg, lg_hi, preferred_element_type=F32)
                + jnp.dot(seg, lg_lo, preferred_element_type=F32))
        ws.append(jnp.exp(dsum))
        qs.append(_silu(q_ref[rs, :]))
        ks.append(1.0 - fg)
        vs.append(i_ref[rs, :].astype(BF16))
    atts = [cross_ref[n_lvl] * lax.dot_general(qs[ch].astype(BF16), ks[ch].astype(BF16), nt,
                                               preferred_element_type=F32) for ch in chunks]
    for lv in range(n_lvl):
        for ch in chunks:
            wl = ws[ch][(1 + lv) * c:(2 + lv) * c]
            atts[ch] = atts[ch] + cross_ref[lv] * lax.dot_general(
                (qs[ch] * wl).astype(BF16), (ks[ch] * wl).astype(BF16), nt, preferred_element_type=F32)
    o_intra = [jnp.dot(atts[ch].astype(BF16), vs[ch], preferred_element_type=F32) for ch in chunks]
    q_dec = [(qs[ch] * ws[ch][0:c]).astype(BF16) for ch in chunks]
    k_end_t = [(ks[ch] * ws[ch][(n_lvl + 1) * c:(n_lvl + 2) * c]).T.astype(BF16) for ch in chunks]
    decay_col = [jnp.broadcast_to(ws[ch][c - 1:c, :], (HG_DK, HG_DV)).T for ch in chunks]
    kv = [jnp.dot(k_end_t[ch], vs[ch], preferred_element_type=F32) for ch in chunks]
    s_cur = s_scr[...]
    for ch in chunks:
        rs = slice(ch * c, (ch + 1) * c)
        o = o_intra[ch] + jnp.dot(q_dec[ch], s_cur.astype(BF16), preferred_element_type=F32)
        s_cur = decay_col[ch] * s_cur + kv[ch]
        ms = jnp.mean(o * o, axis=-1, keepdims=True)
        o_ref[rs, :] = o * lax.rsqrt(ms + EPS) * nw * _silu(g_ref[rs, :])
    s_scr[...] = s_cur

    @pl.when(n == pl.num_programs(1) - 1)
    def _():
        s_out_ref[0] = s_scr[...]


def _hgrn_prompt(z, hg_lb, hg_norm_w, tb):
    t = z.shape[0]
    seg_np, cross_np = _hgrn_consts(HG_CHUNK)
    n_lvl = cross_np.shape[0] - 1
    seg = jnp.asarray(seg_np, dtype=BF16)
    cross = jnp.asarray(cross_np, dtype=F32)
    col = lambda off: pl.BlockSpec((tb, HG_DK), lambda h, n, o=off // HG_DK: (n, o + h))
    kern = functools.partial(_hgrn_kernel, tb=tb, n_lvl=n_lvl)
    vm = 2 * 5 * tb * HG_DK * 4 + 2 * seg_np.size * 2 + 2 * cross_np.size * 4 + 16 * 1024 * 1024
    return pl.pallas_call(
        kern,
        out_shape=(jax.ShapeDtypeStruct((t, HG_WV), F32),
                   jax.ShapeDtypeStruct((HG_HEADS, HG_DK, HG_DV), F32)),
        grid=(HG_HEADS, t // tb),
        in_specs=[col(OFF_QH), col(OFF_FH), col(OFF_IH), col(OFF_GH),
                  pl.BlockSpec((2, HG_DK), lambda h, n: (0, h)),
                  pl.BlockSpec((1, HG_DV), lambda h, n: (0, 0)),
                  pl.BlockSpec(seg_np.shape, lambda h, n: (0, 0)),
                  pl.BlockSpec(cross_np.shape, lambda h, n: (0, 0, 0))],
        out_specs=(pl.BlockSpec((tb, HG_DV), lambda h, n: (n, h)),
                   pl.BlockSpec((1, HG_DK, HG_DV), lambda h, n: (h, 0, 0))),
        scratch_shapes=[pltpu.VMEM((HG_DK, HG_DV), F32)],
        compiler_params=pltpu.CompilerParams(
            dimension_semantics=("arbitrary", "arbitrary"),
            vmem_limit_bytes=_vmem_limit(vm)),
        name="hgrn_prompt",
    )(z, z, z, z, hg_lb, hg_norm_w, seg, cross)


def _hgrn_dec_kernel(q_ref, f_ref, i_ref, g_ref, lb_ref, nw_ref, s_ref, o_ref, s_out_ref, *, heads):
    lb = lb_ref[...]
    mx = jnp.maximum(lb[0:1, :], lb[1:2, :])
    e0, e1 = jnp.exp(lb[0:1, :] - mx), jnp.exp(lb[1:2, :] - mx)
    lower = e0 / (e0 + e1)
    nw = nw_ref[...]
    zpad = jnp.zeros((HG_DK - DEC_PAD, HG_DK), F32)

    def cols(x):
        return jnp.concatenate([x, zpad], axis=0).T

    for h in range(heads):
        hs = slice(h * HG_DK, (h + 1) * HG_DK)
        q = _silu(q_ref[0, :, hs])
        fg = lower[:, hs] + (1.0 - lower[:, hs]) * jax.nn.sigmoid(f_ref[0, :, hs])
        v = i_ref[0, :, hs]
        q_t, f_t, k_t = cols(q), cols(fg), cols(1.0 - fg)
        s = s_ref[0, h]
        row = lax.broadcasted_iota(jnp.int32, (DEC_PAD, HG_DV), 0)
        o = jnp.zeros((DEC_PAD, HG_DV), F32)
        for t in range(DEC_SEQ):
            bc = lambda a: jnp.broadcast_to(a[:, t:t + 1], (HG_DK, HG_DV))
            s = bc(f_t) * s + bc(k_t) * v[t:t + 1, :]
            o = jnp.where(row == t, jnp.sum(s * bc(q_t), axis=0, keepdims=True), o)
        ms = jnp.mean(o * o, axis=-1, keepdims=True)
        o_ref[0, :, hs] = o * lax.rsqrt(ms + EPS) * nw * _silu(g_ref[0, :, hs])
        s_out_ref[0, h] = s


def _hgrn_decode(z8, hg_lb, hg_norm_w, state):
    b = z8.shape[0]
    heads = COL_BLK // HG_DK
    nhb = HG_HEADS // heads
    col = lambda off: pl.BlockSpec((1, DEC_PAD, COL_BLK), lambda i, j, o=off // COL_BLK: (i, 0, o + j))
    kern = functools.partial(_hgrn_dec_kernel, heads=heads)
    sspec = pl.BlockSpec((1, heads, HG_DK, HG_DV), lambda i, j: (i, j, 0, 0))
    return pl.pallas_call(
        kern,
        out_shape=(jax.ShapeDtypeStruct((b, DEC_PAD, HG_WV), F32),
                   jax.ShapeDtypeStruct(state.shape, F32)),
        grid=(b, nhb),
        in_specs=[col(OFF_QH), col(OFF_FH), col(OFF_IH), col(OFF_GH),
                  pl.BlockSpec((2, COL_BLK), lambda i, j: (0, j)),
                  pl.BlockSpec((1, HG_DV), lambda i, j: (0, 0)),
                  sspec],
        out_specs=(pl.BlockSpec((1, DEC_PAD, COL_BLK), lambda i, j: (i, 0, j)), sspec),
        compiler_params=pltpu.CompilerParams(
            dimension_semantics=("arbitrary", "arbitrary"),
            vmem_limit_bytes=_vmem_limit(0)),
        name="hgrn_decode",
    )(z8, z8, z8, z8, hg_lb, hg_norm_w, state)


def _merge_kernel(att_ref, hg_ref, ga_ref, gb_ref, x_ref, g1_ref, wpa_ref, wpb_ref, wo_ref, o_ref,
                  att_scr, hg_scr, acc_scr):
    n = pl.program_id(1)

    @pl.when(n == 0)
    def _():
        att_scr[...] = att_ref[...].astype(BF16)
        hg_scr[...] = hg_ref[...].astype(BF16)
        acc_scr[...] = jnp.zeros_like(acc_scr)

    tn = wpa_ref.shape[2]
    ya = jnp.dot(att_scr[...], wpa_ref[n], preferred_element_type=F32)
    yb = jnp.dot(hg_scr[...], wpb_ref[n], preferred_element_type=F32)
    ymix = jax.nn.sigmoid(ga_ref[...]) * ya + jax.nn.sigmoid(gb_ref[...]) * yb
    wo = wo_ref[pl.ds(pl.multiple_of(n * tn, tn), tn), :]
    acc_scr[...] += jnp.dot(ymix.astype(BF16), wo, preferred_element_type=F32)

    @pl.when(n == pl.num_programs(1) - 1)
    def _():
        o_ref[...] = x_ref[...] + g1_ref[...] * acc_scr[...]


def _merge(att, hg, z, x2d, g1, wpa_bf, wpb_bf, wo_bf, tm):
    rows, d = x2d.shape
    per_row = g1.shape[0] != 1
    tn = COL_BLK
    nn = d // tn
    wpa3 = wpa_bf.reshape(ATT_OUT, nn, tn).transpose(1, 0, 2)
    wpb3 = wpb_bf.reshape(HG_WV, nn, tn).transpose(1, 0, 2)
    resident = pl.Buffered(1)
    vm = (2 * tm * (ATT_OUT + HG_WV + 2 * tn + 2 * d) * 4 + tm * (ATT_OUT + HG_WV) * 2 + tm * d * 4
          + (ATT_OUT * d + HG_WV * d + d * d) * 2 + (2 * tm * d * 4 if per_row else 0)
          + 6 * tm * tn * 4 + 4 * 1024 * 1024)
    return pl.pallas_call(
        _merge_kernel,
        out_shape=jax.ShapeDtypeStruct((rows, d), F32),
        grid=(rows // tm, nn),
        in_specs=[pl.BlockSpec((tm, ATT_OUT), lambda i, n: (i, 0)),
                  pl.BlockSpec((tm, HG_WV), lambda i, n: (i, 0)),
                  pl.BlockSpec((tm, tn), lambda i, n: (i, OFF_GA // tn + n)),
                  pl.BlockSpec((tm, tn), lambda i, n: (i, OFF_GB // tn + n)),
                  pl.BlockSpec((tm, d), lambda i, n: (i, 0)),
                  _mod_spec(per_row, tm, d),
                  pl.BlockSpec((nn, ATT_OUT, tn), lambda i, n: (0, 0, 0), pipeline_mode=resident),
                  pl.BlockSpec((nn, HG_WV, tn), lambda i, n: (0, 0, 0), pipeline_mode=resident),
                  pl.BlockSpec((d, d), lambda i, n: (0, 0), pipeline_mode=resident)],
        out_specs=pl.BlockSpec((tm, d), lambda i, n: (i, 0)),
        scratch_shapes=[pltpu.VMEM((tm, ATT_OUT), BF16), pltpu.VMEM((tm, HG_WV), BF16),
                        pltpu.VMEM((tm, d), F32)],
        compiler_params=pltpu.CompilerParams(
            dimension_semantics=("arbitrary", "arbitrary"),
            vmem_limit_bytes=_vmem_limit(vm)),
        name="merge",
    )(att, hg, z, z, x2d, g1, wpa3, wpb3, wo_bf)


def _ffn_kernel(x_ref, sh_ref, sc_ref, g2_ref, nw_ref, nf_ref, wa_ref, wb_ref, wd_ref, cw_ref, cb_ref,
                p1_ref, p2_ref, y_ref, tail_ref, h_scr, acc_scr, carry_scr,
                *, tm, ta, per_row, seq_rows):
    i = pl.program_id(0)
    f = pl.program_id(1)

    @pl.when(f == 0)
    def _():
        _norm_mod_rows(x_ref, nw_ref, sc_ref, sh_ref, h_scr, tm, per_row)
        acc_scr[...] = jnp.zeros_like(acc_scr)

    if seq_rows is None:
        @pl.when(i == 0)
        def _():
            carry_scr[f] = p1_ref[...]

    h = h_scr[...]
    tf = wa_ref.shape[1]
    groups = [slice(c0, c0 + FFN_SPLIT) for c0 in range(0, tf, FFN_SPLIT)]
    a_parts = [jnp.dot(h, wa_ref[:, cs], preferred_element_type=F32) for cs in groups]
    b_parts = [jnp.dot(h, wb_ref[:, cs], preferred_element_type=F32) for cs in groups]
    row = lax.broadcasted_iota(jnp.int32, (tm, FFN_SPLIT), 0)
    ys = []
    for cs, a, b in zip(groups, a_parts, b_parts):
        tail_ref[:, cs] = a[tm - ta:, :]
        r1 = pltpu.roll(a, 1, 0)
        r2 = pltpu.roll(a, 2, 0)
        if seq_rows is None:
            prev = carry_scr[f, :, cs]
            a1 = jnp.where(row == 0, prev[SUBLANES - 1:SUBLANES, :], r1)
            a2 = jnp.where(row == 0, prev[SUBLANES - 2:SUBLANES - 1, :],
                           jnp.where(row == 1, prev[SUBLANES - 1:SUBLANES, :], r2))
            carry_scr[f, :, cs] = a[tm - SUBLANES:, :]
        else:
            t = row & (seq_rows - 1)
            a1 = jnp.where(t == 0, p1_ref[:, cs], r1)
            a2 = jnp.where(t <= 1, p2_ref[:, cs], r2)
        cw = cw_ref[:, cs]
        u = cb_ref[:, cs] + a2 * cw[0:1, :] + a1 * cw[1:2, :] + a * cw[2:3, :]
        ys.append((_silu(u) * b).astype(BF16))
    y = jnp.concatenate(ys, axis=1)
    acc_scr[...] += jnp.dot(y, wd_ref[...], preferred_element_type=F32)

    @pl.when(f == pl.num_programs(1) - 1)
    def _():
        step = min(tm, 128)

        def body(r, carry):
            rs = pl.ds(pl.multiple_of(r * step, step), step)
            g2 = g2_ref[rs, :] if per_row else g2_ref[...]
            x2 = x_ref[rs, :] + g2 * acc_scr[rs, :]
            ms = jnp.mean(x2 * x2, axis=-1, keepdims=True)
            y_ref[rs, :] = x2 * lax.rsqrt(ms + EPS) * nf_ref[...]
            return carry

        lax.fori_loop(0, tm // step, body, 0)


def _ffn(x1, sh, sc, g2, nw, nf, wa_bf, wb_bf, wd_bf, conv_w, conv_b, p1, p2, tm, seq_rows):
    rows, d = x1.shape
    dff = wa_bf.shape[1]
    per_row = sh.shape[0] != 1
    tf = COL_BLK
    nf_blk = dff // tf
    ta = SUBLANES if seq_rows is None else tm
    prow = SUBLANES if seq_rows is None else tm
    kern = functools.partial(_ffn_kernel, tm=tm, ta=ta, per_row=per_row, seq_rows=seq_rows)
    pspec = pl.BlockSpec((prow, tf), lambda i, f: (0 if seq_rows is None else i, f))
    vm = (4 * tm * d * 4 + tm * d * 2 + tm * d * 4 + 2 * 3 * d * tf * 2 + 10 * tm * tf * 4
          + (6 * tm * d * 4 if per_row else 0) + 8 * 1024 * 1024)
    return pl.pallas_call(
        kern,
        out_shape=(jax.ShapeDtypeStruct((rows, d), F32),
                   jax.ShapeDtypeStruct((ta * (rows // tm), dff), F32)),
        grid=(rows // tm, nf_blk),
        in_specs=[pl.BlockSpec((tm, d), lambda i, f: (i, 0)),
                  _mod_spec(per_row, tm, d), _mod_spec(per_row, tm, d), _mod_spec(per_row, tm, d),
                  pl.BlockSpec((1, d), lambda i, f: (0, 0)),
                  pl.BlockSpec((1, d), lambda i, f: (0, 0)),
                  pl.BlockSpec((d, tf), lambda i, f: (0, f)),
                  pl.BlockSpec((d, tf), lambda i, f: (0, f)),
                  pl.BlockSpec((tf, d), lambda i, f: (f, 0)),
                  pl.BlockSpec((CONV_W, tf), lambda i, f: (0, f)),
                  pl.BlockSpec((1, tf), lambda i, f: (0, f)),
                  pspec, pspec],
        out_specs=(pl.BlockSpec((tm, d), lambda i, f: (i, 0)),
                   pl.BlockSpec((ta, tf), lambda i, f: (i, f))),
        scratch_shapes=[pltpu.VMEM((tm, d), BF16), pltpu.VMEM((tm, d), F32),
                        pltpu.VMEM((nf_blk, SUBLANES, tf), F32)],
        compiler_params=pltpu.CompilerParams(
            dimension_semantics=("arbitrary", "arbitrary"),
            vmem_limit_bytes=_vmem_limit(vm)),
        name="ffn",
    )(x1, sh, sc, g2, nw, nf, wa_bf, wb_bf, wd_bf, conv_w, conv_b, p1, p2)


def kernel(x_prompt, x_sample, cache_kv_w128, cache_kv_w512, cache_kv_w2048, state_hgrn, state_conv,
           c_prompt, c_sample, w_ada, b_ada, norm1_w, w_in, hg_lb, hg_norm_w, w_pa, w_pb, w_o,
           norm2_w, w_ffn_a, w_ffn_b, conv_w, conv_b, w_ffn_down, norm_f_w):
    d = D_MODEL
    bp, t, _ = x_prompt.shape
    bs, ts, _ = x_sample.shape
    assert bp == 1 and ts == DEC_SEQ and w_ada.shape[0] == 1
    rows_s = bs * ts

    w_in_bf = w_in[0].astype(BF16)
    wpa_bf, wpb_bf, wo_bf = w_pa[0].astype(BF16), w_pb[0].astype(BF16), w_o[0].astype(BF16)
    wa_bf, wb_bf, wd_bf = w_ffn_a[0].astype(BF16), w_ffn_b[0].astype(BF16), w_ffn_down[0].astype(BF16)
    nw1, nw2, nwf = norm1_w[0].reshape(1, d), norm2_w[0].reshape(1, d), norm_f_w.reshape(1, d)
    hg_nw = hg_norm_w[0].reshape(1, HG_DV)
    cw, cb = conv_w[0], conv_b[0].reshape(1, D_FF)

    n_seq = bp + bs
    pad_rows = -n_seq % SUBLANES
    c_all = jnp.concatenate([c_prompt, c_sample, jnp.zeros((pad_rows, d), F32)], axis=0)
    mod = _ada(c_all, w_ada[0], b_ada[0])
    mod_p = [mod[0:bp, k * d:(k + 1) * d] for k in range(N_MOD)]
    mod_s = [jnp.repeat(mod[bp:n_seq, k * d:(k + 1) * d], ts, axis=0) for k in range(N_MOD)]

    xp = x_prompt.reshape(t, d)
    rope_p = _rope_tables(np.arange(t))
    zp = _inproj(xp, mod_p[0], mod_p[1], nw1, w_in_bf, rope_p, tm=1024)
    att_p = _attention_prompt(zp)
    hg_p, s_p = _hgrn_prompt(zp, hg_lb, hg_nw, tb=1024)
    x1p = _merge(att_p, hg_p, zp, xp, mod_p[2], wpa_bf, wpb_bf, wo_bf, tm=512)
    conv0 = jnp.zeros((SUBLANES, D_FF), F32)
    yp, tail_p = _ffn(x1p, mod_p[3], mod_p[4], mod_p[5], nw2, nwf, wa_bf, wb_bf, wd_bf, cw, cb,
                      conv0, conv0, tm=512, seq_rows=None)

    kv_p = []
    for g, (win, _) in enumerate(ATT_GROUPS):
        keep = min(win, t)
        ks = zp[t - keep:, OFF_KA + g * ATT_OUT:OFF_KA + (g + 1) * ATT_OUT]
        vs = zp[t - keep:, OFF_VA + g * ATT_OUT:OFF_VA + (g + 1) * ATT_OUT]
        kv_p.append(jnp.stack([ks.reshape(keep, H_G, HEAD_DIM), vs.reshape(keep, H_G, HEAD_DIM)],
                              axis=1)[None, None])
    hgrn_p = s_p[None, None]
    conv_p = tail_p[tail_p.shape[0] - (CONV_W - 1):][None, None]

    xs = x_sample.reshape(rows_s, d)
    rope_s = _rope_tables(PAST_LEN + np.arange(rows_s) % ts)
    zs = _inproj(xs, mod_s[0], mod_s[1], nw1, w_in_bf, rope_s, tm=rows_s)
    zs8 = jnp.pad(zs.reshape(bs, ts, IN_TOTAL), ((0, 0), (0, DEC_PAD - ts), (0, 0)))
    cache_in = (cache_kv_w128, cache_kv_w512, cache_kv_w2048)
    caches = [c.reshape(bs, c.shape[2] * 2 * H_G, HEAD_DIM) for c in cache_in]
    att_s8, *new_caches = _attention_decode(zs8, caches)
    att_s = att_s8[:, :ts].reshape(rows_s, ATT_OUT)
    kv_s = [n.reshape(c.shape) for n, c in zip(new_caches, cache_in)]
    hg_s8, s_s = _hgrn_decode(zs8, hg_lb, hg_nw, state_hgrn[0])
    hg_s = hg_s8[:, :ts].reshape(rows_s, HG_WV)
    x1s = _merge(att_s, hg_s, zs, xs, mod_s[2], wpa_bf, wpb_bf, wo_bf, tm=rows_s)
    buf = state_conv[0]
    zrow = jnp.zeros((bs, 1, D_FF), F32)
    p1 = jnp.concatenate([buf[:, 1:2], zrow, zrow, zrow], axis=1).reshape(rows_s, D_FF)
    p2 = jnp.concatenate([buf[:, 0:1], buf[:, 1:2], zrow, zrow], axis=1).reshape(rows_s, D_FF)
    ys, a_s = _ffn(x1s, mod_s[3], mod_s[4], mod_s[5], nw2, nwf, wa_bf, wb_bf, wd_bf, cw, cb,
                   p1, p2, tm=rows_s, seq_rows=ts)

    hgrn_s = s_s[None]
    conv_s = a_s.reshape(bs, ts, D_FF)[:, ts - (CONV_W - 1):][None]

    return (yp.reshape(bp, t, d), ys.reshape(bs, ts, d),
            kv_p[0], kv_p[1], kv_p[2], hgrn_p, conv_p,
            kv_s[0], kv_s[1], kv_s[2], hgrn_s, conv_s)
```

```python
import functools

import numpy as np
import jax
import jax.numpy as jnp
from jax import lax
from jax.experimental import pallas as pl
from jax.experimental.pallas import tpu as pltpu

F32 = jnp.float32
BF16 = jnp.bfloat16

D_MODEL = 2048
SEQ = 16384
DEC_BATCH = 32
DEC_SEQ = 4
PAST_LEN = 16384

HEAD_DIM = 128
ATT_GROUPS = ((128, 1), (512, 4), (2048, 16))
H_G = 4
N_ATT_HEADS = H_G * len(ATT_GROUPS)
ATT_W = N_ATT_HEADS * HEAD_DIM
ATT_OUT = H_G * HEAD_DIM
ROT_DIM = HEAD_DIM // 4
ROPE_THETA = 500000.0
ATT_SCALE = HEAD_DIM ** -0.5

HG_HEADS = 8
HG_DK = 128
HG_DV = 128
HG_WK = HG_HEADS * HG_DK
HG_WV = HG_HEADS * HG_DV

D_FF = 5632
CONV_W = 3
N_MOD = 6
EPS = 1e-6

IN_SIZES = (ATT_W, ATT_W, ATT_W, HG_WK, HG_WK, HG_WV, HG_WV, D_MODEL, D_MODEL)
IN_TOTAL = sum(IN_SIZES)
IN_OFFS = tuple(int(s) for s in np.cumsum((0,) + IN_SIZES)[:-1])
OFF_QA, OFF_KA, OFF_VA, OFF_QH, OFF_FH, OFF_IH, OFF_GH, OFF_GA, OFF_GB = IN_OFFS
QK_COLS = 2 * ATT_W
R_VA, R_QH, R_FH, R_IH, R_GH, R_GA, R_GB = (o - QK_COLS for o in IN_OFFS[2:])

V7X_VMEM_BYTES = 64 * 1024 * 1024
SUBLANES = 8
LANES = 128
DEC_PAD = SUBLANES

COL_BLK = 512
MXU_COLS = 256
ROPE_ROWS = 512
HG_CHUNK = 128
ATT_QB = 128
ATT_SB = 2048


def _vmem_limit(nbytes):
    return int(min(V7X_VMEM_BYTES - 8 * 1024 * 1024, max(nbytes, 16 * 1024 * 1024)))


def _silu(x):
    return x * jax.nn.sigmoid(x)


def _unroll_for(trips, max_unroll=6):
    return next(u for u in range(min(max_unroll, trips), 0, -1) if trips % u == 0)


def _ada_kernel(c_ref, w_ref, b_ref, o_ref):
    c = c_ref[...]
    s = _silu(c).astype(BF16)
    o_ref[...] = jnp.dot(s, w_ref[...].astype(BF16), preferred_element_type=F32) + b_ref[...]


def _ada(c_all, w_ada, b_ada):
    rows, d = c_all.shape
    n = w_ada.shape[1]
    tn = 1024
    return pl.pallas_call(
        _ada_kernel,
        out_shape=jax.ShapeDtypeStruct((rows, n), F32),
        grid=(n // tn,),
        in_specs=[pl.BlockSpec((rows, d), lambda j: (0, 0)),
                  pl.BlockSpec((d, tn), lambda j: (0, j)),
                  pl.BlockSpec((1, tn), lambda j: (0, j))],
        out_specs=pl.BlockSpec((rows, tn), lambda j: (0, j)),
        compiler_params=pltpu.CompilerParams(
            dimension_semantics=("arbitrary",),
            vmem_limit_bytes=_vmem_limit(2 * d * tn * 4 + d * tn * 2 + 4 * rows * (d + tn) * 4)),
        name="ada",
    )(c_all, w_ada, b_ada.reshape(1, n))


def _norm_mod_rows(x_ref, nw_ref, sc_ref, sh_ref, h_ref, tm, per_row):
    step = min(tm, 128)

    def body(r, carry):
        rs = pl.ds(pl.multiple_of(r * step, step), step)
        x = x_ref[rs, :]
        ms = jnp.mean(x * x, axis=-1, keepdims=True)
        y = x * lax.rsqrt(ms + EPS) * nw_ref[...]
        if per_row:
            h = y * (1.0 + sc_ref[rs, :]) + sh_ref[rs, :]
        else:
            h = y * (1.0 + sc_ref[...]) + sh_ref[...]
        h_ref[rs, :] = h.astype(BF16)
        return carry

    lax.fori_loop(0, tm // step, body, 0)


def _mod_spec(per_row, tm, d):
    if per_row:
        return pl.BlockSpec((tm, d), lambda i, j: (i, 0))
    return pl.BlockSpec((1, d), lambda i, j: (0, 0))


def _norm_kernel(x_ref, sh_ref, sc_ref, nw_ref, h_ref, *, tm, per_row):
    _norm_mod_rows(x_ref, nw_ref, sc_ref, sh_ref, h_ref, tm, per_row)


def _norm_mod(x2d, sh, sc, nw, tm):
    rows, d = x2d.shape
    per_row = sh.shape[0] != 1
    mod = (pl.BlockSpec((tm, d), lambda i: (i, 0)) if per_row else pl.BlockSpec((1, d), lambda i: (0, 0)))
    return pl.pallas_call(
        functools.partial(_norm_kernel, tm=tm, per_row=per_row),
        out_shape=jax.ShapeDtypeStruct((rows, d), BF16),
        grid=(rows // tm,),
        in_specs=[pl.BlockSpec((tm, d), lambda i: (i, 0)), mod, mod,
                  pl.BlockSpec((1, d), lambda i: (0, 0))],
        out_specs=pl.BlockSpec((tm, d), lambda i: (i, 0)),
        compiler_params=pltpu.CompilerParams(
            dimension_semantics=("arbitrary",),
            vmem_limit_bytes=_vmem_limit(2 * tm * d * 6 + (4 * tm * d * 4 if per_row else 0) + 8 * 1024 * 1024)),
        name="norm_mod",
    )(x2d, sh, sc, nw)


def _proj_rope_kernel(h_ref, w_ref, cos_ref, s1_ref, s2_ref, z_ref):
    tm = h_ref.shape[0]
    rstep = min(tm, ROPE_ROWS)
    for r0 in range(0, tm, rstep):
        rs = slice(r0, r0 + rstep)
        h = h_ref[rs, :]
        c, s1, s2 = cos_ref[rs, :], s1_ref[rs, :], s2_ref[rs, :]
        for c0 in range(0, COL_BLK, MXU_COLS):
            grp = jnp.dot(h, w_ref[:, c0:c0 + MXU_COLS], preferred_element_type=F32)
            for hh in range(MXU_COLS // HEAD_DIM):
                blk = grp[:, hh * HEAD_DIM:(hh + 1) * HEAD_DIM]
                z_ref[rs, c0 + hh * HEAD_DIM:c0 + (hh + 1) * HEAD_DIM] = (
                    blk * c + pltpu.roll(blk, HEAD_DIM - ROT_DIM // 2, 1) * s1
                    + pltpu.roll(blk, ROT_DIM // 2, 1) * s2)


def _proj_plain_kernel(h_ref, w_ref, z_ref):
    z_ref[...] = jnp.dot(h_ref[...], w_ref[...], preferred_element_type=F32)


def _proj_gate_kernel(h_ref, w_ref, z_ref):
    tm = h_ref.shape[0]
    rstep = min(tm, ROPE_ROWS)
    for r0 in range(0, tm, rstep):
        rs = slice(r0, r0 + rstep)
        h = h_ref[rs, :]
        for c0 in range(0, COL_BLK, MXU_COLS):
            cs = slice(c0, c0 + MXU_COLS)
            z_ref[rs, cs] = jax.nn.sigmoid(jnp.dot(h, w_ref[:, cs], preferred_element_type=F32)).astype(BF16)


def _inproj(h, w_in_bf, rope, tm):
    rows, d = h.shape
    tn = COL_BLK
    n_qk = QK_COLS // tn
    n_rest = R_GA // tn
    n_gate = (2 * D_MODEL) // tn
    tab = pl.BlockSpec((tm, HEAD_DIM), lambda i, j: (i, 0))
    params = pltpu.CompilerParams(
        dimension_semantics=("arbitrary", "arbitrary"),
        vmem_limit_bytes=_vmem_limit(2 * tm * d * 2 + 2 * d * tn * 2 + 6 * tm * HEAD_DIM * 4
                                     + 6 * tm * tn * 4 + 4 * 1024 * 1024))
    z_qk = pl.pallas_call(
        _proj_rope_kernel,
        out_shape=jax.ShapeDtypeStruct((rows, n_qk * tn), F32),
        grid=(rows // tm, n_qk),
        in_specs=[pl.BlockSpec((tm, d), lambda i, j: (i, 0)),
                  pl.BlockSpec((d, tn), lambda i, j: (0, j)),
                  tab, tab, tab],
        out_specs=pl.BlockSpec((tm, tn), lambda i, j: (i, j)),
        compiler_params=params,
        name="proj_rope",
    )(h, w_in_bf, *rope)
    z_rest = pl.pallas_call(
        _proj_plain_kernel,
        out_shape=jax.ShapeDtypeStruct((rows, n_rest * tn), F32),
        grid=(rows // tm, n_rest),
        in_specs=[pl.BlockSpec((tm, d), lambda i, j: (i, 0)),
                  pl.BlockSpec((d, tn), lambda i, j: (0, n_qk + j))],
        out_specs=pl.BlockSpec((tm, tn), lambda i, j: (i, j)),
        compiler_params=params,
        name="proj_plain",
    )(h, w_in_bf)
    gates = pl.pallas_call(
        _proj_gate_kernel,
        out_shape=jax.ShapeDtypeStruct((rows, n_gate * tn), BF16),
        grid=(rows // tm, n_gate),
        in_specs=[pl.BlockSpec((tm, d), lambda i, j: (i, 0)),
                  pl.BlockSpec((d, tn), lambda i, j: (0, n_qk + n_rest + j))],
        out_specs=pl.BlockSpec((tm, tn), lambda i, j: (i, j)),
        compiler_params=params,
        name="proj_gate",
    )(h, w_in_bf)
    return z_qk, z_rest, gates


def _rope_tables(pos):
    half = ROT_DIM // 2
    inv_freq = ROPE_THETA ** (-np.arange(half, dtype=np.float64) * 2.0 / ROT_DIM)
    ang = np.asarray(pos, dtype=np.float64)[:, None] * inv_freq[None, :]
    cos, sin = jnp.asarray(np.cos(ang), dtype=F32), jnp.asarray(np.sin(ang), dtype=F32)
    rows = pos.shape[0]
    ones = jnp.ones((rows, HEAD_DIM - ROT_DIM), F32)
    zer = jnp.zeros((rows, HEAD_DIM - ROT_DIM), F32)
    zh = jnp.zeros((rows, half), F32)
    c = jnp.concatenate([cos, cos, ones], axis=1)
    s1 = jnp.concatenate([-sin, zh, zer], axis=1)
    s2 = jnp.concatenate([zh, sin, zer], axis=1)
    return c, s1, s2


def _attn_tile(q, ka, kb, va, vb, valid):
    k = jnp.concatenate([ka, kb], axis=0).astype(BF16)
    v = jnp.concatenate([va, vb], axis=0).astype(BF16)
    s = lax.dot_general((q * ATT_SCALE).astype(BF16), k, (((1,), (1,)), ((), ())),
                        preferred_element_type=F32)
    s = jnp.where(valid, s, -jnp.inf)
    m = jnp.max(s, axis=1, keepdims=True)
    p = jnp.exp(s - m)
    l = jnp.sum(p, axis=1, keepdims=True)
    o = jnp.dot(p.astype(BF16), v, preferred_element_type=F32) / l
    return o, m + jnp.log(l)


def _attn_kernel(*refs):
    q_refs = refs[0:3]
    kc_refs = refs[3:6]
    kp_refs = refs[6:9]
    vc_refs = refs[9:12]
    vp_refs = refs[12:15]
    o_ref = refs[15]
    og_scr, lse_scr = refs[16], refs[17]
    n = pl.program_id(0)

    row = lax.broadcasted_iota(jnp.int32, (ATT_QB, 2 * ATT_QB), 0)
    col = lax.broadcasted_iota(jnp.int32, (ATT_QB, 2 * ATT_QB), 1)
    band = (col >= row) & (col <= row + ATT_QB)
    band_first = band & (col >= jnp.where(n > 0, 0, ATT_QB))

    for g, (_, dil) in enumerate(ATT_GROUPS):
        q_ref, kc_ref, kp_ref, vc_ref, vp_ref = q_refs[g], kc_refs[g], kp_refs[g], vc_refs[g], vp_refs[g]
        nq = ATT_SB // (dil * ATT_QB)

        def rows(start, count):
            if dil == 1:
                return pl.ds(pl.multiple_of(start, ATT_QB), count)
            return pl.ds(start, count, stride=dil)

        def put(g_, tok0, o, lse):
            og_scr[g_, rows(tok0, ATT_QB), :] = o
            lse_scr[g_, rows(tok0, ATT_QB), :] = jnp.broadcast_to(lse, (ATT_QB, HEAD_DIM))

        def first_body(r, carry):
            o, lse = _attn_tile(q_ref[rows(r, ATT_QB), :],
                                kp_ref[rows(r, ATT_QB), :], kc_ref[rows(r, ATT_QB), :],
                                vp_ref[rows(r, ATT_QB), :], vc_ref[rows(r, ATT_QB), :],
                                band_first)
            put(g, r, o, lse)
            return carry

        lax.fori_loop(0, dil, first_body, 0, unroll=_unroll_for(dil))

        if nq > 1:
            def rest_body(t, carry):
                r = t // (nq - 1)
                u = t % (nq - 1) + 1
                q0 = r + dil * (u * ATT_QB)
                k0 = r + dil * ((u - 1) * ATT_QB)
                k1 = r + dil * (u * ATT_QB)
                o, lse = _attn_tile(q_ref[rows(q0, ATT_QB), :],
                                    kc_ref[rows(k0, ATT_QB), :], kc_ref[rows(k1, ATT_QB), :],
                                    vc_ref[rows(k0, ATT_QB), :], vc_ref[rows(k1, ATT_QB), :],
                                    band)
                put(g, q0, o, lse)
                return carry

            lax.fori_loop(0, dil * (nq - 1), rest_body, 0, unroll=_unroll_for(dil * (nq - 1)))

    def merge_body(c, carry):
        rs = pl.ds(pl.multiple_of(c * 256, 256), 256)
        l0, l1, l2 = lse_scr[0, rs, :], lse_scr[1, rs, :], lse_scr[2, rs, :]
        m = jnp.maximum(jnp.maximum(l0, l1), l2)
        e0, e1, e2 = jnp.exp(l0 - m), jnp.exp(l1 - m), jnp.exp(l2 - m)
        num = e0 * og_scr[0, rs, :] + e1 * og_scr[1, rs, :] + e2 * og_scr[2, rs, :]
        o_ref[rs, :] = (num / (e0 + e1 + e2)).astype(o_ref.dtype)
        return carry

    lax.fori_loop(0, ATT_SB // 256, merge_body, 0)


def _attention_prompt(z_qk, z_rest):
    t = z_qk.shape[0]
    nsb = t // ATT_SB
    in_specs, args = [], []

    def cur(colblk):
        return pl.BlockSpec((ATT_SB, HEAD_DIM), lambda n, h, c=colblk: (n, c + h))

    def prev(colblk, dil):
        rows_p = dil * ATT_QB
        per = ATT_SB // rows_p
        return pl.BlockSpec((rows_p, HEAD_DIM),
                            lambda n, h, c=colblk, per=per: (jnp.maximum(n * per - 1, 0), c + h))

    for g in range(3):
        in_specs.append(cur(OFF_QA // HEAD_DIM + g * H_G)); args.append(z_qk)
    for z, base in ((z_qk, OFF_KA), (z_rest, R_VA)):
        for g in range(3):
            in_specs.append(cur(base // HEAD_DIM + g * H_G)); args.append(z)
        for g, (_, dil) in enumerate(ATT_GROUPS):
            in_specs.append(prev(base // HEAD_DIM + g * H_G, dil)); args.append(z)
    blk = ATT_SB * HEAD_DIM * 4
    prev_rows = sum(d * ATT_QB for _, d in ATT_GROUPS)
    vm = 2 * (9 * blk + 2 * prev_rows * HEAD_DIM * 4) + 2 * blk + 6 * blk + 8 * 1024 * 1024
    return pl.pallas_call(
        _attn_kernel,
        out_shape=jax.ShapeDtypeStruct((t, ATT_OUT), BF16),
        grid=(nsb, H_G),
        in_specs=in_specs,
        out_specs=pl.BlockSpec((ATT_SB, HEAD_DIM), lambda n, h: (n, h)),
        scratch_shapes=[pltpu.VMEM((3, ATT_SB, HEAD_DIM), F32),
                        pltpu.VMEM((3, ATT_SB, HEAD_DIM), F32)],
        compiler_params=pltpu.CompilerParams(
            dimension_semantics=("arbitrary", "arbitrary"),
            vmem_limit_bytes=_vmem_limit(vm)),
        name="attn_prompt",
    )(*args)


def _attn_dec_kernel(q_ref, kn_ref, vn_ref, c0_ref, c1_ref, c2_ref, o_ref, n0_ref, n1_ref, n2_ref):
    caches = (c0_ref, c1_ref, c2_ref)
    news = (n0_ref, n1_ref, n2_ref)
    rows_per_pos = 2 * H_G
    zpad = jnp.zeros((LANES - DEC_PAD, HEAD_DIM), F32)

    for g in range(len(ATT_GROUPS)):
        c_ref, n_ref = caches[g], news[g]
        n_rows = c_ref.shape[1]
        keep = n_rows - DEC_SEQ * rows_per_pos
        n_ref[0, 0:keep, :] = c_ref[0, DEC_SEQ * rows_per_pos:n_rows, :]
        for t in range(DEC_SEQ):
            for h in range(H_G):
                hs = slice((g * H_G + h) * HEAD_DIM, (g * H_G + h + 1) * HEAD_DIM)
                r = keep + t * rows_per_pos + h
                n_ref[0, r:r + 1, :] = kn_ref[0, t:t + 1, hs]
                n_ref[0, r + H_G:r + H_G + 1, :] = vn_ref[0, t:t + 1, hs]

    for h in range(H_G):
        outs, lses = [], []
        for g, (win, dil) in enumerate(ATT_GROUPS):
            c_ref = caches[g]
            p_len = c_ref.shape[1] // rows_per_pos
            hs = slice((g * H_G + h) * HEAD_DIM, (g * H_G + h + 1) * HEAD_DIM)
            q = (q_ref[0, :, hs] * ATT_SCALE).astype(BF16)
            k = jnp.concatenate([c_ref[0, pl.ds(h, p_len, stride=rows_per_pos), :], kn_ref[0, :, hs], zpad],
                                axis=0).astype(BF16)
            v = jnp.concatenate([c_ref[0, pl.ds(H_G + h, p_len, stride=rows_per_pos), :],
                                 vn_ref[0, :, hs], zpad], axis=0).astype(BF16)
            s = lax.dot_general(q, k, (((1,), (1,)), ((), ())), preferred_element_type=F32)
            tq = lax.broadcasted_iota(jnp.int32, s.shape, 0) & (DEC_SEQ - 1)
            col = lax.broadcasted_iota(jnp.int32, s.shape, 1)
            delta = p_len + tq - col
            valid = ((delta >= 0) & ((delta & (dil - 1)) == 0) & (delta <= win)
                     & (col < p_len + DEC_SEQ))
            s = jnp.where(valid, s, -jnp.inf)
            m = jnp.max(s, axis=1, keepdims=True)
            p = jnp.exp(s - m)
            l = jnp.sum(p, axis=1, keepdims=True)
            outs.append(jnp.dot(p.astype(BF16), v, preferred_element_type=F32) / l)
            lses.append(m + jnp.log(l))
        m = jnp.maximum(jnp.maximum(lses[0], lses[1]), lses[2])
        e = [jnp.exp(x - m) for x in lses]
        num = e[0] * outs[0] + e[1] * outs[1] + e[2] * outs[2]
        o_ref[0, :, h * HEAD_DIM:(h + 1) * HEAD_DIM] = num / (e[0] + e[1] + e[2])


def _attention_decode(zqk8, zrest8, caches):
    b = zqk8.shape[0]
    qspec = lambda c: pl.BlockSpec((1, DEC_PAD, ATT_W), lambda i, c=c: (i, 0, c))
    cspecs = [pl.BlockSpec((1, c.shape[1], c.shape[2]), lambda i: (i, 0, 0)) for c in caches]
    vm = 4 * sum(c.shape[1] * c.shape[2] * 4 for c in caches) + 12 * 1024 * 1024
    return pl.pallas_call(
        _attn_dec_kernel,
        out_shape=[jax.ShapeDtypeStruct((b, DEC_PAD, ATT_OUT), F32)]
        + [jax.ShapeDtypeStruct(c.shape, F32) for c in caches],
        grid=(b,),
        in_specs=[qspec(OFF_QA // ATT_W), qspec(OFF_KA // ATT_W), qspec(R_VA // ATT_W)] + cspecs,
        out_specs=[pl.BlockSpec((1, DEC_PAD, ATT_OUT), lambda i: (i, 0, 0))] + cspecs,
        compiler_params=pltpu.CompilerParams(
            dimension_semantics=("arbitrary",),
            vmem_limit_bytes=_vmem_limit(vm)),
        name="attn_decode",
    )(zqk8, zqk8, zrest8, *caches)


def _hgrn_consts(c):
    t = np.arange(c)[:, None]
    u = np.arange(c)[None, :]
    seg = [(u <= t)]
    cross = []
    m = c // 2
    while m >= 1:
        blk_t, off_t = t // (2 * m), t % (2 * m)
        piv = blk_t * 2 * m + m - 1
        upper = off_t >= m
        seg.append(np.where(upper, (u > piv) & (u <= t), (u > t) & (u <= piv)))
        cross.append((t // (2 * m) == u // (2 * m)) & (t % (2 * m) >= m) & (u % (2 * m) < m))
        m //= 2
    seg.append(u > t)
    cross.append(t == u)
    return (np.stack(seg).astype(np.float32).reshape(-1, c),
            np.stack(cross).astype(np.float32))


def _hgrn_kernel(q_ref, f_ref, i_ref, g_ref, lb_ref, nw_ref, seg_ref, cross_ref, o_ref, s_out_ref, s_scr,
                 *, tb, n_lvl):
    n = pl.program_id(1)
    c = HG_CHUNK

    @pl.when(n == 0)
    def _():
        s_scr[...] = jnp.zeros_like(s_scr)

    lb = lb_ref[...]
    mx = jnp.maximum(lb[0:1, :], lb[1:2, :])
    e0, e1 = jnp.exp(lb[0:1, :] - mx), jnp.exp(lb[1:2, :] - mx)
    lower = e0 / (e0 + e1)
    nw = nw_ref[...]
    seg = seg_ref[...]

    chunks = range(tb // c)
    nt = (((1,), (1,)), ((), ()))
    qs, ks, vs, ws = [], [], [], []
    for ch in chunks:
        rs = slice(ch * c, (ch + 1) * c)
        fg = lower + (1.0 - lower) * jax.nn.sigmoid(f_ref[rs, :])
        lg = jnp.log(fg)
        lg_hi = lg.astype(BF16)
        lg_lo = (lg - lg_hi.astype(F32)).astype(BF16)
        dsum = (jnp.dot(seg, lg_hi, preferred_element_type=F32)
                + jnp.dot(seg, lg_lo, preferred_element_type=F32))
        ws.append(jnp.exp(dsum))
        qs.append(_silu(q_ref[rs, :]))
        ks.append(1.0 - fg)
        vs.append(i_ref[rs, :].astype(BF16))
    atts = [cross_ref[n_lvl] * lax.dot_general(qs[ch].astype(BF16), ks[ch].astype(BF16), nt,
                                               preferred_element_type=F32) for ch in chunks]
    for lv in range(n_lvl):
        for ch in chunks:
            wl = ws[ch][(1 + lv) * c:(2 + lv) * c]
            atts[ch] = atts[ch] + cross_ref[lv] * lax.dot_general(
                (qs[ch] * wl).astype(BF16), (ks[ch] * wl).astype(BF16), nt, preferred_element_type=F32)
    o_intra = [jnp.dot(atts[ch].astype(BF16), vs[ch], preferred_element_type=F32) for ch in chunks]
    q_dec = [(qs[ch] * ws[ch][0:c]).astype(BF16) for ch in chunks]
    k_end_t = [(ks[ch] * ws[ch][(n_lvl + 1) * c:(n_lvl + 2) * c]).T.astype(BF16) for ch in chunks]
    decay_col = [jnp.broadcast_to(ws[ch][c - 1:c, :], (HG_DK, HG_DV)).T for ch in chunks]
    kv = [jnp.dot(k_end_t[ch], vs[ch], preferred_element_type=F32) for ch in chunks]
    s_cur = s_scr[...]
    for ch in chunks:
        rs = slice(ch * c, (ch + 1) * c)
        o = o_intra[ch] + jnp.dot(q_dec[ch], s_cur.astype(BF16), preferred_element_type=F32)
        s_cur = decay_col[ch] * s_cur + kv[ch]
        ms = jnp.mean(o * o, axis=-1, keepdims=True)
        o_ref[rs, :] = (o * lax.rsqrt(ms + EPS) * nw * _silu(g_ref[rs, :])).astype(o_ref.dtype)
    s_scr[...] = s_cur

    @pl.when(n == pl.num_programs(1) - 1)
    def _():
        s_out_ref[0] = s_scr[...]


def _hgrn_prompt(z, hg_lb, hg_norm_w, tb):
    t = z.shape[0]
    seg_np, cross_np = _hgrn_consts(HG_CHUNK)
    n_lvl = cross_np.shape[0] - 1
    seg = jnp.asarray(seg_np, dtype=BF16)
    cross = jnp.asarray(cross_np, dtype=F32)
    col = lambda off: pl.BlockSpec((tb, HG_DK), lambda h, n, o=off // HG_DK: (n, o + h))
    kern = functools.partial(_hgrn_kernel, tb=tb, n_lvl=n_lvl)
    vm = 2 * 5 * tb * HG_DK * 4 + 2 * seg_np.size * 2 + 2 * cross_np.size * 4 + 16 * 1024 * 1024
    return pl.pallas_call(
        kern,
        out_shape=(jax.ShapeDtypeStruct((t, HG_WV), BF16),
                   jax.ShapeDtypeStruct((HG_HEADS, HG_DK, HG_DV), F32)),
        grid=(HG_HEADS, t // tb),
        in_specs=[col(R_QH), col(R_FH), col(R_IH), col(R_GH),
                  pl.BlockSpec((2, HG_DK), lambda h, n: (0, h)),
                  pl.BlockSpec((1, HG_DV), lambda h, n: (0, 0)),
                  pl.BlockSpec(seg_np.shape, lambda h, n: (0, 0)),
                  pl.BlockSpec(cross_np.shape, lambda h, n: (0, 0, 0))],
        out_specs=(pl.BlockSpec((tb, HG_DV), lambda h, n: (n, h)),
                   pl.BlockSpec((1, HG_DK, HG_DV), lambda h, n: (h, 0, 0))),
        scratch_shapes=[pltpu.VMEM((HG_DK, HG_DV), F32)],
        compiler_params=pltpu.CompilerParams(
            dimension_semantics=("arbitrary", "arbitrary"),
            vmem_limit_bytes=_vmem_limit(vm)),
        name="hgrn_prompt",
    )(z, z, z, z, hg_lb, hg_norm_w, seg, cross)


def _hgrn_dec_kernel(q_ref, f_ref, i_ref, g_ref, lb_ref, nw_ref, s_ref, o_ref, s_out_ref, *, heads):
    lb = lb_ref[...]
    mx = jnp.maximum(lb[0:1, :], lb[1:2, :])
    e0, e1 = jnp.exp(lb[0:1, :] - mx), jnp.exp(lb[1:2, :] - mx)
    lower = e0 / (e0 + e1)
    nw = nw_ref[...]
    zpad = jnp.zeros((HG_DK - DEC_PAD, HG_DK), F32)

    def cols(x):
        return jnp.concatenate([x, zpad], axis=0).T

    for h in range(heads):
        hs = slice(h * HG_DK, (h + 1) * HG_DK)
        q = _silu(q_ref[0, :, hs])
        fg = lower[:, hs] + (1.0 - lower[:, hs]) * jax.nn.sigmoid(f_ref[0, :, hs])
        v = i_ref[0, :, hs]
        q_t, f_t, k_t = cols(q), cols(fg), cols(1.0 - fg)
        s = s_ref[0, h]
        row = lax.broadcasted_iota(jnp.int32, (DEC_PAD, HG_DV), 0)
        o = jnp.zeros((DEC_PAD, HG_DV), F32)
        for t in range(DEC_SEQ):
            bc = lambda a: jnp.broadcast_to(a[:, t:t + 1], (HG_DK, HG_DV))
            s = bc(f_t) * s + bc(k_t) * v[t:t + 1, :]
            o = jnp.where(row == t, jnp.sum(s * bc(q_t), axis=0, keepdims=True), o)
        ms = jnp.mean(o * o, axis=-1, keepdims=True)
        o_ref[0, :, hs] = o * lax.rsqrt(ms + EPS) * nw * _silu(g_ref[0, :, hs])
        s_out_ref[0, h] = s


def _hgrn_decode(z8, hg_lb, hg_norm_w, state):
    b = z8.shape[0]
    heads = COL_BLK // HG_DK
    nhb = HG_HEADS // heads
    col = lambda off: pl.BlockSpec((1, DEC_PAD, COL_BLK), lambda i, j, o=off // COL_BLK: (i, 0, o + j))
    kern = functools.partial(_hgrn_dec_kernel, heads=heads)
    sspec = pl.BlockSpec((1, heads, HG_DK, HG_DV), lambda i, j: (i, j, 0, 0))
    return pl.pallas_call(
        kern,
        out_shape=(jax.ShapeDtypeStruct((b, DEC_PAD, HG_WV), F32),
                   jax.ShapeDtypeStruct(state.shape, F32)),
        grid=(b, nhb),
        in_specs=[col(R_QH), col(R_FH), col(R_IH), col(R_GH),
                  pl.BlockSpec((2, COL_BLK), lambda i, j: (0, j)),
                  pl.BlockSpec((1, HG_DV), lambda i, j: (0, 0)),
                  sspec],
        out_specs=(pl.BlockSpec((1, DEC_PAD, COL_BLK), lambda i, j: (i, 0, j)), sspec),
        compiler_params=pltpu.CompilerParams(
            dimension_semantics=("arbitrary", "arbitrary"),
            vmem_limit_bytes=_vmem_limit(0)),
        name="hgrn_decode",
    )(z8, z8, z8, z8, hg_lb, hg_norm_w, state)


def _merge_kernel(att_ref, hg_ref, ga_ref, gb_ref, x_ref, g1_ref, wpa_ref, wpb_ref, wo_ref, o_ref):
    att, hg = att_ref[...], hg_ref[...]
    parts = []
    for c0 in range(0, wpa_ref.shape[1], MXU_COLS):
        cs = slice(c0, c0 + MXU_COLS)
        ya = jnp.dot(att, wpa_ref[:, cs], preferred_element_type=F32)
        yb = jnp.dot(hg, wpb_ref[:, cs], preferred_element_type=F32)
        parts.append((ga_ref[:, cs].astype(F32) * ya + gb_ref[:, cs].astype(F32) * yb).astype(BF16))
    ymix = jnp.concatenate(parts, axis=1)
    o_ref[...] = x_ref[...] + g1_ref[...] * jnp.dot(ymix, wo_ref[...], preferred_element_type=F32)


def _merge(att, hg, gates, x2d, g1, wpa_bf, wpb_bf, wo_bf, tm):
    rows, d = x2d.shape
    per_row = g1.shape[0] != 1
    resident = pl.Buffered(1)
    g1_spec = (pl.BlockSpec((tm, d), lambda i: (i, 0)) if per_row else pl.BlockSpec((1, d), lambda i: (0, 0)))
    vm = (2 * tm * (ATT_OUT + HG_WV + 2 * d) * 2 + 4 * tm * d * 4 + (ATT_OUT * d + HG_WV * d + d * d) * 2
          + (2 * tm * d * 4 if per_row else 0) + tm * d * 10 + 4 * 1024 * 1024)
    return pl.pallas_call(
        _merge_kernel,
        out_shape=jax.ShapeDtypeStruct((rows, d), F32),
        grid=(rows // tm,),
        in_specs=[pl.BlockSpec((tm, ATT_OUT), lambda i: (i, 0)),
                  pl.BlockSpec((tm, HG_WV), lambda i: (i, 0)),
                  pl.BlockSpec((tm, d), lambda i: (i, 0)),
                  pl.BlockSpec((tm, d), lambda i: (i, 1)),
                  pl.BlockSpec((tm, d), lambda i: (i, 0)),
                  g1_spec,
                  pl.BlockSpec((ATT_OUT, d), lambda i: (0, 0), pipeline_mode=resident),
                  pl.BlockSpec((HG_WV, d), lambda i: (0, 0), pipeline_mode=resident),
                  pl.BlockSpec((d, d), lambda i: (0, 0), pipeline_mode=resident)],
        out_specs=pl.BlockSpec((tm, d), lambda i: (i, 0)),
        compiler_params=pltpu.CompilerParams(
            dimension_semantics=("arbitrary",),
            vmem_limit_bytes=_vmem_limit(vm)),
        name="merge",
    )(att, hg, gates, gates, x2d, g1, wpa_bf, wpb_bf, wo_bf)


def _ffn_kernel(x_ref, sh_ref, sc_ref, g2_ref, nw_ref, nf_ref, wa_ref, wb_ref, wd_ref, cw_ref, cb_ref,
                p1_ref, p2_ref, y_ref, tail_ref, h_scr, acc_scr, carry_scr,
                *, tm, ta, per_row, seq_rows):
    i = pl.program_id(0)
    f = pl.program_id(1)

    @pl.when(f == 0)
    def _():
        _norm_mod_rows(x_ref, nw_ref, sc_ref, sh_ref, h_scr, tm, per_row)
        acc_scr[...] = jnp.zeros_like(acc_scr)

    if seq_rows is None:
        @pl.when(i == 0)
        def _():
            carry_scr[f] = p1_ref[...]

    h = h_scr[...]
    tf = wa_ref.shape[1]
    groups = [slice(c0, c0 + MXU_COLS) for c0 in range(0, tf, MXU_COLS)]
    a_parts = [jnp.dot(h, wa_ref[:, cs], preferred_element_type=F32) for cs in groups]
    b_parts = [jnp.dot(h, wb_ref[:, cs], preferred_element_type=F32) for cs in groups]
    row = lax.broadcasted_iota(jnp.int32, (tm, MXU_COLS), 0)
    ys = []
    for cs, a, b in zip(groups, a_parts, b_parts):
        tail_ref[:, cs] = a[tm - ta:, :]
        r1 = pltpu.roll(a, 1, 0)
        r2 = pltpu.roll(a, 2, 0)
        if seq_rows is None:
            prev = carry_scr[f, :, cs]
            a1 = jnp.where(row == 0, prev[SUBLANES - 1:SUBLANES, :], r1)
            a2 = jnp.where(row == 0, prev[SUBLANES - 2:SUBLANES - 1, :],
                           jnp.where(row == 1, prev[SUBLANES - 1:SUBLANES, :], r2))
            carry_scr[f, :, cs] = a[tm - SUBLANES:, :]
        else:
            t = row & (seq_rows - 1)
            a1 = jnp.where(t == 0, p1_ref[:, cs], r1)
            a2 = jnp.where(t <= 1, p2_ref[:, cs], r2)
        cw = cw_ref[:, cs]
        u = cb_ref[:, cs] + a2 * cw[0:1, :] + a1 * cw[1:2, :] + a * cw[2:3, :]
        ys.append((_silu(u) * b).astype(BF16))
    y = jnp.concatenate(ys, axis=1)
    acc_scr[...] += jnp.dot(y, wd_ref[...], preferred_element_type=F32)

    @pl.when(f == pl.num_programs(1) - 1)
    def _():
        step = min(tm, 128)

        def body(r, carry):
            rs = pl.ds(pl.multiple_of(r * step, step), step)
            g2 = g2_ref[rs, :] if per_row else g2_ref[...]
            x2 = x_ref[rs, :] + g2 * acc_scr[rs, :]
            ms = jnp.mean(x2 * x2, axis=-1, keepdims=True)
            y_ref[rs, :] = x2 * lax.rsqrt(ms + EPS) * nf_ref[...]
            return carry

        lax.fori_loop(0, tm // step, body, 0)


def _ffn(x1, sh, sc, g2, nw, nf, wa_bf, wb_bf, wd_bf, conv_w, conv_b, p1, p2, tm, seq_rows):
    rows, d = x1.shape
    dff = wa_bf.shape[1]
    per_row = sh.shape[0] != 1
    tf = COL_BLK
    nf_blk = dff // tf
    ta = SUBLANES if seq_rows is None else tm
    prow = SUBLANES if seq_rows is None else tm
    kern = functools.partial(_ffn_kernel, tm=tm, ta=ta, per_row=per_row, seq_rows=seq_rows)
    pspec = pl.BlockSpec((prow, tf), lambda i, f: (0 if seq_rows is None else i, f))
    vm = (4 * tm * d * 4 + tm * d * 2 + tm * d * 4 + 2 * 3 * d * tf * 2 + 10 * tm * tf * 4
          + (6 * tm * d * 4 if per_row else 0) + 8 * 1024 * 1024)
    return pl.pallas_call(
        kern,
        out_shape=(jax.ShapeDtypeStruct((rows, d), F32),
                   jax.ShapeDtypeStruct((ta * (rows // tm), dff), F32)),
        grid=(rows // tm, nf_blk),
        in_specs=[pl.BlockSpec((tm, d), lambda i, f: (i, 0)),
                  _mod_spec(per_row, tm, d), _mod_spec(per_row, tm, d), _mod_spec(per_row, tm, d),
                  pl.BlockSpec((1, d), lambda i, f: (0, 0)),
                  pl.BlockSpec((1, d), lambda i, f: (0, 0)),
                  pl.BlockSpec((d, tf), lambda i, f: (0, f)),
                  pl.BlockSpec((d, tf), lambda i, f: (0, f)),
                  pl.BlockSpec((tf, d), lambda i, f: (f, 0)),
                  pl.BlockSpec((CONV_W, tf), lambda i, f: (0, f)),
                  pl.BlockSpec((1, tf), lambda i, f: (0, f)),
                  pspec, pspec],
        out_specs=(pl.BlockSpec((tm, d), lambda i, f: (i, 0)),
                   pl.BlockSpec((ta, tf), lambda i, f: (i, f))),
        scratch_shapes=[pltpu.VMEM((tm, d), BF16), pltpu.VMEM((tm, d), F32),
                        pltpu.VMEM((nf_blk, SUBLANES, tf), F32)],
        compiler_params=pltpu.CompilerParams(
            dimension_semantics=("arbitrary", "arbitrary"),
            vmem_limit_bytes=_vmem_limit(vm)),
        name="ffn",
    )(x1, sh, sc, g2, nw, nf, wa_bf, wb_bf, wd_bf, conv_w, conv_b, p1, p2)


def kernel(x_prompt, x_sample, cache_kv_w128, cache_kv_w512, cache_kv_w2048, state_hgrn, state_conv,
           c_prompt, c_sample, w_ada, b_ada, norm1_w, w_in, hg_lb, hg_norm_w, w_pa, w_pb, w_o,
           norm2_w, w_ffn_a, w_ffn_b, conv_w, conv_b, w_ffn_down, norm_f_w):
    d = D_MODEL
    bp, t, _ = x_prompt.shape
    bs, ts, _ = x_sample.shape
    assert bp == 1 and ts == DEC_SEQ and w_ada.shape[0] == 1
    rows_s = bs * ts

    w_in_bf = w_in[0].astype(BF16)
    wpa_bf, wpb_bf, wo_bf = w_pa[0].astype(BF16), w_pb[0].astype(BF16), w_o[0].astype(BF16)
    wa_bf, wb_bf, wd_bf = w_ffn_a[0].astype(BF16), w_ffn_b[0].astype(BF16), w_ffn_down[0].astype(BF16)
    nw1, nw2, nwf = norm1_w[0].reshape(1, d), norm2_w[0].reshape(1, d), norm_f_w.reshape(1, d)
    hg_nw = hg_norm_w[0].reshape(1, HG_DV)
    cw, cb = conv_w[0], conv_b[0].reshape(1, D_FF)

    n_seq = bp + bs
    pad_rows = -n_seq % SUBLANES
    c_all = jnp.concatenate([c_prompt, c_sample, jnp.zeros((pad_rows, d), F32)], axis=0)
    mod = _ada(c_all, w_ada[0], b_ada[0])
    mod_p = [mod[0:bp, k * d:(k + 1) * d] for k in range(N_MOD)]
    mod_s = [jnp.repeat(mod[bp:n_seq, k * d:(k + 1) * d], ts, axis=0) for k in range(N_MOD)]

    xp = x_prompt.reshape(t, d)
    rope_p = _rope_tables(np.arange(t))
    hp = _norm_mod(xp, mod_p[0], mod_p[1], nw1, tm=1024)
    zqk_p, zr_p, gates_p = _inproj(hp, w_in_bf, rope_p, tm=2048)
    att_p = _attention_prompt(zqk_p, zr_p)
    hg_p, s_p = _hgrn_prompt(zr_p, hg_lb, hg_nw, tb=1024)
    x1p = _merge(att_p, hg_p, gates_p, xp, mod_p[2], wpa_bf, wpb_bf, wo_bf, tm=512)
    conv0 = jnp.zeros((SUBLANES, D_FF), F32)
    yp, tail_p = _ffn(x1p, mod_p[3], mod_p[4], mod_p[5], nw2, nwf, wa_bf, wb_bf, wd_bf, cw, cb,
                      conv0, conv0, tm=512, seq_rows=None)

    kv_p = []
    for g, (win, _) in enumerate(ATT_GROUPS):
        keep = min(win, t)
        ks = zqk_p[t - keep:, OFF_KA + g * ATT_OUT:OFF_KA + (g + 1) * ATT_OUT]
        vs = zr_p[t - keep:, R_VA + g * ATT_OUT:R_VA + (g + 1) * ATT_OUT]
        kv_p.append(jnp.stack([ks.reshape(keep, H_G, HEAD_DIM), vs.reshape(keep, H_G, HEAD_DIM)],
                              axis=1)[None, None])
    hgrn_p = s_p[None, None]
    conv_p = tail_p[tail_p.shape[0] - (CONV_W - 1):][None, None]

    xs = x_sample.reshape(rows_s, d)
    rope_s = _rope_tables(PAST_LEN + np.arange(rows_s) % ts)
    hs = _norm_mod(xs, mod_s[0], mod_s[1], nw1, tm=rows_s)
    zqk_s, zr_s, gates_s = _inproj(hs, w_in_bf, rope_s, tm=rows_s)
    pad8 = lambda z: jnp.pad(z.reshape(bs, ts, z.shape[1]), ((0, 0), (0, DEC_PAD - ts), (0, 0)))
    zqk_s8, zr_s8 = pad8(zqk_s), pad8(zr_s)
    cache_in = (cache_kv_w128, cache_kv_w512, cache_kv_w2048)
    caches = [c.reshape(bs, c.shape[2] * 2 * H_G, HEAD_DIM) for c in cache_in]
    att_s8, *new_caches = _attention_decode(zqk_s8, zr_s8, caches)
    att_s = att_s8[:, :ts].reshape(rows_s, ATT_OUT).astype(BF16)
    kv_s = [n.reshape(c.shape) for n, c in zip(new_caches, cache_in)]
    hg_s8, s_s = _hgrn_decode(zr_s8, hg_lb, hg_nw, state_hgrn[0])
    hg_s = hg_s8[:, :ts].reshape(rows_s, HG_WV).astype(BF16)
    x1s = _merge(att_s, hg_s, gates_s, xs, mod_s[2], wpa_bf, wpb_bf, wo_bf, tm=rows_s)
    buf = state_conv[0]
    zrow = jnp.zeros((bs, 1, D_FF), F32)
    p1 = jnp.concatenate([buf[:, 1:2], zrow, zrow, zrow], axis=1).reshape(rows_s, D_FF)
    p2 = jnp.concatenate([buf[:, 0:1], buf[:, 1:2], zrow, zrow], axis=1).reshape(rows_s, D_FF)
    ys, a_s = _ffn(x1s, mod_s[3], mod_s[4], mod_s[5], nw2, nwf, wa_bf, wb_bf, wd_bf, cw, cb,
                   p1, p2, tm=rows_s, seq_rows=ts)

    hgrn_s = s_s[None]
    conv_s = a_s.reshape(bs, ts, D_FF)[:, ts - (CONV_W - 1):][None]

    return (yp.reshape(bp, t, d), ys.reshape(bs, ts, d),
            kv_p[0], kv_p[1], kv_p[2], hgrn_p, conv_p,
            kv_s[0], kv_s[1], kv_s[2], hgrn_s, conv_s)
```

```python
import functools

import numpy as np
import jax
import jax.numpy as jnp
from jax import lax
from jax.experimental import pallas as pl
from jax.experimental.pallas import tpu as pltpu

F32 = jnp.float32
BF16 = jnp.bfloat16

D_MODEL = 2048
SEQ = 16384
DEC_BATCH = 32
DEC_SEQ = 4
PAST_LEN = 16384

HEAD_DIM = 128
ATT_GROUPS = ((128, 1), (512, 4), (2048, 16))
H_G = 4
N_ATT_HEADS = H_G * len(ATT_GROUPS)
ATT_W = N_ATT_HEADS * HEAD_DIM
ATT_OUT = H_G * HEAD_DIM
ROT_DIM = HEAD_DIM // 4
ROPE_THETA = 500000.0
ATT_SCALE = HEAD_DIM ** -0.5

HG_HEADS = 8
HG_DK = 128
HG_DV = 128
HG_WK = HG_HEADS * HG_DK
HG_WV = HG_HEADS * HG_DV

D_FF = 5632
CONV_W = 3
N_MOD = 6
EPS = 1e-6

IN_SIZES = (ATT_W, ATT_W, ATT_W, HG_WK, HG_WK, HG_WV, HG_WV, D_MODEL, D_MODEL)
IN_TOTAL = sum(IN_SIZES)
IN_OFFS = tuple(int(s) for s in np.cumsum((0,) + IN_SIZES)[:-1])
OFF_QA, OFF_KA, OFF_VA, OFF_QH, OFF_FH, OFF_IH, OFF_GH, OFF_GA, OFF_GB = IN_OFFS
QK_COLS = 2 * ATT_W
R_VA, R_QH, R_FH, R_IH, R_GH, R_GA, R_GB = (o - QK_COLS for o in IN_OFFS[2:])

V7X_VMEM_BYTES = 64 * 1024 * 1024
SUBLANES = 8
LANES = 128
DEC_PAD = SUBLANES

COL_BLK = 512
MXU_COLS = 256
ROPE_ROWS = 256
FFN_ROWS = 512
HG_CHUNK = 128
ATT_QB = 128
ATT_SB = 2048


def _vmem_limit(nbytes):
    return int(min(V7X_VMEM_BYTES - 8 * 1024 * 1024, max(nbytes, 16 * 1024 * 1024)))


def _silu(x):
    return x * jax.nn.sigmoid(x)


def _unroll_for(trips, max_unroll=6):
    return next(u for u in range(min(max_unroll, trips), 0, -1) if trips % u == 0)


def _ada_kernel(c_ref, w_ref, b_ref, o_ref):
    c = c_ref[...]
    s = _silu(c).astype(BF16)
    o_ref[...] = jnp.dot(s, w_ref[...].astype(BF16), preferred_element_type=F32) + b_ref[...]


def _ada(c_all, w_ada, b_ada):
    rows, d = c_all.shape
    n = w_ada.shape[1]
    tn = 1024
    return pl.pallas_call(
        _ada_kernel,
        out_shape=jax.ShapeDtypeStruct((rows, n), F32),
        grid=(n // tn,),
        in_specs=[pl.BlockSpec((rows, d), lambda j: (0, 0)),
                  pl.BlockSpec((d, tn), lambda j: (0, j)),
                  pl.BlockSpec((1, tn), lambda j: (0, j))],
        out_specs=pl.BlockSpec((rows, tn), lambda j: (0, j)),
        compiler_params=pltpu.CompilerParams(
            dimension_semantics=("arbitrary",),
            vmem_limit_bytes=_vmem_limit(2 * d * tn * 4 + d * tn * 2 + 4 * rows * (d + tn) * 4)),
        name="ada",
    )(c_all, w_ada, b_ada.reshape(1, n))


def _norm_mod_rows(x_ref, nw_ref, sc_ref, sh_ref, h_ref, tm, per_row):
    step = min(tm, 128)

    def body(r, carry):
        rs = pl.ds(pl.multiple_of(r * step, step), step)
        x = x_ref[rs, :]
        ms = jnp.mean(x * x, axis=-1, keepdims=True)
        y = x * lax.rsqrt(ms + EPS) * nw_ref[...]
        if per_row:
            h = y * (1.0 + sc_ref[rs, :]) + sh_ref[rs, :]
        else:
            h = y * (1.0 + sc_ref[...]) + sh_ref[...]
        h_ref[rs, :] = h.astype(BF16)
        return carry

    lax.fori_loop(0, tm // step, body, 0)


def _mod_spec(per_row, tm, d):
    if per_row:
        return pl.BlockSpec((tm, d), lambda i, j: (i, 0))
    return pl.BlockSpec((1, d), lambda i, j: (0, 0))


def _norm_kernel(x_ref, sh_ref, sc_ref, nw_ref, h_ref, *, tm, per_row):
    _norm_mod_rows(x_ref, nw_ref, sc_ref, sh_ref, h_ref, tm, per_row)


def _norm_mod(x2d, sh, sc, nw, tm):
    rows, d = x2d.shape
    per_row = sh.shape[0] != 1
    mod = (pl.BlockSpec((tm, d), lambda i: (i, 0)) if per_row else pl.BlockSpec((1, d), lambda i: (0, 0)))
    return pl.pallas_call(
        functools.partial(_norm_kernel, tm=tm, per_row=per_row),
        out_shape=jax.ShapeDtypeStruct((rows, d), BF16),
        grid=(rows // tm,),
        in_specs=[pl.BlockSpec((tm, d), lambda i: (i, 0)), mod, mod,
                  pl.BlockSpec((1, d), lambda i: (0, 0))],
        out_specs=pl.BlockSpec((tm, d), lambda i: (i, 0)),
        compiler_params=pltpu.CompilerParams(
            dimension_semantics=("arbitrary",),
            vmem_limit_bytes=_vmem_limit(2 * tm * d * 6 + (4 * tm * d * 4 if per_row else 0) + 8 * 1024 * 1024)),
        name="norm_mod",
    )(x2d, sh, sc, nw)


def _proj_rope_kernel(h_ref, w_ref, cos_ref, s1_ref, s2_ref, z_ref):
    tm = h_ref.shape[0]
    rstep = min(tm, ROPE_ROWS)
    for r0 in range(0, tm, rstep):
        rs = slice(r0, r0 + rstep)
        h = h_ref[rs, :]
        c, s1, s2 = cos_ref[rs, :], s1_ref[rs, :], s2_ref[rs, :]
        for c0 in range(0, z_ref.shape[1], MXU_COLS):
            grp = jnp.dot(h, w_ref[:, c0:c0 + MXU_COLS], preferred_element_type=F32)
            for hh in range(MXU_COLS // HEAD_DIM):
                blk = grp[:, hh * HEAD_DIM:(hh + 1) * HEAD_DIM]
                z_ref[rs, c0 + hh * HEAD_DIM:c0 + (hh + 1) * HEAD_DIM] = (
                    blk * c + pltpu.roll(blk, HEAD_DIM - ROT_DIM // 2, 1) * s1
                    + pltpu.roll(blk, ROT_DIM // 2, 1) * s2)


def _proj_plain_kernel(h_ref, w_ref, z_ref):
    z_ref[...] = jnp.dot(h_ref[...], w_ref[...], preferred_element_type=F32)


def _proj_gate_kernel(h_ref, w_ref, z_ref):
    tm = h_ref.shape[0]
    rstep = min(tm, ROPE_ROWS)
    for r0 in range(0, tm, rstep):
        rs = slice(r0, r0 + rstep)
        h = h_ref[rs, :]
        for c0 in range(0, z_ref.shape[1], MXU_COLS):
            cs = slice(c0, c0 + MXU_COLS)
            z_ref[rs, cs] = jax.nn.sigmoid(jnp.dot(h, w_ref[:, cs], preferred_element_type=F32)).astype(BF16)


def _inproj(h, w_in_bf, rope, tm):
    rows, d = h.shape

    def call(kern, name, col0, ncols, tn, out_dtype, tables=()):
        out_bytes = jnp.dtype(out_dtype).itemsize
        vm = (2 * tm * d * 2 + 2 * d * tn * 2 + 2 * len(tables) * tm * HEAD_DIM * 4
              + 2 * tm * tn * out_bytes + 2 * tm * min(tn, COL_BLK) * 4 + 4 * 1024 * 1024)
        tab = pl.BlockSpec((tm, HEAD_DIM), lambda i, j: (i, 0))
        return pl.pallas_call(
            kern,
            out_shape=jax.ShapeDtypeStruct((rows, ncols), out_dtype),
            grid=(rows // tm, ncols // tn),
            in_specs=[pl.BlockSpec((tm, d), lambda i, j: (i, 0)),
                      pl.BlockSpec((d, tn), lambda i, j, c=col0 // tn: (0, c + j))] + [tab] * len(tables),
            out_specs=pl.BlockSpec((tm, tn), lambda i, j: (i, j)),
            compiler_params=pltpu.CompilerParams(dimension_semantics=("arbitrary", "arbitrary"),
                                                 vmem_limit_bytes=_vmem_limit(vm)),
            name=name,
        )(h, w_in_bf, *tables)

    z_qk = call(_proj_rope_kernel, "proj_rope", 0, QK_COLS, 2 * COL_BLK, F32, rope)
    z_rest = call(_proj_plain_kernel, "proj_plain", QK_COLS, R_GA, COL_BLK, F32)
    gates = call(_proj_gate_kernel, "proj_gate", QK_COLS + R_GA, 2 * D_MODEL, COL_BLK, BF16)
    return z_qk, z_rest, gates


def _rope_tables(pos):
    half = ROT_DIM // 2
    inv_freq = ROPE_THETA ** (-np.arange(half, dtype=np.float64) * 2.0 / ROT_DIM)
    ang = np.asarray(pos, dtype=np.float64)[:, None] * inv_freq[None, :]
    cos, sin = jnp.asarray(np.cos(ang), dtype=F32), jnp.asarray(np.sin(ang), dtype=F32)
    rows = pos.shape[0]
    ones = jnp.ones((rows, HEAD_DIM - ROT_DIM), F32)
    zer = jnp.zeros((rows, HEAD_DIM - ROT_DIM), F32)
    zh = jnp.zeros((rows, half), F32)
    c = jnp.concatenate([cos, cos, ones], axis=1)
    s1 = jnp.concatenate([-sin, zh, zer], axis=1)
    s2 = jnp.concatenate([zh, sin, zer], axis=1)
    return c, s1, s2


def _attn_tile(q, ka, kb, va, vb, valid):
    k = jnp.concatenate([ka, kb], axis=0).astype(BF16)
    v = jnp.concatenate([va, vb], axis=0).astype(BF16)
    s = lax.dot_general((q * ATT_SCALE).astype(BF16), k, (((1,), (1,)), ((), ())),
                        preferred_element_type=F32)
    s = jnp.where(valid, s, -jnp.inf)
    m = jnp.max(s, axis=1, keepdims=True)
    p = jnp.exp(s - m)
    l = jnp.sum(p, axis=1, keepdims=True)
    o = jnp.dot(p.astype(BF16), v, preferred_element_type=F32) / l
    return o, m + jnp.log(l)


def _attn_kernel(*refs):
    q_refs = refs[0:3]
    kc_refs = refs[3:6]
    kp_refs = refs[6:9]
    vc_refs = refs[9:12]
    vp_refs = refs[12:15]
    o_ref = refs[15]
    og_scr, lse_scr = refs[16], refs[17]
    n = pl.program_id(0)

    row = lax.broadcasted_iota(jnp.int32, (ATT_QB, 2 * ATT_QB), 0)
    col = lax.broadcasted_iota(jnp.int32, (ATT_QB, 2 * ATT_QB), 1)
    band = (col >= row) & (col <= row + ATT_QB)
    band_first = band & (col >= jnp.where(n > 0, 0, ATT_QB))

    for g, (_, dil) in enumerate(ATT_GROUPS):
        q_ref, kc_ref, kp_ref, vc_ref, vp_ref = q_refs[g], kc_refs[g], kp_refs[g], vc_refs[g], vp_refs[g]
        nq = ATT_SB // (dil * ATT_QB)

        def rows(start, count):
            if dil == 1:
                return pl.ds(pl.multiple_of(start, ATT_QB), count)
            return pl.ds(start, count, stride=dil)

        def put(g_, tok0, o, lse):
            og_scr[g_, rows(tok0, ATT_QB), :] = o
            lse_scr[g_, rows(tok0, ATT_QB), :] = jnp.broadcast_to(lse, (ATT_QB, HEAD_DIM))

        def first_body(r, carry):
            o, lse = _attn_tile(q_ref[rows(r, ATT_QB), :],
                                kp_ref[rows(r, ATT_QB), :], kc_ref[rows(r, ATT_QB), :],
                                vp_ref[rows(r, ATT_QB), :], vc_ref[rows(r, ATT_QB), :],
                                band_first)
            put(g, r, o, lse)
            return carry

        lax.fori_loop(0, dil, first_body, 0, unroll=_unroll_for(dil))

        if nq > 1:
            def rest_body(t, carry):
                r = t // (nq - 1)
                u = t % (nq - 1) + 1
                q0 = r + dil * (u * ATT_QB)
                k0 = r + dil * ((u - 1) * ATT_QB)
                k1 = r + dil * (u * ATT_QB)
                o, lse = _attn_tile(q_ref[rows(q0, ATT_QB), :],
                                    kc_ref[rows(k0, ATT_QB), :], kc_ref[rows(k1, ATT_QB), :],
                                    vc_ref[rows(k0, ATT_QB), :], vc_ref[rows(k1, ATT_QB), :],
                                    band)
                put(g, q0, o, lse)
                return carry

            lax.fori_loop(0, dil * (nq - 1), rest_body, 0, unroll=_unroll_for(dil * (nq - 1)))

    def merge_body(c, carry):
        rs = pl.ds(pl.multiple_of(c * 256, 256), 256)
        l0, l1, l2 = lse_scr[0, rs, :], lse_scr[1, rs, :], lse_scr[2, rs, :]
        m = jnp.maximum(jnp.maximum(l0, l1), l2)
        e0, e1, e2 = jnp.exp(l0 - m), jnp.exp(l1 - m), jnp.exp(l2 - m)
        num = e0 * og_scr[0, rs, :] + e1 * og_scr[1, rs, :] + e2 * og_scr[2, rs, :]
        o_ref[rs, :] = (num / (e0 + e1 + e2)).astype(o_ref.dtype)
        return carry

    lax.fori_loop(0, ATT_SB // 256, merge_body, 0)


def _attention_prompt(z_qk, z_rest):
    t = z_qk.shape[0]
    nsb = t // ATT_SB
    in_specs, args = [], []

    def cur(colblk):
        return pl.BlockSpec((ATT_SB, HEAD_DIM), lambda n, h, c=colblk: (n, c + h))

    def prev(colblk, dil):
        rows_p = dil * ATT_QB
        per = ATT_SB // rows_p
        return pl.BlockSpec((rows_p, HEAD_DIM),
                            lambda n, h, c=colblk, per=per: (jnp.maximum(n * per - 1, 0), c + h))

    for g in range(3):
        in_specs.append(cur(OFF_QA // HEAD_DIM + g * H_G)); args.append(z_qk)
    for z, base in ((z_qk, OFF_KA), (z_rest, R_VA)):
        for g in range(3):
            in_specs.append(cur(base // HEAD_DIM + g * H_G)); args.append(z)
        for g, (_, dil) in enumerate(ATT_GROUPS):
            in_specs.append(prev(base // HEAD_DIM + g * H_G, dil)); args.append(z)
    blk = ATT_SB * HEAD_DIM * 4
    prev_rows = sum(d * ATT_QB for _, d in ATT_GROUPS)
    vm = 2 * (9 * blk + 2 * prev_rows * HEAD_DIM * 4) + 2 * blk + 6 * blk + 8 * 1024 * 1024
    return pl.pallas_call(
        _attn_kernel,
        out_shape=jax.ShapeDtypeStruct((t, ATT_OUT), BF16),
        grid=(nsb, H_G),
        in_specs=in_specs,
        out_specs=pl.BlockSpec((ATT_SB, HEAD_DIM), lambda n, h: (n, h)),
        scratch_shapes=[pltpu.VMEM((3, ATT_SB, HEAD_DIM), F32),
                        pltpu.VMEM((3, ATT_SB, HEAD_DIM), F32)],
        compiler_params=pltpu.CompilerParams(
            dimension_semantics=("arbitrary", "arbitrary"),
            vmem_limit_bytes=_vmem_limit(vm)),
        name="attn_prompt",
    )(*args)


def _attn_dec_kernel(q_ref, kn_ref, vn_ref, c0_ref, c1_ref, c2_ref, o_ref, n0_ref, n1_ref, n2_ref):
    caches = (c0_ref, c1_ref, c2_ref)
    news = (n0_ref, n1_ref, n2_ref)
    rows_per_pos = 2 * H_G
    zpad = jnp.zeros((LANES - DEC_PAD, HEAD_DIM), F32)

    for g in range(len(ATT_GROUPS)):
        c_ref, n_ref = caches[g], news[g]
        n_rows = c_ref.shape[1]
        keep = n_rows - DEC_SEQ * rows_per_pos
        n_ref[0, 0:keep, :] = c_ref[0, DEC_SEQ * rows_per_pos:n_rows, :]
        for t in range(DEC_SEQ):
            for h in range(H_G):
                hs = slice((g * H_G + h) * HEAD_DIM, (g * H_G + h + 1) * HEAD_DIM)
                r = keep + t * rows_per_pos + h
                n_ref[0, r:r + 1, :] = kn_ref[0, t:t + 1, hs]
                n_ref[0, r + H_G:r + H_G + 1, :] = vn_ref[0, t:t + 1, hs]

    for h in range(H_G):
        outs, lses = [], []
        for g, (win, dil) in enumerate(ATT_GROUPS):
            c_ref = caches[g]
            p_len = c_ref.shape[1] // rows_per_pos
            hs = slice((g * H_G + h) * HEAD_DIM, (g * H_G + h + 1) * HEAD_DIM)
            q = (q_ref[0, :, hs] * ATT_SCALE).astype(BF16)
            k = jnp.concatenate([c_ref[0, pl.ds(h, p_len, stride=rows_per_pos), :], kn_ref[0, :, hs], zpad],
                                axis=0).astype(BF16)
            v = jnp.concatenate([c_ref[0, pl.ds(H_G + h, p_len, stride=rows_per_pos), :],
                                 vn_ref[0, :, hs], zpad], axis=0).astype(BF16)
            s = lax.dot_general(q, k, (((1,), (1,)), ((), ())), preferred_element_type=F32)
            tq = lax.broadcasted_iota(jnp.int32, s.shape, 0) & (DEC_SEQ - 1)
            col = lax.broadcasted_iota(jnp.int32, s.shape, 1)
            delta = p_len + tq - col
            valid = ((delta >= 0) & ((delta & (dil - 1)) == 0) & (delta <= win)
                     & (col < p_len + DEC_SEQ))
            s = jnp.where(valid, s, -jnp.inf)
            m = jnp.max(s, axis=1, keepdims=True)
            p = jnp.exp(s - m)
            l = jnp.sum(p, axis=1, keepdims=True)
            outs.append(jnp.dot(p.astype(BF16), v, preferred_element_type=F32) / l)
            lses.append(m + jnp.log(l))
        m = jnp.maximum(jnp.maximum(lses[0], lses[1]), lses[2])
        e = [jnp.exp(x - m) for x in lses]
        num = e[0] * outs[0] + e[1] * outs[1] + e[2] * outs[2]
        o_ref[0, :, h * HEAD_DIM:(h + 1) * HEAD_DIM] = num / (e[0] + e[1] + e[2])


def _attention_decode(zqk8, zrest8, caches):
    b = zqk8.shape[0]
    qspec = lambda c: pl.BlockSpec((1, DEC_PAD, ATT_W), lambda i, c=c: (i, 0, c))
    cspecs = [pl.BlockSpec((1, c.shape[1], c.shape[2]), lambda i: (i, 0, 0)) for c in caches]
    vm = 4 * sum(c.shape[1] * c.shape[2] * 4 for c in caches) + 12 * 1024 * 1024
    return pl.pallas_call(
        _attn_dec_kernel,
        out_shape=[jax.ShapeDtypeStruct((b, DEC_PAD, ATT_OUT), F32)]
        + [jax.ShapeDtypeStruct(c.shape, F32) for c in caches],
        grid=(b,),
        in_specs=[qspec(OFF_QA // ATT_W), qspec(OFF_KA // ATT_W), qspec(R_VA // ATT_W)] + cspecs,
        out_specs=[pl.BlockSpec((1, DEC_PAD, ATT_OUT), lambda i: (i, 0, 0))] + cspecs,
        compiler_params=pltpu.CompilerParams(
            dimension_semantics=("arbitrary",),
            vmem_limit_bytes=_vmem_limit(vm)),
        name="attn_decode",
    )(zqk8, zqk8, zrest8, *caches)


def _hgrn_consts(c):
    t = np.arange(c)[:, None]
    u = np.arange(c)[None, :]
    seg = [(u <= t)]
    cross = []
    m = c // 2
    while m >= 1:
        blk_t, off_t = t // (2 * m), t % (2 * m)
        piv = blk_t * 2 * m + m - 1
        upper = off_t >= m
        if 1 < m < SUBLANES:
            seg.append(np.where(upper, (u > piv) & (u <= t), (u > t) & (u <= piv)))
        cross.append((t // (2 * m) == u // (2 * m)) & (t % (2 * m) >= m) & (u % (2 * m) < m))
        m //= 2
    cross.append(t == u)
    return (np.stack(seg).astype(np.float32).reshape(-1, c),
            np.stack(cross).astype(np.float32))


def _hgrn_level_weights(b, dsum, fg, c):
    out = []
    m = c // 2
    low = 1
    while m >= 1:
        if m >= SUBLANES:
            parts = []
            for k0 in range(0, c, 2 * m):
                piv = b[k0 + m - 1:k0 + m, :]
                parts += [piv - b[k0:k0 + m, :], b[k0 + m:k0 + 2 * m, :] - piv]
            out.append(jnp.exp2(jnp.concatenate(parts, axis=0)))
        elif m > 1:
            out.append(jnp.exp2(dsum[low * c:(low + 1) * c]))
            low += 1
        else:
            odd = (lax.broadcasted_iota(jnp.int32, fg.shape, 0) & 1) == 1
            out.append(jnp.where(odd, fg, 1.0))
        m //= 2
    return out


def _hgrn_kernel(q_ref, f_ref, i_ref, g_ref, lb_ref, nw_ref, seg_ref, cross_ref, o_ref, s_out_ref, s_scr,
                 *, tb, n_lvl):
    n = pl.program_id(1)
    c = HG_CHUNK

    @pl.when(n == 0)
    def _():
        s_scr[...] = jnp.zeros_like(s_scr)

    lb = lb_ref[...]
    mx = jnp.maximum(lb[0:1, :], lb[1:2, :])
    e0, e1 = jnp.exp(lb[0:1, :] - mx), jnp.exp(lb[1:2, :] - mx)
    lower = e0 / (e0 + e1)
    nw = nw_ref[...]
    seg = seg_ref[...]

    chunks = range(tb // c)
    nt = (((1,), (1,)), ((), ()))
    qs, ks, vs, ws, w_pre, w_suf = [], [], [], [], [], []
    for ch in chunks:
        rs = slice(ch * c, (ch + 1) * c)
        fg = lower + (1.0 - lower) * jax.nn.sigmoid(f_ref[rs, :])
        lg = jnp.log2(fg)
        lg_hi = lg.astype(BF16)
        lg_lo = (lg - lg_hi.astype(F32)).astype(BF16)
        dsum = (jnp.dot(seg, lg_hi, preferred_element_type=F32)
                + jnp.dot(seg, lg_lo, preferred_element_type=F32))
        b = dsum[0:c]
        ws.append(_hgrn_level_weights(b, dsum, fg, c))
        w_pre.append(jnp.exp2(b))
        w_suf.append(jnp.exp2(b[c - 1:c, :] - b))
        qs.append(_silu(q_ref[rs, :]))
        ks.append(1.0 - fg)
        vs.append(i_ref[rs, :].astype(BF16))
    qb = [qs[ch].astype(BF16) for ch in chunks]
    kb = [ks[ch].astype(BF16) for ch in chunks]
    atts = [cross_ref[n_lvl] * lax.dot_general(qb[ch], kb[ch], nt, preferred_element_type=F32)
            for ch in chunks]
    for lv in range(n_lvl):
        for ch in chunks:
            wl = ws[ch][lv].astype(BF16)
            atts[ch] = atts[ch] + cross_ref[lv] * lax.dot_general(
                qb[ch] * wl, kb[ch] * wl, nt, preferred_element_type=F32)
    o_intra = [jnp.dot(atts[ch].astype(BF16), vs[ch], preferred_element_type=F32) for ch in chunks]
    q_dec = [(qs[ch] * w_pre[ch]).astype(BF16) for ch in chunks]
    k_end_t = [(ks[ch] * w_suf[ch]).T.astype(BF16) for ch in chunks]
    decay_col = [jnp.broadcast_to(w_pre[ch][c - 1:c, :], (HG_DK, HG_DV)).T for ch in chunks]
    kv = [jnp.dot(k_end_t[ch], vs[ch], preferred_element_type=F32) for ch in chunks]
    s_cur = s_scr[...]
    for ch in chunks:
        rs = slice(ch * c, (ch + 1) * c)
        o = o_intra[ch] + jnp.dot(q_dec[ch], s_cur.astype(BF16), preferred_element_type=F32)
        s_cur = decay_col[ch] * s_cur + kv[ch]
        ms = jnp.mean(o * o, axis=-1, keepdims=True)
        o_ref[rs, :] = (o * lax.rsqrt(ms + EPS) * nw * _silu(g_ref[rs, :])).astype(o_ref.dtype)
    s_scr[...] = s_cur

    @pl.when(n == pl.num_programs(1) - 1)
    def _():
        s_out_ref[0] = s_scr[...]


def _hgrn_prompt(z, hg_lb, hg_norm_w, tb):
    t = z.shape[0]
    seg_np, cross_np = _hgrn_consts(HG_CHUNK)
    n_lvl = cross_np.shape[0] - 1
    seg = jnp.asarray(seg_np, dtype=BF16)
    cross = jnp.asarray(cross_np, dtype=F32)
    col = lambda off: pl.BlockSpec((tb, HG_DK), lambda h, n, o=off // HG_DK: (n, o + h))
    kern = functools.partial(_hgrn_kernel, tb=tb, n_lvl=n_lvl)
    vm = 2 * 5 * tb * HG_DK * 4 + 2 * seg_np.size * 2 + 2 * cross_np.size * 4 + 16 * 1024 * 1024
    return pl.pallas_call(
        kern,
        out_shape=(jax.ShapeDtypeStruct((t, HG_WV), BF16),
                   jax.ShapeDtypeStruct((HG_HEADS, HG_DK, HG_DV), F32)),
        grid=(HG_HEADS, t // tb),
        in_specs=[col(R_QH), col(R_FH), col(R_IH), col(R_GH),
                  pl.BlockSpec((2, HG_DK), lambda h, n: (0, h)),
                  pl.BlockSpec((1, HG_DV), lambda h, n: (0, 0)),
                  pl.BlockSpec(seg_np.shape, lambda h, n: (0, 0)),
                  pl.BlockSpec(cross_np.shape, lambda h, n: (0, 0, 0))],
        out_specs=(pl.BlockSpec((tb, HG_DV), lambda h, n: (n, h)),
                   pl.BlockSpec((1, HG_DK, HG_DV), lambda h, n: (h, 0, 0))),
        scratch_shapes=[pltpu.VMEM((HG_DK, HG_DV), F32)],
        compiler_params=pltpu.CompilerParams(
            dimension_semantics=("arbitrary", "arbitrary"),
            vmem_limit_bytes=_vmem_limit(vm)),
        name="hgrn_prompt",
    )(z, z, z, z, hg_lb, hg_norm_w, seg, cross)


def _hgrn_dec_kernel(q_ref, f_ref, i_ref, g_ref, lb_ref, nw_ref, s_ref, o_ref, s_out_ref, *, heads):
    lb = lb_ref[...]
    mx = jnp.maximum(lb[0:1, :], lb[1:2, :])
    e0, e1 = jnp.exp(lb[0:1, :] - mx), jnp.exp(lb[1:2, :] - mx)
    lower = e0 / (e0 + e1)
    nw = nw_ref[...]
    zpad = jnp.zeros((HG_DK - DEC_PAD, HG_DK), F32)
    row = lax.broadcasted_iota(jnp.int32, (DEC_PAD, HG_DV), 0)

    def before(x, d):
        return pltpu.roll(x, d, 0)

    def after(x, d):
        return pltpu.roll(x, DEC_PAD - d, 0)

    def split(x):
        hi = x.astype(BF16)
        return hi, (x - hi.astype(F32)).astype(BF16)

    def dot3(a, b):
        (ah, al), (bh, bl) = split(a), split(b)
        return (jnp.dot(ah, bh, preferred_element_type=F32) + jnp.dot(ah, bl, preferred_element_type=F32)
                + jnp.dot(al, bh, preferred_element_type=F32))

    for h in range(heads):
        hs = slice(h * HG_DK, (h + 1) * HG_DK)
        q = _silu(q_ref[0, :, hs])
        fg = lower[:, hs] + (1.0 - lower[:, hs]) * jax.nn.sigmoid(f_ref[0, :, hs])
        k = 1.0 - fg
        v = i_ref[0, :, hs]
        s0 = s_ref[0, h]
        p = [jnp.ones_like(fg), fg]
        for d in range(2, DEC_SEQ + 1):
            p.append(p[d - 1] * before(fg, d - 1))
        f_cum = p[DEC_SEQ]
        for t in range(DEC_SEQ - 1):
            f_cum = jnp.where(row == t, p[t + 1], f_cum)
        g_suf = jnp.ones_like(fg)
        run = jnp.ones_like(fg)
        for d in range(1, DEC_SEQ):
            run = run * after(fg, d)
            g_suf = jnp.where(row == DEC_SEQ - 1 - d, run, g_suf)
        o = dot3(q * f_cum, s0)
        for d in range(DEC_SEQ):
            coef = jnp.sum(q * p[d] * (before(k, d) if d else k), axis=1, keepdims=True)
            o = o + jnp.where(row >= d, coef, 0.0) * (before(v, d) if d else v)
        kd = jnp.where(row < DEC_SEQ, k * g_suf, 0.0)
        kd_t = jnp.concatenate([kd, zpad], axis=0).T
        v_pad = jnp.concatenate([v, zpad], axis=0)
        decay_col = jnp.broadcast_to(f_cum[DEC_SEQ - 1:DEC_SEQ, :], (HG_DK, HG_DV)).T
        s_out_ref[0, h] = decay_col * s0 + dot3(kd_t, v_pad)
        ms = jnp.mean(o * o, axis=-1, keepdims=True)
        o_ref[0, :, hs] = o * lax.rsqrt(ms + EPS) * nw * _silu(g_ref[0, :, hs])


def _hgrn_decode(z8, hg_lb, hg_norm_w, state):
    b = z8.shape[0]
    heads = COL_BLK // HG_DK
    nhb = HG_HEADS // heads
    col = lambda off: pl.BlockSpec((1, DEC_PAD, COL_BLK), lambda i, j, o=off // COL_BLK: (i, 0, o + j))
    kern = functools.partial(_hgrn_dec_kernel, heads=heads)
    sspec = pl.BlockSpec((1, heads, HG_DK, HG_DV), lambda i, j: (i, j, 0, 0))
    return pl.pallas_call(
        kern,
        out_shape=(jax.ShapeDtypeStruct((b, DEC_PAD, HG_WV), F32),
                   jax.ShapeDtypeStruct(state.shape, F32)),
        grid=(b, nhb),
        in_specs=[col(R_QH), col(R_FH), col(R_IH), col(R_GH),
                  pl.BlockSpec((2, COL_BLK), lambda i, j: (0, j)),
                  pl.BlockSpec((1, HG_DV), lambda i, j: (0, 0)),
                  sspec],
        out_specs=(pl.BlockSpec((1, DEC_PAD, COL_BLK), lambda i, j: (i, 0, j)), sspec),
        compiler_params=pltpu.CompilerParams(
            dimension_semantics=("arbitrary", "arbitrary"),
            vmem_limit_bytes=_vmem_limit(0)),
        name="hgrn_decode",
    )(z8, z8, z8, z8, hg_lb, hg_norm_w, state)


def _merge_kernel(att_ref, hg_ref, ga_ref, gb_ref, x_ref, g1_ref, wpa_ref, wpb_ref, wo_ref, o_ref):
    att, hg = att_ref[...], hg_ref[...]
    parts = []
    for c0 in range(0, wpa_ref.shape[1], MXU_COLS):
        cs = slice(c0, c0 + MXU_COLS)
        ya = jnp.dot(att, wpa_ref[:, cs], preferred_element_type=F32)
        yb = jnp.dot(hg, wpb_ref[:, cs], preferred_element_type=F32)
        parts.append((ga_ref[:, cs].astype(F32) * ya + gb_ref[:, cs].astype(F32) * yb).astype(BF16))
    ymix = jnp.concatenate(parts, axis=1)
    o_ref[...] = x_ref[...] + g1_ref[...] * jnp.dot(ymix, wo_ref[...], preferred_element_type=F32)


def _merge(att, hg, gates, x2d, g1, wpa_bf, wpb_bf, wo_bf, tm):
    rows, d = x2d.shape
    per_row = g1.shape[0] != 1
    resident = pl.Buffered(1)
    g1_spec = (pl.BlockSpec((tm, d), lambda i: (i, 0)) if per_row else pl.BlockSpec((1, d), lambda i: (0, 0)))
    vm = (2 * tm * (ATT_OUT + HG_WV + 2 * d) * 2 + 4 * tm * d * 4 + (ATT_OUT * d + HG_WV * d + d * d) * 2
          + (2 * tm * d * 4 if per_row else 0) + tm * d * 10 + 4 * 1024 * 1024)
    return pl.pallas_call(
        _merge_kernel,
        out_shape=jax.ShapeDtypeStruct((rows, d), F32),
        grid=(rows // tm,),
        in_specs=[pl.BlockSpec((tm, ATT_OUT), lambda i: (i, 0)),
                  pl.BlockSpec((tm, HG_WV), lambda i: (i, 0)),
                  pl.BlockSpec((tm, d), lambda i: (i, 0)),
                  pl.BlockSpec((tm, d), lambda i: (i, 1)),
                  pl.BlockSpec((tm, d), lambda i: (i, 0)),
                  g1_spec,
                  pl.BlockSpec((ATT_OUT, d), lambda i: (0, 0), pipeline_mode=resident),
                  pl.BlockSpec((HG_WV, d), lambda i: (0, 0), pipeline_mode=resident),
                  pl.BlockSpec((d, d), lambda i: (0, 0), pipeline_mode=resident)],
        out_specs=pl.BlockSpec((tm, d), lambda i: (i, 0)),
        compiler_params=pltpu.CompilerParams(
            dimension_semantics=("arbitrary",),
            vmem_limit_bytes=_vmem_limit(vm)),
        name="merge",
    )(att, hg, gates, gates, x2d, g1, wpa_bf, wpb_bf, wo_bf)


def _ffn_kernel(x_ref, sh_ref, sc_ref, g2_ref, nw_ref, nf_ref, wa_ref, wb_ref, wd_ref, cw_ref, cb_ref,
                p1_ref, p2_ref, y_ref, tail_ref, h_scr, acc_scr, carry_scr,
                *, tm, ta, per_row, seq_rows):
    i = pl.program_id(0)
    f = pl.program_id(1)

    @pl.when(f == 0)
    def _():
        _norm_mod_rows(x_ref, nw_ref, sc_ref, sh_ref, h_scr, tm, per_row)
        acc_scr[...] = jnp.zeros_like(acc_scr)

    if seq_rows is None:
        @pl.when(i == 0)
        def _():
            carry_scr[f] = p1_ref[...]

    tf = wa_ref.shape[1]
    groups = [slice(c0, c0 + MXU_COLS) for c0 in range(0, tf, MXU_COLS)]
    rc = min(tm, FFN_ROWS)
    row = lax.broadcasted_iota(jnp.int32, (rc, MXU_COLS), 0)
    if seq_rows is None:
        tails = [carry_scr[f, :, cs] for cs in groups]
    for r0 in range(0, tm, rc):
        rs = slice(r0, r0 + rc)
        h = h_scr[rs, :]
        a_parts = [jnp.dot(h, wa_ref[:, cs], preferred_element_type=F32) for cs in groups]
        b_parts = [jnp.dot(h, wb_ref[:, cs], preferred_element_type=F32) for cs in groups]
        ys = []
        for gi, (cs, a, b) in enumerate(zip(groups, a_parts, b_parts)):
            r1 = pltpu.roll(a, 1, 0)
            r2 = pltpu.roll(a, 2, 0)
            if seq_rows is None:
                prev = tails[gi]
                a1 = jnp.where(row == 0, prev[SUBLANES - 1:SUBLANES, :], r1)
                a2 = jnp.where(row == 0, prev[SUBLANES - 2:SUBLANES - 1, :],
                               jnp.where(row == 1, prev[SUBLANES - 1:SUBLANES, :], r2))
                tails[gi] = a[rc - SUBLANES:, :]
            else:
                t = row & (seq_rows - 1)
                a1 = jnp.where(t == 0, p1_ref[rs, cs], r1)
                a2 = jnp.where(t <= 1, p2_ref[rs, cs], r2)
                tail_ref[rs, cs] = a
            cw = cw_ref[:, cs]
            u = cb_ref[:, cs] + a2 * cw[0:1, :] + a1 * cw[1:2, :] + a * cw[2:3, :]
            ys.append((_silu(u) * b).astype(BF16))
        y = jnp.concatenate(ys, axis=1)
        acc_scr[rs, :] += jnp.dot(y, wd_ref[...], preferred_element_type=F32)
    if seq_rows is None:
        for cs, tail in zip(groups, tails):
            carry_scr[f, :, cs] = tail
            tail_ref[:, cs] = tail

    @pl.when(f == pl.num_programs(1) - 1)
    def _():
        step = min(tm, 128)

        def body(r, carry):
            rs = pl.ds(pl.multiple_of(r * step, step), step)
            g2 = g2_ref[rs, :] if per_row else g2_ref[...]
            x2 = x_ref[rs, :] + g2 * acc_scr[rs, :]
            ms = jnp.mean(x2 * x2, axis=-1, keepdims=True)
            y_ref[rs, :] = x2 * lax.rsqrt(ms + EPS) * nf_ref[...]
            return carry

        lax.fori_loop(0, tm // step, body, 0)


def _ffn(x1, sh, sc, g2, nw, nf, wa_bf, wb_bf, wd_bf, conv_w, conv_b, p1, p2, tm, seq_rows):
    rows, d = x1.shape
    dff = wa_bf.shape[1]
    per_row = sh.shape[0] != 1
    tf = COL_BLK
    nf_blk = dff // tf
    ta = SUBLANES if seq_rows is None else tm
    prow = SUBLANES if seq_rows is None else tm
    kern = functools.partial(_ffn_kernel, tm=tm, ta=ta, per_row=per_row, seq_rows=seq_rows)
    pspec = pl.BlockSpec((prow, tf), lambda i, f: (0 if seq_rows is None else i, f))
    vm = (4 * tm * d * 4 + tm * d * 2 + tm * d * 4 + 2 * 3 * d * tf * 2 + 10 * tm * tf * 4
          + (6 * tm * d * 4 if per_row else 0) + 8 * 1024 * 1024)
    return pl.pallas_call(
        kern,
        out_shape=(jax.ShapeDtypeStruct((rows, d), F32),
                   jax.ShapeDtypeStruct((ta * (rows // tm), dff), F32)),
        grid=(rows // tm, nf_blk),
        in_specs=[pl.BlockSpec((tm, d), lambda i, f: (i, 0)),
                  _mod_spec(per_row, tm, d), _mod_spec(per_row, tm, d), _mod_spec(per_row, tm, d),
                  pl.BlockSpec((1, d), lambda i, f: (0, 0)),
                  pl.BlockSpec((1, d), lambda i, f: (0, 0)),
                  pl.BlockSpec((d, tf), lambda i, f: (0, f)),
                  pl.BlockSpec((d, tf), lambda i, f: (0, f)),
                  pl.BlockSpec((tf, d), lambda i, f: (f, 0)),
                  pl.BlockSpec((CONV_W, tf), lambda i, f: (0, f)),
                  pl.BlockSpec((1, tf), lambda i, f: (0, f)),
                  pspec, pspec],
        out_specs=(pl.BlockSpec((tm, d), lambda i, f: (i, 0)),
                   pl.BlockSpec((ta, tf), lambda i, f: (i, f))),
        scratch_shapes=[pltpu.VMEM((tm, d), BF16), pltpu.VMEM((tm, d), F32),
                        pltpu.VMEM((nf_blk, SUBLANES, tf), F32)],
        compiler_params=pltpu.CompilerParams(
            dimension_semantics=("arbitrary", "arbitrary"),
            vmem_limit_bytes=_vmem_limit(vm)),
        name="ffn",
    )(x1, sh, sc, g2, nw, nf, wa_bf, wb_bf, wd_bf, conv_w, conv_b, p1, p2)


def kernel(x_prompt, x_sample, cache_kv_w128, cache_kv_w512, cache_kv_w2048, state_hgrn, state_conv,
           c_prompt, c_sample, w_ada, b_ada, norm1_w, w_in, hg_lb, hg_norm_w, w_pa, w_pb, w_o,
           norm2_w, w_ffn_a, w_ffn_b, conv_w, conv_b, w_ffn_down, norm_f_w):
    d = D_MODEL
    bp, t, _ = x_prompt.shape
    bs, ts, _ = x_sample.shape
    assert bp == 1 and ts == DEC_SEQ and w_ada.shape[0] == 1
    rows_s = bs * ts

    w_in_bf = w_in[0].astype(BF16)
    wpa_bf, wpb_bf, wo_bf = w_pa[0].astype(BF16), w_pb[0].astype(BF16), w_o[0].astype(BF16)
    wa_bf, wb_bf, wd_bf = w_ffn_a[0].astype(BF16), w_ffn_b[0].astype(BF16), w_ffn_down[0].astype(BF16)
    nw1, nw2, nwf = norm1_w[0].reshape(1, d), norm2_w[0].reshape(1, d), norm_f_w.reshape(1, d)
    hg_nw = hg_norm_w[0].reshape(1, HG_DV)
    cw, cb = conv_w[0], conv_b[0].reshape(1, D_FF)

    n_seq = bp + bs
    pad_rows = -n_seq % SUBLANES
    c_all = jnp.concatenate([c_prompt, c_sample, jnp.zeros((pad_rows, d), F32)], axis=0)
    mod = _ada(c_all, w_ada[0], b_ada[0])
    mod_p = [mod[0:bp, k * d:(k + 1) * d] for k in range(N_MOD)]
    mod_s = [jnp.repeat(mod[bp:n_seq, k * d:(k + 1) * d], ts, axis=0) for k in range(N_MOD)]

    xp = x_prompt.reshape(t, d)
    rope_p = _rope_tables(np.arange(t))
    hp = _norm_mod(xp, mod_p[0], mod_p[1], nw1, tm=1024)
    zqk_p, zr_p, gates_p = _inproj(hp, w_in_bf, rope_p, tm=2048)
    att_p = _attention_prompt(zqk_p, zr_p)
    hg_p, s_p = _hgrn_prompt(zr_p, hg_lb, hg_nw, tb=1024)
    x1p = _merge(att_p, hg_p, gates_p, xp, mod_p[2], wpa_bf, wpb_bf, wo_bf, tm=512)
    conv0 = jnp.zeros((SUBLANES, D_FF), F32)
    yp, tail_p = _ffn(x1p, mod_p[3], mod_p[4], mod_p[5], nw2, nwf, wa_bf, wb_bf, wd_bf, cw, cb,
                      conv0, conv0, tm=512, seq_rows=None)

    kv_p = []
    for g, (win, _) in enumerate(ATT_GROUPS):
        keep = min(win, t)
        ks = zqk_p[t - keep:, OFF_KA + g * ATT_OUT:OFF_KA + (g + 1) * ATT_OUT]
        vs = zr_p[t - keep:, R_VA + g * ATT_OUT:R_VA + (g + 1) * ATT_OUT]
        kv_p.append(jnp.stack([ks.reshape(keep, H_G, HEAD_DIM), vs.reshape(keep, H_G, HEAD_DIM)],
                              axis=1)[None, None])
    hgrn_p = s_p[None, None]
    conv_p = tail_p[tail_p.shape[0] - (CONV_W - 1):][None, None]

    xs = x_sample.reshape(rows_s, d)
    rope_s = _rope_tables(PAST_LEN + np.arange(rows_s) % ts)
    hs = _norm_mod(xs, mod_s[0], mod_s[1], nw1, tm=rows_s)
    zqk_s, zr_s, gates_s = _inproj(hs, w_in_bf, rope_s, tm=rows_s)
    pad8 = lambda z: jnp.pad(z.reshape(bs, ts, z.shape[1]), ((0, 0), (0, DEC_PAD - ts), (0, 0)))
    zqk_s8, zr_s8 = pad8(zqk_s), pad8(zr_s)
    cache_in = (cache_kv_w128, cache_kv_w512, cache_kv_w2048)
    caches = [c.reshape(bs, c.shape[2] * 2 * H_G, HEAD_DIM) for c in cache_in]
    att_s8, *new_caches = _attention_decode(zqk_s8, zr_s8, caches)
    att_s = att_s8[:, :ts].reshape(rows_s, ATT_OUT).astype(BF16)
    kv_s = [n.reshape(c.shape) for n, c in zip(new_caches, cache_in)]
    hg_s8, s_s = _hgrn_decode(zr_s8, hg_lb, hg_nw, state_hgrn[0])
    hg_s = hg_s8[:, :ts].reshape(rows_s, HG_WV).astype(BF16)
    x1s = _merge(att_s, hg_s, gates_s, xs, mod_s[2], wpa_bf, wpb_bf, wo_bf, tm=rows_s)
    buf = state_conv[0]
    zrow = jnp.zeros((bs, 1, D_FF), F32)
    p1 = jnp.concatenate([buf[:, 1:2], zrow, zrow, zrow], axis=1).reshape(rows_s, D_FF)
    p2 = jnp.concatenate([buf[:, 0:1], buf[:, 1:2], zrow, zrow], axis=1).reshape(rows_s, D_FF)
    ys, a_s = _ffn(x1s, mod_s[3], mod_s[4], mod_s[5], nw2, nwf, wa_bf, wb_bf, wd_bf, cw, cb,
                   p1, p2, tm=rows_s, seq_rows=ts)

    hgrn_s = s_s[None]
    conv_s = a_s.reshape(bs, ts, D_FF)[:, ts - (CONV_W - 1):][None]

    return (yp.reshape(bp, t, d), ys.reshape(bs, ts, d),
            kv_p[0], kv_p[1], kv_p[2], hgrn_p, conv_p,
            kv_s[0], kv_s[1], kv_s[2], hgrn_s, conv_s)
```

```python
import functools

import numpy as np
import jax
import jax.numpy as jnp
from jax import lax
from jax.experimental import pallas as pl
from jax.experimental.pallas import tpu as pltpu

F32 = jnp.float32
BF16 = jnp.bfloat16

D_MODEL = 2048
SEQ = 16384
DEC_BATCH = 32
DEC_SEQ = 4
PAST_LEN = 16384

HEAD_DIM = 128
ATT_GROUPS = ((128, 1), (512, 4), (2048, 16))
H_G = 4
N_ATT_HEADS = H_G * len(ATT_GROUPS)
ATT_W = N_ATT_HEADS * HEAD_DIM
ATT_OUT = H_G * HEAD_DIM
ROT_DIM = HEAD_DIM // 4
ROPE_THETA = 500000.0
ATT_SCALE = HEAD_DIM ** -0.5

HG_HEADS = 8
HG_DK = 128
HG_DV = 128
HG_WK = HG_HEADS * HG_DK
HG_WV = HG_HEADS * HG_DV

D_FF = 5632
CONV_W = 3
N_MOD = 6
EPS = 1e-6

IN_SIZES = (ATT_W, ATT_W, ATT_W, HG_WK, HG_WK, HG_WV, HG_WV, D_MODEL, D_MODEL)
IN_TOTAL = sum(IN_SIZES)
IN_OFFS = tuple(int(s) for s in np.cumsum((0,) + IN_SIZES)[:-1])
OFF_QA, OFF_KA, OFF_VA, OFF_QH, OFF_FH, OFF_IH, OFF_GH, OFF_GA, OFF_GB = IN_OFFS
QK_COLS = 2 * ATT_W
R_VA, R_QH, R_FH, R_IH, R_GH, R_GA, R_GB = (o - QK_COLS for o in IN_OFFS[2:])

V7X_VMEM_BYTES = 64 * 1024 * 1024
SUBLANES = 8
LANES = 128
DEC_PAD = SUBLANES

COL_BLK = 512
MXU_COLS = 256
ROPE_ROWS = 256
FFN_ROWS = 512
HG_CHUNK = 128
ATT_QB = 128
ATT_SB = 2048


def _vmem_limit(nbytes):
    return int(min(V7X_VMEM_BYTES - 8 * 1024 * 1024, max(nbytes, 16 * 1024 * 1024)))


def _silu(x):
    return x * jax.nn.sigmoid(x)


def _unroll_for(trips, max_unroll=6):
    return next(u for u in range(min(max_unroll, trips), 0, -1) if trips % u == 0)


def _ada_kernel(c_ref, w_ref, b_ref, o_ref):
    c = c_ref[...]
    s = _silu(c).astype(BF16)
    o_ref[...] = jnp.dot(s, w_ref[...].astype(BF16), preferred_element_type=F32) + b_ref[...]


def _ada(c_all, w_ada, b_ada):
    rows, d = c_all.shape
    n = w_ada.shape[1]
    tn = 1024
    return pl.pallas_call(
        _ada_kernel,
        out_shape=jax.ShapeDtypeStruct((rows, n), F32),
        grid=(n // tn,),
        in_specs=[pl.BlockSpec((rows, d), lambda j: (0, 0)),
                  pl.BlockSpec((d, tn), lambda j: (0, j)),
                  pl.BlockSpec((1, tn), lambda j: (0, j))],
        out_specs=pl.BlockSpec((rows, tn), lambda j: (0, j)),
        compiler_params=pltpu.CompilerParams(
            dimension_semantics=("arbitrary",),
            vmem_limit_bytes=_vmem_limit(2 * d * tn * 4 + d * tn * 2 + 4 * rows * (d + tn) * 4)),
        name="ada",
    )(c_all, w_ada, b_ada.reshape(1, n))


def _norm_mod_rows(x_ref, nw_ref, sc_ref, sh_ref, h_ref, tm, per_row):
    step = min(tm, 128)

    def body(r, carry):
        rs = pl.ds(pl.multiple_of(r * step, step), step)
        x = x_ref[rs, :]
        ms = jnp.mean(x * x, axis=-1, keepdims=True)
        y = x * lax.rsqrt(ms + EPS) * nw_ref[...]
        if per_row:
            h = y * (1.0 + sc_ref[rs, :]) + sh_ref[rs, :]
        else:
            h = y * (1.0 + sc_ref[...]) + sh_ref[...]
        h_ref[rs, :] = h.astype(BF16)
        return carry

    lax.fori_loop(0, tm // step, body, 0)


def _mod_spec(per_row, tm, d):
    if per_row:
        return pl.BlockSpec((tm, d), lambda i, j: (i, 0))
    return pl.BlockSpec((1, d), lambda i, j: (0, 0))


def _norm_kernel(x_ref, sh_ref, sc_ref, nw_ref, h_ref, *, tm, per_row):
    _norm_mod_rows(x_ref, nw_ref, sc_ref, sh_ref, h_ref, tm, per_row)


def _norm_mod(x2d, sh, sc, nw, tm):
    rows, d = x2d.shape
    per_row = sh.shape[0] != 1
    mod = (pl.BlockSpec((tm, d), lambda i: (i, 0)) if per_row else pl.BlockSpec((1, d), lambda i: (0, 0)))
    return pl.pallas_call(
        functools.partial(_norm_kernel, tm=tm, per_row=per_row),
        out_shape=jax.ShapeDtypeStruct((rows, d), BF16),
        grid=(rows // tm,),
        in_specs=[pl.BlockSpec((tm, d), lambda i: (i, 0)), mod, mod,
                  pl.BlockSpec((1, d), lambda i: (0, 0))],
        out_specs=pl.BlockSpec((tm, d), lambda i: (i, 0)),
        compiler_params=pltpu.CompilerParams(
            dimension_semantics=("arbitrary",),
            vmem_limit_bytes=_vmem_limit(2 * tm * d * 6 + (4 * tm * d * 4 if per_row else 0) + 8 * 1024 * 1024)),
        name="norm_mod",
    )(x2d, sh, sc, nw)


def _bf16_weight(w_ref, wb_ref):
    if wb_ref is None:
        return w_ref
    w = w_ref[...].astype(BF16)
    wb_ref[...] = w
    return w


def _proj_rope_kernel(h_ref, w_ref, cos_ref, s1_ref, s2_ref, z_ref, wb_ref=None):
    w = _bf16_weight(w_ref, wb_ref)
    tm = h_ref.shape[0]
    rstep = min(tm, ROPE_ROWS)
    for r0 in range(0, tm, rstep):
        rs = slice(r0, r0 + rstep)
        h = h_ref[rs, :]
        c, s1, s2 = cos_ref[rs, :], s1_ref[rs, :], s2_ref[rs, :]
        for c0 in range(0, z_ref.shape[1], MXU_COLS):
            grp = jnp.dot(h, w[:, c0:c0 + MXU_COLS], preferred_element_type=F32)
            for hh in range(MXU_COLS // HEAD_DIM):
                blk = grp[:, hh * HEAD_DIM:(hh + 1) * HEAD_DIM]
                z_ref[rs, c0 + hh * HEAD_DIM:c0 + (hh + 1) * HEAD_DIM] = (
                    blk * c + pltpu.roll(blk, HEAD_DIM - ROT_DIM // 2, 1) * s1
                    + pltpu.roll(blk, ROT_DIM // 2, 1) * s2)


def _proj_plain_kernel(h_ref, w_ref, z_ref, wb_ref=None):
    w = _bf16_weight(w_ref, wb_ref)
    z_ref[...] = jnp.dot(h_ref[...], w[...], preferred_element_type=F32)


def _proj_gate_kernel(h_ref, w_ref, z_ref, wb_ref=None):
    w = _bf16_weight(w_ref, wb_ref)
    tm = h_ref.shape[0]
    rstep = min(tm, ROPE_ROWS)
    for r0 in range(0, tm, rstep):
        rs = slice(r0, r0 + rstep)
        h = h_ref[rs, :]
        for c0 in range(0, z_ref.shape[1], MXU_COLS):
            cs = slice(c0, c0 + MXU_COLS)
            z_ref[rs, cs] = jax.nn.sigmoid(jnp.dot(h, w[:, cs], preferred_element_type=F32)).astype(BF16)


def _inproj(h, w_in, rope, tm):
    rows, d = h.shape
    emit = not isinstance(w_in, tuple)
    assert not emit or rows == tm

    def call(kern, name, idx, col0, ncols, tn, out_dtype, tables=()):
        w = w_in if emit else w_in[idx]
        vm = (2 * tm * d * 2 + 2 * d * tn * jnp.dtype(w.dtype).itemsize + (3 * d * tn * 2 if emit else 0)
              + 2 * len(tables) * tm * HEAD_DIM * 4 + 2 * tm * tn * jnp.dtype(out_dtype).itemsize
              + 2 * tm * min(tn, COL_BLK) * 4 + 4 * 1024 * 1024)
        tab = pl.BlockSpec((tm, HEAD_DIM), lambda i, j: (i, 0))
        out_shape = [jax.ShapeDtypeStruct((rows, ncols), out_dtype)]
        out_specs = [pl.BlockSpec((tm, tn), lambda i, j: (i, j))]
        if emit:
            out_shape.append(jax.ShapeDtypeStruct((d, ncols), BF16))
            out_specs.append(pl.BlockSpec((d, tn), lambda i, j: (0, j)))
        outs = pl.pallas_call(
            kern,
            out_shape=out_shape,
            grid=(rows // tm, ncols // tn),
            in_specs=[pl.BlockSpec((tm, d), lambda i, j: (i, 0)),
                      pl.BlockSpec((d, tn), lambda i, j, c=(col0 // tn if emit else 0): (0, c + j))]
            + [tab] * len(tables),
            out_specs=out_specs,
            compiler_params=pltpu.CompilerParams(dimension_semantics=("arbitrary", "arbitrary"),
                                                 vmem_limit_bytes=_vmem_limit(vm)),
            name=name,
        )(h, w, *tables)
        return outs if emit else (outs[0], None)

    z_qk, w0 = call(_proj_rope_kernel, "proj_rope", 0, 0, QK_COLS, 2 * COL_BLK, F32, rope)
    z_rest, w1 = call(_proj_plain_kernel, "proj_plain", 1, QK_COLS, R_GA, COL_BLK, F32)
    gates, w2 = call(_proj_gate_kernel, "proj_gate", 2, QK_COLS + R_GA, 2 * D_MODEL, COL_BLK, BF16)
    return (z_qk, z_rest, gates), ((w0, w1, w2) if emit else None)


def _rope_tables(pos):
    half = ROT_DIM // 2
    inv_freq = ROPE_THETA ** (-np.arange(half, dtype=np.float64) * 2.0 / ROT_DIM)
    ang = np.asarray(pos, dtype=np.float64)[:, None] * inv_freq[None, :]
    cos, sin = jnp.asarray(np.cos(ang), dtype=F32), jnp.asarray(np.sin(ang), dtype=F32)
    rows = pos.shape[0]
    ones = jnp.ones((rows, HEAD_DIM - ROT_DIM), F32)
    zer = jnp.zeros((rows, HEAD_DIM - ROT_DIM), F32)
    zh = jnp.zeros((rows, half), F32)
    c = jnp.concatenate([cos, cos, ones], axis=1)
    s1 = jnp.concatenate([-sin, zh, zer], axis=1)
    s2 = jnp.concatenate([zh, sin, zer], axis=1)
    return c, s1, s2


def _attn_tile(q, ka, kb, va, vb, valid):
    k = jnp.concatenate([ka, kb], axis=0).astype(BF16)
    v = jnp.concatenate([va, vb], axis=0).astype(BF16)
    s = lax.dot_general((q * ATT_SCALE).astype(BF16), k, (((1,), (1,)), ((), ())),
                        preferred_element_type=F32)
    s = jnp.where(valid, s, -jnp.inf)
    m = jnp.max(s, axis=1, keepdims=True)
    p = jnp.exp(s - m)
    l = jnp.sum(p, axis=1, keepdims=True)
    o = jnp.dot(p.astype(BF16), v, preferred_element_type=F32) / l
    return o, m + jnp.log(l)


def _attn_kernel(*refs):
    q_refs = refs[0:3]
    kc_refs = refs[3:6]
    kp_refs = refs[6:9]
    vc_refs = refs[9:12]
    vp_refs = refs[12:15]
    o_ref = refs[15]
    og_scr, lse_scr = refs[16], refs[17]
    n = pl.program_id(0)

    row = lax.broadcasted_iota(jnp.int32, (ATT_QB, 2 * ATT_QB), 0)
    col = lax.broadcasted_iota(jnp.int32, (ATT_QB, 2 * ATT_QB), 1)
    band = (col >= row) & (col <= row + ATT_QB)
    band_first = band & (col >= jnp.where(n > 0, 0, ATT_QB))

    for g, (_, dil) in enumerate(ATT_GROUPS):
        q_ref, kc_ref, kp_ref, vc_ref, vp_ref = q_refs[g], kc_refs[g], kp_refs[g], vc_refs[g], vp_refs[g]
        nq = ATT_SB // (dil * ATT_QB)

        def rows(start, count):
            if dil == 1:
                return pl.ds(pl.multiple_of(start, ATT_QB), count)
            return pl.ds(start, count, stride=dil)

        def put(g_, tok0, o, lse):
            og_scr[g_, rows(tok0, ATT_QB), :] = o
            lse_scr[g_, rows(tok0, ATT_QB), :] = jnp.broadcast_to(lse, (ATT_QB, HEAD_DIM))

        def first_body(r, carry):
            o, lse = _attn_tile(q_ref[rows(r, ATT_QB), :],
                                kp_ref[rows(r, ATT_QB), :], kc_ref[rows(r, ATT_QB), :],
                                vp_ref[rows(r, ATT_QB), :], vc_ref[rows(r, ATT_QB), :],
                                band_first)
            put(g, r, o, lse)
            return carry

        lax.fori_loop(0, dil, first_body, 0, unroll=_unroll_for(dil))

        if nq > 1:
            def rest_body(t, carry):
                r = t // (nq - 1)
                u = t % (nq - 1) + 1
                q0 = r + dil * (u * ATT_QB)
                k0 = r + dil * ((u - 1) * ATT_QB)
                k1 = r + dil * (u * ATT_QB)
                o, lse = _attn_tile(q_ref[rows(q0, ATT_QB), :],
                                    kc_ref[rows(k0, ATT_QB), :], kc_ref[rows(k1, ATT_QB), :],
                                    vc_ref[rows(k0, ATT_QB), :], vc_ref[rows(k1, ATT_QB), :],
                                    band)
                put(g, q0, o, lse)
                return carry

            lax.fori_loop(0, dil * (nq - 1), rest_body, 0, unroll=_unroll_for(dil * (nq - 1)))

    def merge_body(c, carry):
        rs = pl.ds(pl.multiple_of(c * 256, 256), 256)
        l0, l1, l2 = lse_scr[0, rs, :], lse_scr[1, rs, :], lse_scr[2, rs, :]
        m = jnp.maximum(jnp.maximum(l0, l1), l2)
        e0, e1, e2 = jnp.exp(l0 - m), jnp.exp(l1 - m), jnp.exp(l2 - m)
        num = e0 * og_scr[0, rs, :] + e1 * og_scr[1, rs, :] + e2 * og_scr[2, rs, :]
        o_ref[rs, :] = (num / (e0 + e1 + e2)).astype(o_ref.dtype)
        return carry

    lax.fori_loop(0, ATT_SB // 256, merge_body, 0)


def _attention_prompt(z_qk, z_rest):
    t = z_qk.shape[0]
    nsb = t // ATT_SB
    in_specs, args = [], []

    def cur(colblk):
        return pl.BlockSpec((ATT_SB, HEAD_DIM), lambda n, h, c=colblk: (n, c + h))

    def prev(colblk, dil):
        rows_p = dil * ATT_QB
        per = ATT_SB // rows_p
        return pl.BlockSpec((rows_p, HEAD_DIM),
                            lambda n, h, c=colblk, per=per: (jnp.maximum(n * per - 1, 0), c + h))

    for g in range(3):
        in_specs.append(cur(OFF_QA // HEAD_DIM + g * H_G)); args.append(z_qk)
    for z, base in ((z_qk, OFF_KA), (z_rest, R_VA)):
        for g in range(3):
            in_specs.append(cur(base // HEAD_DIM + g * H_G)); args.append(z)
        for g, (_, dil) in enumerate(ATT_GROUPS):
            in_specs.append(prev(base // HEAD_DIM + g * H_G, dil)); args.append(z)
    blk = ATT_SB * HEAD_DIM * 4
    prev_rows = sum(d * ATT_QB for _, d in ATT_GROUPS)
    vm = 2 * (9 * blk + 2 * prev_rows * HEAD_DIM * 4) + 2 * blk + 6 * blk + 8 * 1024 * 1024
    return pl.pallas_call(
        _attn_kernel,
        out_shape=jax.ShapeDtypeStruct((t, ATT_OUT), BF16),
        grid=(nsb, H_G),
        in_specs=in_specs,
        out_specs=pl.BlockSpec((ATT_SB, HEAD_DIM), lambda n, h: (n, h)),
        scratch_shapes=[pltpu.VMEM((3, ATT_SB, HEAD_DIM), F32),
                        pltpu.VMEM((3, ATT_SB, HEAD_DIM), F32)],
        compiler_params=pltpu.CompilerParams(
            dimension_semantics=("arbitrary", "arbitrary"),
            vmem_limit_bytes=_vmem_limit(vm)),
        name="attn_prompt",
    )(*args)


def _attn_dec_kernel(q_ref, kn_ref, vn_ref, c0_ref, c1_ref, c2_ref, o_ref, n0_ref, n1_ref, n2_ref):
    caches = (c0_ref, c1_ref, c2_ref)
    news = (n0_ref, n1_ref, n2_ref)
    rows_per_pos = 2 * H_G
    zpad = jnp.zeros((LANES - DEC_PAD, HEAD_DIM), F32)

    for g in range(len(ATT_GROUPS)):
        c_ref, n_ref = caches[g], news[g]
        n_rows = c_ref.shape[1]
        keep = n_rows - DEC_SEQ * rows_per_pos
        n_ref[0, 0:keep, :] = c_ref[0, DEC_SEQ * rows_per_pos:n_rows, :]
        for t in range(DEC_SEQ):
            for h in range(H_G):
                hs = slice((g * H_G + h) * HEAD_DIM, (g * H_G + h + 1) * HEAD_DIM)
                r = keep + t * rows_per_pos + h
                n_ref[0, r:r + 1, :] = kn_ref[0, t:t + 1, hs]
                n_ref[0, r + H_G:r + H_G + 1, :] = vn_ref[0, t:t + 1, hs]

    for h in range(H_G):
        outs, lses = [], []
        for g, (win, dil) in enumerate(ATT_GROUPS):
            c_ref = caches[g]
            p_len = c_ref.shape[1] // rows_per_pos
            hs = slice((g * H_G + h) * HEAD_DIM, (g * H_G + h + 1) * HEAD_DIM)
            q = (q_ref[0, :, hs] * ATT_SCALE).astype(BF16)
            k = jnp.concatenate([c_ref[0, pl.ds(h, p_len, stride=rows_per_pos), :], kn_ref[0, :, hs], zpad],
                                axis=0).astype(BF16)
            v = jnp.concatenate([c_ref[0, pl.ds(H_G + h, p_len, stride=rows_per_pos), :],
                                 vn_ref[0, :, hs], zpad], axis=0).astype(BF16)
            s = lax.dot_general(q, k, (((1,), (1,)), ((), ())), preferred_element_type=F32)
            tq = lax.broadcasted_iota(jnp.int32, s.shape, 0) & (DEC_SEQ - 1)
            col = lax.broadcasted_iota(jnp.int32, s.shape, 1)
            delta = p_len + tq - col
            valid = ((delta >= 0) & ((delta & (dil - 1)) == 0) & (delta <= win)
                     & (col < p_len + DEC_SEQ))
            s = jnp.where(valid, s, -jnp.inf)
            m = jnp.max(s, axis=1, keepdims=True)
            p = jnp.exp(s - m)
            l = jnp.sum(p, axis=1, keepdims=True)
            outs.append(jnp.dot(p.astype(BF16), v, preferred_element_type=F32) / l)
            lses.append(m + jnp.log(l))
        m = jnp.maximum(jnp.maximum(lses[0], lses[1]), lses[2])
        e = [jnp.exp(x - m) for x in lses]
        num = e[0] * outs[0] + e[1] * outs[1] + e[2] * outs[2]
        o_ref[0, :, h * HEAD_DIM:(h + 1) * HEAD_DIM] = num / (e[0] + e[1] + e[2])


def _attention_decode(zqk8, zrest8, caches):
    b = zqk8.shape[0]
    qspec = lambda c: pl.BlockSpec((1, DEC_PAD, ATT_W), lambda i, c=c: (i, 0, c))
    cspecs = [pl.BlockSpec((1, c.shape[1], c.shape[2]), lambda i: (i, 0, 0)) for c in caches]
    vm = 4 * sum(c.shape[1] * c.shape[2] * 4 for c in caches) + 12 * 1024 * 1024
    return pl.pallas_call(
        _attn_dec_kernel,
        out_shape=[jax.ShapeDtypeStruct((b, DEC_PAD, ATT_OUT), F32)]
        + [jax.ShapeDtypeStruct(c.shape, F32) for c in caches],
        grid=(b,),
        in_specs=[qspec(OFF_QA // ATT_W), qspec(OFF_KA // ATT_W), qspec(R_VA // ATT_W)] + cspecs,
        out_specs=[pl.BlockSpec((1, DEC_PAD, ATT_OUT), lambda i: (i, 0, 0))] + cspecs,
        compiler_params=pltpu.CompilerParams(
            dimension_semantics=("arbitrary",),
            vmem_limit_bytes=_vmem_limit(vm)),
        name="attn_decode",
    )(zqk8, zqk8, zrest8, *caches)


def _hgrn_consts(c):
    t = np.arange(c)[:, None]
    u = np.arange(c)[None, :]
    seg = [(u <= t)]
    cross = []
    m = c // 2
    while m >= 1:
        blk_t, off_t = t // (2 * m), t % (2 * m)
        piv = blk_t * 2 * m + m - 1
        upper = off_t >= m
        if 1 < m < SUBLANES:
            seg.append(np.where(upper, (u > piv) & (u <= t), (u > t) & (u <= piv)))
        cross.append((t // (2 * m) == u // (2 * m)) & (t % (2 * m) >= m) & (u % (2 * m) < m))
        m //= 2
    cross.append(t == u)
    return (np.stack(seg).astype(np.float32).reshape(-1, c),
            np.stack(cross).astype(np.float32))


def _hgrn_level_weights(b, dsum, fg, c):
    out = []
    m = c // 2
    low = 1
    while m >= 1:
        if m >= SUBLANES:
            parts = []
            for k0 in range(0, c, 2 * m):
                piv = b[k0 + m - 1:k0 + m, :]
                parts += [piv - b[k0:k0 + m, :], b[k0 + m:k0 + 2 * m, :] - piv]
            out.append(jnp.exp2(jnp.concatenate(parts, axis=0)))
        elif m > 1:
            out.append(jnp.exp2(dsum[low * c:(low + 1) * c]))
            low += 1
        else:
            odd = (lax.broadcasted_iota(jnp.int32, fg.shape, 0) & 1) == 1
            out.append(jnp.where(odd, fg, 1.0))
        m //= 2
    return out


def _hgrn_kernel(q_ref, f_ref, i_ref, g_ref, lb_ref, nw_ref, seg_ref, cross_ref, o_ref, s_out_ref, s_scr,
                 *, tb, n_lvl):
    n = pl.program_id(1)
    c = HG_CHUNK

    @pl.when(n == 0)
    def _():
        s_scr[...] = jnp.zeros_like(s_scr)

    lb = lb_ref[...]
    mx = jnp.maximum(lb[0:1, :], lb[1:2, :])
    e0, e1 = jnp.exp(lb[0:1, :] - mx), jnp.exp(lb[1:2, :] - mx)
    lower = e0 / (e0 + e1)
    nw = nw_ref[...]
    seg = seg_ref[...]

    chunks = range(tb // c)
    nt = (((1,), (1,)), ((), ()))
    qs, ks, vs, ws, w_pre, w_suf = [], [], [], [], [], []
    for ch in chunks:
        rs = slice(ch * c, (ch + 1) * c)
        fg = lower + (1.0 - lower) * jax.nn.sigmoid(f_ref[rs, :])
        lg = jnp.log2(fg)
        lg_hi = lg.astype(BF16)
        lg_lo = (lg - lg_hi.astype(F32)).astype(BF16)
        dsum = (jnp.dot(seg, lg_hi, preferred_element_type=F32)
                + jnp.dot(seg, lg_lo, preferred_element_type=F32))
        b = dsum[0:c]
        ws.append(_hgrn_level_weights(b, dsum, fg, c))
        w_pre.append(jnp.exp2(b))
        w_suf.append(jnp.exp2(b[c - 1:c, :] - b))
        qs.append(_silu(q_ref[rs, :]))
        ks.append(1.0 - fg)
        vs.append(i_ref[rs, :].astype(BF16))
    qb = [qs[ch].astype(BF16) for ch in chunks]
    kb = [ks[ch].astype(BF16) for ch in chunks]
    atts = [cross_ref[n_lvl] * lax.dot_general(qb[ch], kb[ch], nt, preferred_element_type=F32)
            for ch in chunks]
    for lv in range(n_lvl):
        for ch in chunks:
            wl = ws[ch][lv].astype(BF16)
            atts[ch] = atts[ch] + cross_ref[lv] * lax.dot_general(
                qb[ch] * wl, kb[ch] * wl, nt, preferred_element_type=F32)
    o_intra = [jnp.dot(atts[ch].astype(BF16), vs[ch], preferred_element_type=F32) for ch in chunks]
    q_dec = [(qs[ch] * w_pre[ch]).astype(BF16) for ch in chunks]
    k_end_t = [(ks[ch] * w_suf[ch]).T.astype(BF16) for ch in chunks]
    decay_col = [jnp.broadcast_to(w_pre[ch][c - 1:c, :], (HG_DK, HG_DV)).T for ch in chunks]
    kv = [jnp.dot(k_end_t[ch], vs[ch], preferred_element_type=F32) for ch in chunks]
    s_cur = s_scr[...]
    for ch in chunks:
        rs = slice(ch * c, (ch + 1) * c)
        o = o_intra[ch] + jnp.dot(q_dec[ch], s_cur.astype(BF16), preferred_element_type=F32)
        s_cur = decay_col[ch] * s_cur + kv[ch]
        ms = jnp.mean(o * o, axis=-1, keepdims=True)
        o_ref[rs, :] = (o * lax.rsqrt(ms + EPS) * nw * _silu(g_ref[rs, :])).astype(o_ref.dtype)
    s_scr[...] = s_cur

    @pl.when(n == pl.num_programs(1) - 1)
    def _():
        s_out_ref[0] = s_scr[...]


def _hgrn_prompt(z, hg_lb, hg_norm_w, tb):
    t = z.shape[0]
    seg_np, cross_np = _hgrn_consts(HG_CHUNK)
    n_lvl = cross_np.shape[0] - 1
    seg = jnp.asarray(seg_np, dtype=BF16)
    cross = jnp.asarray(cross_np, dtype=F32)
    col = lambda off: pl.BlockSpec((tb, HG_DK), lambda h, n, o=off // HG_DK: (n, o + h))
    kern = functools.partial(_hgrn_kernel, tb=tb, n_lvl=n_lvl)
    vm = 2 * 5 * tb * HG_DK * 4 + 2 * seg_np.size * 2 + 2 * cross_np.size * 4 + 16 * 1024 * 1024
    return pl.pallas_call(
        kern,
        out_shape=(jax.ShapeDtypeStruct((t, HG_WV), BF16),
                   jax.ShapeDtypeStruct((HG_HEADS, HG_DK, HG_DV), F32)),
        grid=(HG_HEADS, t // tb),
        in_specs=[col(R_QH), col(R_FH), col(R_IH), col(R_GH),
                  pl.BlockSpec((2, HG_DK), lambda h, n: (0, h)),
                  pl.BlockSpec((1, HG_DV), lambda h, n: (0, 0)),
                  pl.BlockSpec(seg_np.shape, lambda h, n: (0, 0)),
                  pl.BlockSpec(cross_np.shape, lambda h, n: (0, 0, 0))],
        out_specs=(pl.BlockSpec((tb, HG_DV), lambda h, n: (n, h)),
                   pl.BlockSpec((1, HG_DK, HG_DV), lambda h, n: (h, 0, 0))),
        scratch_shapes=[pltpu.VMEM((HG_DK, HG_DV), F32)],
        compiler_params=pltpu.CompilerParams(
            dimension_semantics=("arbitrary", "arbitrary"),
            vmem_limit_bytes=_vmem_limit(vm)),
        name="hgrn_prompt",
    )(z, z, z, z, hg_lb, hg_norm_w, seg, cross)


def _hgrn_dec_kernel(q_ref, f_ref, i_ref, g_ref, lb_ref, nw_ref, s_ref, o_ref, s_out_ref, *, heads):
    lb = lb_ref[...]
    mx = jnp.maximum(lb[0:1, :], lb[1:2, :])
    e0, e1 = jnp.exp(lb[0:1, :] - mx), jnp.exp(lb[1:2, :] - mx)
    lower = e0 / (e0 + e1)
    nw = nw_ref[...]
    zpad = jnp.zeros((HG_DK - DEC_PAD, HG_DK), F32)
    row = lax.broadcasted_iota(jnp.int32, (DEC_PAD, HG_DV), 0)

    def before(x, d):
        return pltpu.roll(x, d, 0)

    def after(x, d):
        return pltpu.roll(x, DEC_PAD - d, 0)

    def split(x):
        hi = x.astype(BF16)
        return hi, (x - hi.astype(F32)).astype(BF16)

    def dot3(a, b):
        (ah, al), (bh, bl) = split(a), split(b)
        return (jnp.dot(ah, bh, preferred_element_type=F32) + jnp.dot(ah, bl, preferred_element_type=F32)
                + jnp.dot(al, bh, preferred_element_type=F32))

    for h in range(heads):
        hs = slice(h * HG_DK, (h + 1) * HG_DK)
        q = _silu(q_ref[0, :, hs])
        fg = lower[:, hs] + (1.0 - lower[:, hs]) * jax.nn.sigmoid(f_ref[0, :, hs])
        k = 1.0 - fg
        v = i_ref[0, :, hs]
        s0 = s_ref[0, h]
        p = [jnp.ones_like(fg), fg]
        for d in range(2, DEC_SEQ + 1):
            p.append(p[d - 1] * before(fg, d - 1))
        f_cum = p[DEC_SEQ]
        for t in range(DEC_SEQ - 1):
            f_cum = jnp.where(row == t, p[t + 1], f_cum)
        g_suf = jnp.ones_like(fg)
        run = jnp.ones_like(fg)
        for d in range(1, DEC_SEQ):
            run = run * after(fg, d)
            g_suf = jnp.where(row == DEC_SEQ - 1 - d, run, g_suf)
        o = dot3(q * f_cum, s0)
        for d in range(DEC_SEQ):
            coef = jnp.sum(q * p[d] * (before(k, d) if d else k), axis=1, keepdims=True)
            o = o + jnp.where(row >= d, coef, 0.0) * (before(v, d) if d else v)
        kd = jnp.where(row < DEC_SEQ, k * g_suf, 0.0)
        kd_t = jnp.concatenate([kd, zpad], axis=0).T
        v_pad = jnp.concatenate([v, zpad], axis=0)
        decay_col = jnp.broadcast_to(f_cum[DEC_SEQ - 1:DEC_SEQ, :], (HG_DK, HG_DV)).T
        s_out_ref[0, h] = decay_col * s0 + dot3(kd_t, v_pad)
        ms = jnp.mean(o * o, axis=-1, keepdims=True)
        o_ref[0, :, hs] = o * lax.rsqrt(ms + EPS) * nw * _silu(g_ref[0, :, hs])


def _hgrn_decode(z8, hg_lb, hg_norm_w, state):
    b = z8.shape[0]
    heads = COL_BLK // HG_DK
    nhb = HG_HEADS // heads
    col = lambda off: pl.BlockSpec((1, DEC_PAD, COL_BLK), lambda i, j, o=off // COL_BLK: (i, 0, o + j))
    kern = functools.partial(_hgrn_dec_kernel, heads=heads)
    sspec = pl.BlockSpec((1, heads, HG_DK, HG_DV), lambda i, j: (i, j, 0, 0))
    return pl.pallas_call(
        kern,
        out_shape=(jax.ShapeDtypeStruct((b, DEC_PAD, HG_WV), F32),
                   jax.ShapeDtypeStruct(state.shape, F32)),
        grid=(b, nhb),
        in_specs=[col(R_QH), col(R_FH), col(R_IH), col(R_GH),
                  pl.BlockSpec((2, COL_BLK), lambda i, j: (0, j)),
                  pl.BlockSpec((1, HG_DV), lambda i, j: (0, 0)),
                  sspec],
        out_specs=(pl.BlockSpec((1, DEC_PAD, COL_BLK), lambda i, j: (i, 0, j)), sspec),
        compiler_params=pltpu.CompilerParams(
            dimension_semantics=("arbitrary", "arbitrary"),
            vmem_limit_bytes=_vmem_limit(0)),
        name="hgrn_decode",
    )(z8, z8, z8, z8, hg_lb, hg_norm_w, state)


def _merge_kernel(att_ref, hg_ref, ga_ref, gb_ref, x_ref, g1_ref, wpa_ref, wpb_ref, wo_ref, o_ref):
    att, hg = att_ref[...], hg_ref[...]
    parts = []
    for c0 in range(0, wpa_ref.shape[1], MXU_COLS):
        cs = slice(c0, c0 + MXU_COLS)
        ya = jnp.dot(att, wpa_ref[:, cs], preferred_element_type=F32)
        yb = jnp.dot(hg, wpb_ref[:, cs], preferred_element_type=F32)
        parts.append((ga_ref[:, cs].astype(F32) * ya + gb_ref[:, cs].astype(F32) * yb).astype(BF16))
    ymix = jnp.concatenate(parts, axis=1)
    o_ref[...] = x_ref[...] + g1_ref[...] * jnp.dot(ymix, wo_ref[...], preferred_element_type=F32)


def _merge(att, hg, gates, x2d, g1, wpa_bf, wpb_bf, wo_bf, tm):
    rows, d = x2d.shape
    per_row = g1.shape[0] != 1
    resident = pl.Buffered(1)
    g1_spec = (pl.BlockSpec((tm, d), lambda i: (i, 0)) if per_row else pl.BlockSpec((1, d), lambda i: (0, 0)))
    vm = (2 * tm * (ATT_OUT + HG_WV + 2 * d) * 2 + 4 * tm * d * 4 + (ATT_OUT * d + HG_WV * d + d * d) * 2
          + (2 * tm * d * 4 if per_row else 0) + tm * d * 10 + 4 * 1024 * 1024)
    return pl.pallas_call(
        _merge_kernel,
        out_shape=jax.ShapeDtypeStruct((rows, d), F32),
        grid=(rows // tm,),
        in_specs=[pl.BlockSpec((tm, ATT_OUT), lambda i: (i, 0)),
                  pl.BlockSpec((tm, HG_WV), lambda i: (i, 0)),
                  pl.BlockSpec((tm, d), lambda i: (i, 0)),
                  pl.BlockSpec((tm, d), lambda i: (i, 1)),
                  pl.BlockSpec((tm, d), lambda i: (i, 0)),
                  g1_spec,
                  pl.BlockSpec((ATT_OUT, d), lambda i: (0, 0), pipeline_mode=resident),
                  pl.BlockSpec((HG_WV, d), lambda i: (0, 0), pipeline_mode=resident),
                  pl.BlockSpec((d, d), lambda i: (0, 0), pipeline_mode=resident)],
        out_specs=pl.BlockSpec((tm, d), lambda i: (i, 0)),
        compiler_params=pltpu.CompilerParams(
            dimension_semantics=("arbitrary",),
            vmem_limit_bytes=_vmem_limit(vm)),
        name="merge",
    )(att, hg, gates, gates, x2d, g1, wpa_bf, wpb_bf, wo_bf)


def _ffn_kernel(x_ref, sh_ref, sc_ref, g2_ref, nw_ref, nf_ref, wa_ref, wb_ref, wd_ref, cw_ref, cb_ref,
                p1_ref, p2_ref, y_ref, tail_ref, *rest, tm, ta, per_row, seq_rows, emit):
    wa_out, wb_out, wd_out = rest[:3] if emit else (None, None, None)
    h_scr, acc_scr, carry_scr = rest[-3:]
    wa_ref = _bf16_weight(wa_ref, wa_out)
    wb_ref = _bf16_weight(wb_ref, wb_out)
    wd_ref = _bf16_weight(wd_ref, wd_out)
    i = pl.program_id(0)
    f = pl.program_id(1)

    @pl.when(f == 0)
    def _():
        _norm_mod_rows(x_ref, nw_ref, sc_ref, sh_ref, h_scr, tm, per_row)
        acc_scr[...] = jnp.zeros_like(acc_scr)

    if seq_rows is None:
        @pl.when(i == 0)
        def _():
            carry_scr[f] = p1_ref[...]

    tf = wa_ref.shape[1]
    groups = [slice(c0, c0 + MXU_COLS) for c0 in range(0, tf, MXU_COLS)]
    rc = min(tm, FFN_ROWS)
    row = lax.broadcasted_iota(jnp.int32, (rc, MXU_COLS), 0)
    if seq_rows is None:
        tails = [carry_scr[f, :, cs] for cs in groups]
    for r0 in range(0, tm, rc):
        rs = slice(r0, r0 + rc)
        h = h_scr[rs, :]
        a_parts = [jnp.dot(h, wa_ref[:, cs], preferred_element_type=F32) for cs in groups]
        b_parts = [jnp.dot(h, wb_ref[:, cs], preferred_element_type=F32) for cs in groups]
        ys = []
        for gi, (cs, a, b) in enumerate(zip(groups, a_parts, b_parts)):
            r1 = pltpu.roll(a, 1, 0)
            r2 = pltpu.roll(a, 2, 0)
            if seq_rows is None:
                prev = tails[gi]
                a1 = jnp.where(row == 0, prev[SUBLANES - 1:SUBLANES, :], r1)
                a2 = jnp.where(row == 0, prev[SUBLANES - 2:SUBLANES - 1, :],
                               jnp.where(row == 1, prev[SUBLANES - 1:SUBLANES, :], r2))
                tails[gi] = a[rc - SUBLANES:, :]
            else:
                t = row & (seq_rows - 1)
                a1 = jnp.where(t == 0, p1_ref[rs, cs], r1)
                a2 = jnp.where(t <= 1, p2_ref[rs, cs], r2)
                tail_ref[rs, cs] = a
            cw = cw_ref[:, cs]
            u = cb_ref[:, cs] + a2 * cw[0:1, :] + a1 * cw[1:2, :] + a * cw[2:3, :]
            ys.append((_silu(u) * b).astype(BF16))
        y = jnp.concatenate(ys, axis=1)
        acc_scr[rs, :] += jnp.dot(y, wd_ref[...], preferred_element_type=F32)
    if seq_rows is None:
        for cs, tail in zip(groups, tails):
            carry_scr[f, :, cs] = tail
            tail_ref[:, cs] = tail

    @pl.when(f == pl.num_programs(1) - 1)
    def _():
        step = min(tm, 128)

        def body(r, carry):
            rs = pl.ds(pl.multiple_of(r * step, step), step)
            g2 = g2_ref[rs, :] if per_row else g2_ref[...]
            x2 = x_ref[rs, :] + g2 * acc_scr[rs, :]
            ms = jnp.mean(x2 * x2, axis=-1, keepdims=True)
            y_ref[rs, :] = x2 * lax.rsqrt(ms + EPS) * nf_ref[...]
            return carry

        lax.fori_loop(0, tm // step, body, 0)


def _ffn(x1, sh, sc, g2, nw, nf, wa, wb, wd, conv_w, conv_b, p1, p2, tm, seq_rows):
    rows, d = x1.shape
    dff = wa.shape[1]
    per_row = sh.shape[0] != 1
    emit = wa.dtype == F32
    assert not emit or rows == tm
    tf = MXU_COLS if emit else COL_BLK
    nf_blk = dff // tf
    ta = SUBLANES if seq_rows is None else tm
    prow = SUBLANES if seq_rows is None else tm
    kern = functools.partial(_ffn_kernel, tm=tm, ta=ta, per_row=per_row, seq_rows=seq_rows, emit=emit)
    pspec = pl.BlockSpec((prow, tf), lambda i, f: (0 if seq_rows is None else i, f))
    w_specs = [pl.BlockSpec((d, tf), lambda i, f: (0, f)),
               pl.BlockSpec((d, tf), lambda i, f: (0, f)),
               pl.BlockSpec((tf, d), lambda i, f: (f, 0))]
    vm = (4 * tm * d * 4 + tm * d * 2 + tm * d * 4 + 3 * d * tf * (14 if emit else 4) + 10 * tm * tf * 4
          + (6 * tm * d * 4 if per_row else 0) + 8 * 1024 * 1024)
    return pl.pallas_call(
        kern,
        out_shape=(jax.ShapeDtypeStruct((rows, d), F32),
                   jax.ShapeDtypeStruct((ta * (rows // tm), dff), F32))
        + ((jax.ShapeDtypeStruct(wa.shape, BF16), jax.ShapeDtypeStruct(wb.shape, BF16),
            jax.ShapeDtypeStruct(wd.shape, BF16)) if emit else ()),
        grid=(rows // tm, nf_blk),
        in_specs=[pl.BlockSpec((tm, d), lambda i, f: (i, 0)),
                  _mod_spec(per_row, tm, d), _mod_spec(per_row, tm, d), _mod_spec(per_row, tm, d),
                  pl.BlockSpec((1, d), lambda i, f: (0, 0)),
                  pl.BlockSpec((1, d), lambda i, f: (0, 0))]
        + w_specs
        + [pl.BlockSpec((CONV_W, tf), lambda i, f: (0, f)),
           pl.BlockSpec((1, tf), lambda i, f: (0, f)),
           pspec, pspec],
        out_specs=(pl.BlockSpec((tm, d), lambda i, f: (i, 0)),
                   pl.BlockSpec((ta, tf), lambda i, f: (i, f))) + (tuple(w_specs) if emit else ()),
        scratch_shapes=[pltpu.VMEM((tm, d), BF16), pltpu.VMEM((tm, d), F32),
                        pltpu.VMEM((nf_blk, SUBLANES, tf), F32)],
        compiler_params=pltpu.CompilerParams(
            dimension_semantics=("arbitrary", "arbitrary"),
            vmem_limit_bytes=_vmem_limit(vm)),
        name="ffn",
    )(x1, sh, sc, g2, nw, nf, wa, wb, wd, conv_w, conv_b, p1, p2)


def kernel(x_prompt, x_sample, cache_kv_w128, cache_kv_w512, cache_kv_w2048, state_hgrn, state_conv,
           c_prompt, c_sample, w_ada, b_ada, norm1_w, w_in, hg_lb, hg_norm_w, w_pa, w_pb, w_o,
           norm2_w, w_ffn_a, w_ffn_b, conv_w, conv_b, w_ffn_down, norm_f_w):
    d = D_MODEL
    bp, t, _ = x_prompt.shape
    bs, ts, _ = x_sample.shape
    assert bp == 1 and ts == DEC_SEQ and w_ada.shape[0] == 1
    rows_s = bs * ts

    wpa_bf, wpb_bf, wo_bf = w_pa[0].astype(BF16), w_pb[0].astype(BF16), w_o[0].astype(BF16)
    nw1, nw2, nwf = norm1_w[0].reshape(1, d), norm2_w[0].reshape(1, d), norm_f_w.reshape(1, d)
    hg_nw = hg_norm_w[0].reshape(1, HG_DV)
    cw, cb = conv_w[0], conv_b[0].reshape(1, D_FF)

    n_seq = bp + bs
    pad_rows = -n_seq % SUBLANES
    c_all = jnp.concatenate([c_prompt, c_sample, jnp.zeros((pad_rows, d), F32)], axis=0)
    mod = _ada(c_all, w_ada[0], b_ada[0])
    mod_p = [mod[0:bp, k * d:(k + 1) * d] for k in range(N_MOD)]
    mod_s = [jnp.repeat(mod[bp:n_seq, k * d:(k + 1) * d], ts, axis=0) for k in range(N_MOD)]

    xs = x_sample.reshape(rows_s, d)
    rope_s = _rope_tables(PAST_LEN + np.arange(rows_s) % ts)
    hs = _norm_mod(xs, mod_s[0], mod_s[1], nw1, tm=rows_s)
    (zqk_s, zr_s, gates_s), w_in_bf = _inproj(hs, w_in[0], rope_s, tm=rows_s)
    pad8 = lambda z: jnp.pad(z.reshape(bs, ts, z.shape[1]), ((0, 0), (0, DEC_PAD - ts), (0, 0)))
    zqk_s8, zr_s8 = pad8(zqk_s), pad8(zr_s)
    cache_in = (cache_kv_w128, cache_kv_w512, cache_kv_w2048)
    caches = [c.reshape(bs, c.shape[2] * 2 * H_G, HEAD_DIM) for c in cache_in]
    att_s8, *new_caches = _attention_decode(zqk_s8, zr_s8, caches)
    att_s = att_s8[:, :ts].reshape(rows_s, ATT_OUT).astype(BF16)
    kv_s = [n.reshape(c.shape) for n, c in zip(new_caches, cache_in)]
    hg_s8, s_s = _hgrn_decode(zr_s8, hg_lb, hg_nw, state_hgrn[0])
    hg_s = hg_s8[:, :ts].reshape(rows_s, HG_WV).astype(BF16)
    x1s = _merge(att_s, hg_s, gates_s, xs, mod_s[2], wpa_bf, wpb_bf, wo_bf, tm=rows_s)
    buf = state_conv[0]
    zrow = jnp.zeros((bs, 1, D_FF), F32)
    p1 = jnp.concatenate([buf[:, 1:2], zrow, zrow, zrow], axis=1).reshape(rows_s, D_FF)
    p2 = jnp.concatenate([buf[:, 0:1], buf[:, 1:2], zrow, zrow], axis=1).reshape(rows_s, D_FF)
    ys, a_s, wa_bf, wb_bf, wd_bf = _ffn(x1s, mod_s[3], mod_s[4], mod_s[5], nw2, nwf,
                                        w_ffn_a[0], w_ffn_b[0], w_ffn_down[0], cw, cb,
                                        p1, p2, tm=rows_s, seq_rows=ts)

    xp = x_prompt.reshape(t, d)
    rope_p = _rope_tables(np.arange(t))
    hp = _norm_mod(xp, mod_p[0], mod_p[1], nw1, tm=1024)
    (zqk_p, zr_p, gates_p), _ = _inproj(hp, w_in_bf, rope_p, tm=2048)
    att_p = _attention_prompt(zqk_p, zr_p)
    hg_p, s_p = _hgrn_prompt(zr_p, hg_lb, hg_nw, tb=1024)
    x1p = _merge(att_p, hg_p, gates_p, xp, mod_p[2], wpa_bf, wpb_bf, wo_bf, tm=512)
    conv0 = jnp.zeros((SUBLANES, D_FF), F32)
    yp, tail_p = _ffn(x1p, mod_p[3], mod_p[4], mod_p[5], nw2, nwf, wa_bf, wb_bf, wd_bf, cw, cb,
                      conv0, conv0, tm=512, seq_rows=None)

    kv_p = []
    for g, (win, _) in enumerate(ATT_GROUPS):
        keep = min(win, t)
        ks = zqk_p[t - keep:, OFF_KA + g * ATT_OUT:OFF_KA + (g + 1) * ATT_OUT]
        vs = zr_p[t - keep:, R_VA + g * ATT_OUT:R_VA + (g + 1) * ATT_OUT]
        kv_p.append(jnp.stack([ks.reshape(keep, H_G, HEAD_DIM), vs.reshape(keep, H_G, HEAD_DIM)],
                              axis=1)[None, None])
    hgrn_p = s_p[None, None]
    conv_p = tail_p[tail_p.shape[0] - (CONV_W - 1):][None, None]

    hgrn_s = s_s[None]
    conv_s = a_s.reshape(bs, ts, D_FF)[:, ts - (CONV_W - 1):][None]

    return (yp.reshape(bp, t, d), ys.reshape(bs, ts, d),
            kv_p[0], kv_p[1], kv_p[2], hgrn_p, conv_p,
            kv_s[0], kv_s[1], kv_s[2], hgrn_s, conv_s)
```

```python
import functools

import numpy as np
import jax
import jax.numpy as jnp
from jax import lax
from jax.experimental import pallas as pl
from jax.experimental.pallas import tpu as pltpu

F32 = jnp.float32
BF16 = jnp.bfloat16

D_MODEL = 2048
SEQ = 16384
DEC_BATCH = 32
DEC_SEQ = 4
PAST_LEN = 16384

HEAD_DIM = 128
ATT_GROUPS = ((128, 1), (512, 4), (2048, 16))
H_G = 4
N_ATT_HEADS = H_G * len(ATT_GROUPS)
ATT_W = N_ATT_HEADS * HEAD_DIM
ATT_OUT = H_G * HEAD_DIM
ROT_DIM = HEAD_DIM // 4
ROPE_THETA = 500000.0
ATT_SCALE = HEAD_DIM ** -0.5

HG_HEADS = 8
HG_DK = 128
HG_DV = 128
HG_WK = HG_HEADS * HG_DK
HG_WV = HG_HEADS * HG_DV

D_FF = 5632
CONV_W = 3
N_MOD = 6
EPS = 1e-6

IN_SIZES = (ATT_W, ATT_W, ATT_W, HG_WK, HG_WK, HG_WV, HG_WV, D_MODEL, D_MODEL)
IN_TOTAL = sum(IN_SIZES)
IN_OFFS = tuple(int(s) for s in np.cumsum((0,) + IN_SIZES)[:-1])
OFF_QA, OFF_KA, OFF_VA, OFF_QH, OFF_FH, OFF_IH, OFF_GH, OFF_GA, OFF_GB = IN_OFFS
QK_COLS = 2 * ATT_W
R_VA, R_QH, R_FH, R_IH, R_GH, R_GA, R_GB = (o - QK_COLS for o in IN_OFFS[2:])

V7X_VMEM_BYTES = 64 * 1024 * 1024
SUBLANES = 8
LANES = 128
DEC_PAD = SUBLANES

COL_BLK = 512
MXU_COLS = 256
ROPE_ROWS = 256
FFN_ROWS = 512
HG_CHUNK = 128
ATT_QB = 128
ATT_SB = 2048


def _vmem_limit(nbytes):
    return int(min(V7X_VMEM_BYTES - 8 * 1024 * 1024, max(nbytes, 16 * 1024 * 1024)))


def _silu(x):
    return x * jax.nn.sigmoid(x)


def _unroll_for(trips, max_unroll=6):
    return next(u for u in range(min(max_unroll, trips), 0, -1) if trips % u == 0)


def _ada_kernel(c_ref, w_ref, b_ref, o_ref):
    c = c_ref[...]
    s = _silu(c).astype(BF16)
    o_ref[...] = jnp.dot(s, w_ref[...].astype(BF16), preferred_element_type=F32) + b_ref[...]


def _ada(c_all, w_ada, b_ada):
    rows, d = c_all.shape
    n = w_ada.shape[1]
    tn = 1024
    return pl.pallas_call(
        _ada_kernel,
        out_shape=jax.ShapeDtypeStruct((rows, n), F32),
        grid=(n // tn,),
        in_specs=[pl.BlockSpec((rows, d), lambda j: (0, 0)),
                  pl.BlockSpec((d, tn), lambda j: (0, j)),
                  pl.BlockSpec((1, tn), lambda j: (0, j))],
        out_specs=pl.BlockSpec((rows, tn), lambda j: (0, j)),
        compiler_params=pltpu.CompilerParams(
            dimension_semantics=("arbitrary",),
            vmem_limit_bytes=_vmem_limit(2 * d * tn * 4 + d * tn * 2 + 4 * rows * (d + tn) * 4)),
        name="ada",
    )(c_all, w_ada, b_ada.reshape(1, n))


def _norm_mod_rows(x_ref, nw_ref, sc_ref, sh_ref, h_ref, tm, per_row):
    step = min(tm, 128)

    def body(r, carry):
        rs = pl.ds(pl.multiple_of(r * step, step), step)
        x = x_ref[rs, :]
        ms = jnp.mean(x * x, axis=-1, keepdims=True)
        y = x * lax.rsqrt(ms + EPS) * nw_ref[...]
        if per_row:
            h = y * (1.0 + sc_ref[rs, :]) + sh_ref[rs, :]
        else:
            h = y * (1.0 + sc_ref[...]) + sh_ref[...]
        h_ref[rs, :] = h.astype(BF16)
        return carry

    lax.fori_loop(0, tm // step, body, 0)


def _mod_spec(per_row, tm, d):
    if per_row:
        return pl.BlockSpec((tm, d), lambda i, j: (i, 0))
    return pl.BlockSpec((1, d), lambda i, j: (0, 0))


def _norm_kernel(x_ref, sh_ref, sc_ref, nw_ref, h_ref, *, tm, per_row):
    _norm_mod_rows(x_ref, nw_ref, sc_ref, sh_ref, h_ref, tm, per_row)


def _norm_mod(x2d, sh, sc, nw, tm):
    rows, d = x2d.shape
    per_row = sh.shape[0] != 1
    mod = (pl.BlockSpec((tm, d), lambda i: (i, 0)) if per_row else pl.BlockSpec((1, d), lambda i: (0, 0)))
    return pl.pallas_call(
        functools.partial(_norm_kernel, tm=tm, per_row=per_row),
        out_shape=jax.ShapeDtypeStruct((rows, d), BF16),
        grid=(rows // tm,),
        in_specs=[pl.BlockSpec((tm, d), lambda i: (i, 0)), mod, mod,
                  pl.BlockSpec((1, d), lambda i: (0, 0))],
        out_specs=pl.BlockSpec((tm, d), lambda i: (i, 0)),
        compiler_params=pltpu.CompilerParams(
            dimension_semantics=("arbitrary",),
            vmem_limit_bytes=_vmem_limit(2 * tm * d * 6 + (4 * tm * d * 4 if per_row else 0) + 8 * 1024 * 1024)),
        name="norm_mod",
    )(x2d, sh, sc, nw)


def _bf16_weight(w_ref, wb_ref):
    if wb_ref is None:
        return w_ref
    w = w_ref[...].astype(BF16)
    wb_ref[...] = w
    return w


def _proj_rope_kernel(h_ref, w_ref, cos_ref, s1_ref, s2_ref, z_ref, wb_ref=None):
    w = _bf16_weight(w_ref, wb_ref)
    tm = h_ref.shape[0]
    rstep = min(tm, ROPE_ROWS)
    for r0 in range(0, tm, rstep):
        rs = slice(r0, r0 + rstep)
        h = h_ref[rs, :]
        c, s1, s2 = cos_ref[rs, :], s1_ref[rs, :], s2_ref[rs, :]
        for c0 in range(0, z_ref.shape[1], MXU_COLS):
            grp = jnp.dot(h, w[:, c0:c0 + MXU_COLS], preferred_element_type=F32)
            for hh in range(MXU_COLS // HEAD_DIM):
                blk = grp[:, hh * HEAD_DIM:(hh + 1) * HEAD_DIM]
                z_ref[rs, c0 + hh * HEAD_DIM:c0 + (hh + 1) * HEAD_DIM] = (
                    blk * c + pltpu.roll(blk, HEAD_DIM - ROT_DIM // 2, 1) * s1
                    + pltpu.roll(blk, ROT_DIM // 2, 1) * s2)


def _proj_plain_kernel(h_ref, w_ref, z_ref, wb_ref=None):
    w = _bf16_weight(w_ref, wb_ref)
    z_ref[...] = jnp.dot(h_ref[...], w[...], preferred_element_type=F32)


def _proj_gate_kernel(h_ref, w_ref, z_ref, wb_ref=None):
    w = _bf16_weight(w_ref, wb_ref)
    tm = h_ref.shape[0]
    rstep = min(tm, ROPE_ROWS)
    for r0 in range(0, tm, rstep):
        rs = slice(r0, r0 + rstep)
        h = h_ref[rs, :]
        for c0 in range(0, z_ref.shape[1], MXU_COLS):
            cs = slice(c0, c0 + MXU_COLS)
            z_ref[rs, cs] = jax.nn.sigmoid(jnp.dot(h, w[:, cs], preferred_element_type=F32)).astype(BF16)


def _inproj(h, w_in, rope, tm):
    rows, d = h.shape
    emit = not isinstance(w_in, tuple)
    assert not emit or rows == tm

    def call(kern, name, idx, col0, ncols, tn, out_dtype, tables=()):
        w = w_in if emit else w_in[idx]
        vm = (2 * tm * d * 2 + 2 * d * tn * jnp.dtype(w.dtype).itemsize + (3 * d * tn * 2 if emit else 0)
              + 2 * len(tables) * tm * HEAD_DIM * 4 + 2 * tm * tn * jnp.dtype(out_dtype).itemsize
              + 2 * tm * min(tn, COL_BLK) * 4 + 4 * 1024 * 1024)
        tab = pl.BlockSpec((tm, HEAD_DIM), lambda i, j: (i, 0))
        out_shape = [jax.ShapeDtypeStruct((rows, ncols), out_dtype)]
        out_specs = [pl.BlockSpec((tm, tn), lambda i, j: (i, j))]
        if emit:
            out_shape.append(jax.ShapeDtypeStruct((d, ncols), BF16))
            out_specs.append(pl.BlockSpec((d, tn), lambda i, j: (0, j)))
        outs = pl.pallas_call(
            kern,
            out_shape=out_shape,
            grid=(rows // tm, ncols // tn),
            in_specs=[pl.BlockSpec((tm, d), lambda i, j: (i, 0)),
                      pl.BlockSpec((d, tn), lambda i, j, c=(col0 // tn if emit else 0): (0, c + j))]
            + [tab] * len(tables),
            out_specs=out_specs,
            compiler_params=pltpu.CompilerParams(dimension_semantics=("arbitrary", "arbitrary"),
                                                 vmem_limit_bytes=_vmem_limit(vm)),
            name=name,
        )(h, w, *tables)
        return outs if emit else (outs[0], None)

    z_qk, w0 = call(_proj_rope_kernel, "proj_rope", 0, 0, QK_COLS, 2 * COL_BLK, F32, rope)
    z_rest, w1 = call(_proj_plain_kernel, "proj_plain", 1, QK_COLS, R_GA, COL_BLK, F32)
    gates, w2 = call(_proj_gate_kernel, "proj_gate", 2, QK_COLS + R_GA, 2 * D_MODEL, COL_BLK, BF16)
    return (z_qk, z_rest, gates), ((w0, w1, w2) if emit else None)


def _rope_tables(pos):
    half = ROT_DIM // 2
    inv_freq = ROPE_THETA ** (-np.arange(half, dtype=np.float64) * 2.0 / ROT_DIM)
    ang = np.asarray(pos, dtype=np.float64)[:, None] * inv_freq[None, :]
    cos, sin = jnp.asarray(np.cos(ang), dtype=F32), jnp.asarray(np.sin(ang), dtype=F32)
    rows = pos.shape[0]
    ones = jnp.ones((rows, HEAD_DIM - ROT_DIM), F32)
    zer = jnp.zeros((rows, HEAD_DIM - ROT_DIM), F32)
    zh = jnp.zeros((rows, half), F32)
    c = jnp.concatenate([cos, cos, ones], axis=1)
    s1 = jnp.concatenate([-sin, zh, zer], axis=1)
    s2 = jnp.concatenate([zh, sin, zer], axis=1)
    return c, s1, s2


def _attn_tile(q, ka, kb, va, vb, valid):
    k = jnp.concatenate([ka, kb], axis=0).astype(BF16)
    v = jnp.concatenate([va, vb], axis=0).astype(BF16)
    s = lax.dot_general((q * ATT_SCALE).astype(BF16), k, (((1,), (1,)), ((), ())),
                        preferred_element_type=F32)
    s = jnp.where(valid, s, -jnp.inf)
    m = jnp.max(s, axis=1, keepdims=True)
    p = jnp.exp(s - m)
    l = jnp.sum(p, axis=1, keepdims=True)
    o = jnp.dot(p.astype(BF16), v, preferred_element_type=F32) / l
    return o, m + jnp.log(l)


def _attn_kernel(*refs):
    q_refs = refs[0:3]
    kc_refs = refs[3:6]
    kp_refs = refs[6:9]
    vc_refs = refs[9:12]
    vp_refs = refs[12:15]
    o_ref = refs[15]
    og_scr, lse_scr = refs[16], refs[17]
    n = pl.program_id(0)

    row = lax.broadcasted_iota(jnp.int32, (ATT_QB, 2 * ATT_QB), 0)
    col = lax.broadcasted_iota(jnp.int32, (ATT_QB, 2 * ATT_QB), 1)
    band = (col >= row) & (col <= row + ATT_QB)
    band_first = band & (col >= jnp.where(n > 0, 0, ATT_QB))

    for g, (_, dil) in enumerate(ATT_GROUPS):
        q_ref, kc_ref, kp_ref, vc_ref, vp_ref = q_refs[g], kc_refs[g], kp_refs[g], vc_refs[g], vp_refs[g]
        nq = ATT_SB // (dil * ATT_QB)

        def rows(start, count):
            if dil == 1:
                return pl.ds(pl.multiple_of(start, ATT_QB), count)
            return pl.ds(start, count, stride=dil)

        def put(g_, tok0, o, lse):
            og_scr[g_, rows(tok0, ATT_QB), :] = o
            lse_scr[g_, rows(tok0, ATT_QB), :] = jnp.broadcast_to(lse, (ATT_QB, HEAD_DIM))

        def first_body(r, carry):
            o, lse = _attn_tile(q_ref[rows(r, ATT_QB), :],
                                kp_ref[rows(r, ATT_QB), :], kc_ref[rows(r, ATT_QB), :],
                                vp_ref[rows(r, ATT_QB), :], vc_ref[rows(r, ATT_QB), :],
                                band_first)
            put(g, r, o, lse)
            return carry

        lax.fori_loop(0, dil, first_body, 0, unroll=_unroll_for(dil, 16))

        if nq > 1:
            def rest_body(t, carry):
                r = t // (nq - 1)
                u = t % (nq - 1) + 1
                q0 = r + dil * (u * ATT_QB)
                k0 = r + dil * ((u - 1) * ATT_QB)
                k1 = r + dil * (u * ATT_QB)
                o, lse = _attn_tile(q_ref[rows(q0, ATT_QB), :],
                                    kc_ref[rows(k0, ATT_QB), :], kc_ref[rows(k1, ATT_QB), :],
                                    vc_ref[rows(k0, ATT_QB), :], vc_ref[rows(k1, ATT_QB), :],
                                    band)
                put(g, q0, o, lse)
                return carry

            lax.fori_loop(0, dil * (nq - 1), rest_body, 0, unroll=_unroll_for(dil * (nq - 1), 16))

    def merge_body(c, carry):
        rs = pl.ds(pl.multiple_of(c * 256, 256), 256)
        l0, l1, l2 = lse_scr[0, rs, :], lse_scr[1, rs, :], lse_scr[2, rs, :]
        m = jnp.maximum(jnp.maximum(l0, l1), l2)
        e0, e1, e2 = jnp.exp(l0 - m), jnp.exp(l1 - m), jnp.exp(l2 - m)
        num = e0 * og_scr[0, rs, :] + e1 * og_scr[1, rs, :] + e2 * og_scr[2, rs, :]
        o_ref[rs, :] = (num / (e0 + e1 + e2)).astype(o_ref.dtype)
        return carry

    lax.fori_loop(0, ATT_SB // 256, merge_body, 0)


def _attention_prompt(z_qk, z_rest):
    t = z_qk.shape[0]
    nsb = t // ATT_SB
    in_specs, args = [], []

    def cur(colblk):
        return pl.BlockSpec((ATT_SB, HEAD_DIM), lambda n, h, c=colblk: (n, c + h))

    def prev(colblk, dil):
        rows_p = dil * ATT_QB
        per = ATT_SB // rows_p
        return pl.BlockSpec((rows_p, HEAD_DIM),
                            lambda n, h, c=colblk, per=per: (jnp.maximum(n * per - 1, 0), c + h))

    for g in range(3):
        in_specs.append(cur(OFF_QA // HEAD_DIM + g * H_G)); args.append(z_qk)
    for z, base in ((z_qk, OFF_KA), (z_rest, R_VA)):
        for g in range(3):
            in_specs.append(cur(base // HEAD_DIM + g * H_G)); args.append(z)
        for g, (_, dil) in enumerate(ATT_GROUPS):
            in_specs.append(prev(base // HEAD_DIM + g * H_G, dil)); args.append(z)
    blk = ATT_SB * HEAD_DIM * 4
    prev_rows = sum(d * ATT_QB for _, d in ATT_GROUPS)
    vm = 2 * (9 * blk + 2 * prev_rows * HEAD_DIM * 4) + 2 * blk + 6 * blk + 8 * 1024 * 1024
    return pl.pallas_call(
        _attn_kernel,
        out_shape=jax.ShapeDtypeStruct((t, ATT_OUT), BF16),
        grid=(nsb, H_G),
        in_specs=in_specs,
        out_specs=pl.BlockSpec((ATT_SB, HEAD_DIM), lambda n, h: (n, h)),
        scratch_shapes=[pltpu.VMEM((3, ATT_SB, HEAD_DIM), F32),
                        pltpu.VMEM((3, ATT_SB, HEAD_DIM), F32)],
        compiler_params=pltpu.CompilerParams(
            dimension_semantics=("arbitrary", "arbitrary"),
            vmem_limit_bytes=_vmem_limit(vm)),
        name="attn_prompt",
    )(*args)


def _attn_dec_kernel(q_ref, kn_ref, vn_ref, c0_ref, c1_ref, c2_ref, o_ref, n0_ref, n1_ref, n2_ref):
    caches = (c0_ref, c1_ref, c2_ref)
    news = (n0_ref, n1_ref, n2_ref)
    rows_per_pos = 2 * H_G
    zpad = jnp.zeros((LANES - DEC_PAD, HEAD_DIM), F32)

    for g in range(len(ATT_GROUPS)):
        c_ref, n_ref = caches[g], news[g]
        n_rows = c_ref.shape[1]
        keep = n_rows - DEC_SEQ * rows_per_pos
        n_ref[0, 0:keep, :] = c_ref[0, DEC_SEQ * rows_per_pos:n_rows, :]
        for t in range(DEC_SEQ):
            for h in range(H_G):
                hs = slice((g * H_G + h) * HEAD_DIM, (g * H_G + h + 1) * HEAD_DIM)
                r = keep + t * rows_per_pos + h
                n_ref[0, r:r + 1, :] = kn_ref[0, t:t + 1, hs]
                n_ref[0, r + H_G:r + H_G + 1, :] = vn_ref[0, t:t + 1, hs]

    for h in range(H_G):
        outs, lses = [], []
        for g, (win, dil) in enumerate(ATT_GROUPS):
            c_ref = caches[g]
            p_len = c_ref.shape[1] // rows_per_pos
            hs = slice((g * H_G + h) * HEAD_DIM, (g * H_G + h + 1) * HEAD_DIM)
            q = (q_ref[0, :, hs] * ATT_SCALE).astype(BF16)
            k = jnp.concatenate([c_ref[0, pl.ds(h, p_len, stride=rows_per_pos), :], kn_ref[0, :, hs], zpad],
                                axis=0).astype(BF16)
            v = jnp.concatenate([c_ref[0, pl.ds(H_G + h, p_len, stride=rows_per_pos), :],
                                 vn_ref[0, :, hs], zpad], axis=0).astype(BF16)
            s = lax.dot_general(q, k, (((1,), (1,)), ((), ())), preferred_element_type=F32)
            tq = lax.broadcasted_iota(jnp.int32, s.shape, 0) & (DEC_SEQ - 1)
            col = lax.broadcasted_iota(jnp.int32, s.shape, 1)
            delta = p_len + tq - col
            valid = ((delta >= 0) & ((delta & (dil - 1)) == 0) & (delta <= win)
                     & (col < p_len + DEC_SEQ))
            s = jnp.where(valid, s, -jnp.inf)
            m = jnp.max(s, axis=1, keepdims=True)
            p = jnp.exp(s - m)
            l = jnp.sum(p, axis=1, keepdims=True)
            outs.append(jnp.dot(p.astype(BF16), v, preferred_element_type=F32) / l)
            lses.append(m + jnp.log(l))
        m = jnp.maximum(jnp.maximum(lses[0], lses[1]), lses[2])
        e = [jnp.exp(x - m) for x in lses]
        num = e[0] * outs[0] + e[1] * outs[1] + e[2] * outs[2]
        o_ref[0, :, h * HEAD_DIM:(h + 1) * HEAD_DIM] = num / (e[0] + e[1] + e[2])


def _attention_decode(zqk8, zrest8, caches):
    b = zqk8.shape[0]
    qspec = lambda c: pl.BlockSpec((1, DEC_PAD, ATT_W), lambda i, c=c: (i, 0, c))
    cspecs = [pl.BlockSpec((1, c.shape[1], c.shape[2]), lambda i: (i, 0, 0)) for c in caches]
    vm = 4 * sum(c.shape[1] * c.shape[2] * 4 for c in caches) + 12 * 1024 * 1024
    return pl.pallas_call(
        _attn_dec_kernel,
        out_shape=[jax.ShapeDtypeStruct((b, DEC_PAD, ATT_OUT), F32)]
        + [jax.ShapeDtypeStruct(c.shape, F32) for c in caches],
        grid=(b,),
        in_specs=[qspec(OFF_QA // ATT_W), qspec(OFF_KA // ATT_W), qspec(R_VA // ATT_W)] + cspecs,
        out_specs=[pl.BlockSpec((1, DEC_PAD, ATT_OUT), lambda i: (i, 0, 0))] + cspecs,
        compiler_params=pltpu.CompilerParams(
            dimension_semantics=("arbitrary",),
            vmem_limit_bytes=_vmem_limit(vm)),
        name="attn_decode",
    )(zqk8, zqk8, zrest8, *caches)


def _hgrn_consts(c):
    t = np.arange(c)[:, None]
    u = np.arange(c)[None, :]
    seg = [(u <= t)]
    cross = []
    m = c // 2
    while m >= 1:
        blk_t, off_t = t // (2 * m), t % (2 * m)
        piv = blk_t * 2 * m + m - 1
        upper = off_t >= m
        if 1 < m < SUBLANES:
            seg.append(np.where(upper, (u > piv) & (u <= t), (u > t) & (u <= piv)))
        cross.append((t // (2 * m) == u // (2 * m)) & (t % (2 * m) >= m) & (u % (2 * m) < m))
        m //= 2
    cross.append(t == u)
    return (np.stack(seg).astype(np.float32).reshape(-1, c),
            np.stack(cross).astype(np.float32))


def _hgrn_level_weights(b, dsum, fg, c):
    out = []
    m = c // 2
    low = 1
    while m >= 1:
        if m >= SUBLANES:
            parts = []
            for k0 in range(0, c, 2 * m):
                piv = b[k0 + m - 1:k0 + m, :]
                parts += [piv - b[k0:k0 + m, :], b[k0 + m:k0 + 2 * m, :] - piv]
            out.append(jnp.exp2(jnp.concatenate(parts, axis=0)))
        elif m > 1:
            out.append(jnp.exp2(dsum[low * c:(low + 1) * c]))
            low += 1
        else:
            odd = (lax.broadcasted_iota(jnp.int32, fg.shape, 0) & 1) == 1
            out.append(jnp.where(odd, fg, 1.0))
        m //= 2
    return out


def _hgrn_kernel(q_ref, f_ref, i_ref, g_ref, lb_ref, nw_ref, seg_ref, cross_ref, o_ref, s_out_ref, s_scr,
                 *, tb, n_lvl):
    n = pl.program_id(1)
    c = HG_CHUNK

    @pl.when(n == 0)
    def _():
        s_scr[...] = jnp.zeros_like(s_scr)

    lb = lb_ref[...]
    mx = jnp.maximum(lb[0:1, :], lb[1:2, :])
    e0, e1 = jnp.exp(lb[0:1, :] - mx), jnp.exp(lb[1:2, :] - mx)
    lower = e0 / (e0 + e1)
    nw = nw_ref[...]
    seg = seg_ref[...]

    chunks = range(tb // c)
    nt = (((1,), (1,)), ((), ()))
    qs, ks, vs, ws, w_pre, w_suf = [], [], [], [], [], []
    for ch in chunks:
        rs = slice(ch * c, (ch + 1) * c)
        fg = lower + (1.0 - lower) * jax.nn.sigmoid(f_ref[rs, :])
        lg = jnp.log2(fg)
        lg_hi = lg.astype(BF16)
        lg_lo = (lg - lg_hi.astype(F32)).astype(BF16)
        dsum = (jnp.dot(seg, lg_hi, preferred_element_type=F32)
                + jnp.dot(seg, lg_lo, preferred_element_type=F32))
        b = dsum[0:c]
        ws.append(_hgrn_level_weights(b, dsum, fg, c))
        w_pre.append(jnp.exp2(b))
        w_suf.append(jnp.exp2(b[c - 1:c, :] - b))
        qs.append(_silu(q_ref[rs, :]))
        ks.append(1.0 - fg)
        vs.append(i_ref[rs, :].astype(BF16))
    qb = [qs[ch].astype(BF16) for ch in chunks]
    kb = [ks[ch].astype(BF16) for ch in chunks]
    atts = [cross_ref[n_lvl] * lax.dot_general(qb[ch], kb[ch], nt, preferred_element_type=F32)
            for ch in chunks]
    for lv in range(n_lvl):
        for ch in chunks:
            wl = ws[ch][lv].astype(BF16)
            atts[ch] = atts[ch] + cross_ref[lv] * lax.dot_general(
                qb[ch] * wl, kb[ch] * wl, nt, preferred_element_type=F32)
    o_intra = [jnp.dot(atts[ch].astype(BF16), vs[ch], preferred_element_type=F32) for ch in chunks]
    q_dec = [(qs[ch] * w_pre[ch]).astype(BF16) for ch in chunks]
    k_end_t = [(ks[ch] * w_suf[ch]).T.astype(BF16) for ch in chunks]
    decay_col = [jnp.broadcast_to(w_pre[ch][c - 1:c, :], (HG_DK, HG_DV)).T for ch in chunks]
    kv = [jnp.dot(k_end_t[ch], vs[ch], preferred_element_type=F32) for ch in chunks]
    s_cur = s_scr[...]
    for ch in chunks:
        rs = slice(ch * c, (ch + 1) * c)
        o = o_intra[ch] + jnp.dot(q_dec[ch], s_cur.astype(BF16), preferred_element_type=F32)
        s_cur = decay_col[ch] * s_cur + kv[ch]
        ms = jnp.mean(o * o, axis=-1, keepdims=True)
        o_ref[rs, :] = (o * lax.rsqrt(ms + EPS) * nw * _silu(g_ref[rs, :])).astype(o_ref.dtype)
    s_scr[...] = s_cur

    @pl.when(n == pl.num_programs(1) - 1)
    def _():
        s_out_ref[0] = s_scr[...]


def _hgrn_prompt(z, hg_lb, hg_norm_w, tb):
    t = z.shape[0]
    seg_np, cross_np = _hgrn_consts(HG_CHUNK)
    n_lvl = cross_np.shape[0] - 1
    seg = jnp.asarray(seg_np, dtype=BF16)
    cross = jnp.asarray(cross_np, dtype=F32)
    col = lambda off: pl.BlockSpec((tb, HG_DK), lambda h, n, o=off // HG_DK: (n, o + h))
    kern = functools.partial(_hgrn_kernel, tb=tb, n_lvl=n_lvl)
    vm = 2 * 5 * tb * HG_DK * 4 + 2 * seg_np.size * 2 + 2 * cross_np.size * 4 + 16 * 1024 * 1024
    return pl.pallas_call(
        kern,
        out_shape=(jax.ShapeDtypeStruct((t, HG_WV), BF16),
                   jax.ShapeDtypeStruct((HG_HEADS, HG_DK, HG_DV), F32)),
        grid=(HG_HEADS, t // tb),
        in_specs=[col(R_QH), col(R_FH), col(R_IH), col(R_GH),
                  pl.BlockSpec((2, HG_DK), lambda h, n: (0, h)),
                  pl.BlockSpec((1, HG_DV), lambda h, n: (0, 0)),
                  pl.BlockSpec(seg_np.shape, lambda h, n: (0, 0)),
                  pl.BlockSpec(cross_np.shape, lambda h, n: (0, 0, 0))],
        out_specs=(pl.BlockSpec((tb, HG_DV), lambda h, n: (n, h)),
                   pl.BlockSpec((1, HG_DK, HG_DV), lambda h, n: (h, 0, 0))),
        scratch_shapes=[pltpu.VMEM((HG_DK, HG_DV), F32)],
        compiler_params=pltpu.CompilerParams(
            dimension_semantics=("arbitrary", "arbitrary"),
            vmem_limit_bytes=_vmem_limit(vm)),
        name="hgrn_prompt",
    )(z, z, z, z, hg_lb, hg_norm_w, seg, cross)


def _hgrn_dec_kernel(q_ref, f_ref, i_ref, g_ref, lb_ref, nw_ref, s_ref, o_ref, s_out_ref, *, heads):
    lb = lb_ref[...]
    mx = jnp.maximum(lb[0:1, :], lb[1:2, :])
    e0, e1 = jnp.exp(lb[0:1, :] - mx), jnp.exp(lb[1:2, :] - mx)
    lower = e0 / (e0 + e1)
    nw = nw_ref[...]
    zpad = jnp.zeros((HG_DK - DEC_PAD, HG_DK), F32)
    row = lax.broadcasted_iota(jnp.int32, (DEC_PAD, HG_DV), 0)

    def before(x, d):
        return pltpu.roll(x, d, 0)

    def after(x, d):
        return pltpu.roll(x, DEC_PAD - d, 0)

    def split(x):
        hi = x.astype(BF16)
        return hi, (x - hi.astype(F32)).astype(BF16)

    def dot3(a, b):
        (ah, al), (bh, bl) = split(a), split(b)
        return (jnp.dot(ah, bh, preferred_element_type=F32) + jnp.dot(ah, bl, preferred_element_type=F32)
                + jnp.dot(al, bh, preferred_element_type=F32))

    for h in range(heads):
        hs = slice(h * HG_DK, (h + 1) * HG_DK)
        q = _silu(q_ref[0, :, hs])
        fg = lower[:, hs] + (1.0 - lower[:, hs]) * jax.nn.sigmoid(f_ref[0, :, hs])
        k = 1.0 - fg
        v = i_ref[0, :, hs]
        s0 = s_ref[0, h]
        p = [jnp.ones_like(fg), fg]
        for d in range(2, DEC_SEQ + 1):
            p.append(p[d - 1] * before(fg, d - 1))
        f_cum = p[DEC_SEQ]
        for t in range(DEC_SEQ - 1):
            f_cum = jnp.where(row == t, p[t + 1], f_cum)
        g_suf = jnp.ones_like(fg)
        run = jnp.ones_like(fg)
        for d in range(1, DEC_SEQ):
            run = run * after(fg, d)
            g_suf = jnp.where(row == DEC_SEQ - 1 - d, run, g_suf)
        o = dot3(q * f_cum, s0)
        for d in range(DEC_SEQ):
            coef = jnp.sum(q * p[d] * (before(k, d) if d else k), axis=1, keepdims=True)
            o = o + jnp.where(row >= d, coef, 0.0) * (before(v, d) if d else v)
        kd = jnp.where(row < DEC_SEQ, k * g_suf, 0.0)
        kd_t = jnp.concatenate([kd, zpad], axis=0).T
        v_pad = jnp.concatenate([v, zpad], axis=0)
        decay_col = jnp.broadcast_to(f_cum[DEC_SEQ - 1:DEC_SEQ, :], (HG_DK, HG_DV)).T
        s_out_ref[0, h] = decay_col * s0 + dot3(kd_t, v_pad)
        ms = jnp.mean(o * o, axis=-1, keepdims=True)
        o_ref[0, :, hs] = o * lax.rsqrt(ms + EPS) * nw * _silu(g_ref[0, :, hs])


def _hgrn_decode(z8, hg_lb, hg_norm_w, state):
    b = z8.shape[0]
    heads = COL_BLK // HG_DK
    nhb = HG_HEADS // heads
    col = lambda off: pl.BlockSpec((1, DEC_PAD, COL_BLK), lambda i, j, o=off // COL_BLK: (i, 0, o + j))
    kern = functools.partial(_hgrn_dec_kernel, heads=heads)
    sspec = pl.BlockSpec((1, heads, HG_DK, HG_DV), lambda i, j: (i, j, 0, 0))
    return pl.pallas_call(
        kern,
        out_shape=(jax.ShapeDtypeStruct((b, DEC_PAD, HG_WV), F32),
                   jax.ShapeDtypeStruct(state.shape, F32)),
        grid=(b, nhb),
        in_specs=[col(R_QH), col(R_FH), col(R_IH), col(R_GH),
                  pl.BlockSpec((2, COL_BLK), lambda i, j: (0, j)),
                  pl.BlockSpec((1, HG_DV), lambda i, j: (0, 0)),
                  sspec],
        out_specs=(pl.BlockSpec((1, DEC_PAD, COL_BLK), lambda i, j: (i, 0, j)), sspec),
        compiler_params=pltpu.CompilerParams(
            dimension_semantics=("arbitrary", "arbitrary"),
            vmem_limit_bytes=_vmem_limit(0)),
        name="hgrn_decode",
    )(z8, z8, z8, z8, hg_lb, hg_norm_w, state)


def _merge_kernel(att_ref, hg_ref, ga_ref, gb_ref, x_ref, g1_ref, wpa_ref, wpb_ref, wo_ref, o_ref):
    att, hg = att_ref[...], hg_ref[...]
    parts = []
    for c0 in range(0, wpa_ref.shape[1], MXU_COLS):
        cs = slice(c0, c0 + MXU_COLS)
        ya = jnp.dot(att, wpa_ref[:, cs], preferred_element_type=F32)
        yb = jnp.dot(hg, wpb_ref[:, cs], preferred_element_type=F32)
        parts.append((ga_ref[:, cs].astype(F32) * ya + gb_ref[:, cs].astype(F32) * yb).astype(BF16))
    ymix = jnp.concatenate(parts, axis=1)
    o_ref[...] = x_ref[...] + g1_ref[...] * jnp.dot(ymix, wo_ref[...], preferred_element_type=F32)


def _merge(att, hg, gates, x2d, g1, wpa_bf, wpb_bf, wo_bf, tm):
    rows, d = x2d.shape
    per_row = g1.shape[0] != 1
    resident = pl.Buffered(1)
    g1_spec = (pl.BlockSpec((tm, d), lambda i: (i, 0)) if per_row else pl.BlockSpec((1, d), lambda i: (0, 0)))
    vm = (2 * tm * (ATT_OUT + HG_WV + 2 * d) * 2 + 4 * tm * d * 4 + (ATT_OUT * d + HG_WV * d + d * d) * 2
          + (2 * tm * d * 4 if per_row else 0) + tm * d * 10 + 4 * 1024 * 1024)
    return pl.pallas_call(
        _merge_kernel,
        out_shape=jax.ShapeDtypeStruct((rows, d), F32),
        grid=(rows // tm,),
        in_specs=[pl.BlockSpec((tm, ATT_OUT), lambda i: (i, 0)),
                  pl.BlockSpec((tm, HG_WV), lambda i: (i, 0)),
                  pl.BlockSpec((tm, d), lambda i: (i, 0)),
                  pl.BlockSpec((tm, d), lambda i: (i, 1)),
                  pl.BlockSpec((tm, d), lambda i: (i, 0)),
                  g1_spec,
                  pl.BlockSpec((ATT_OUT, d), lambda i: (0, 0), pipeline_mode=resident),
                  pl.BlockSpec((HG_WV, d), lambda i: (0, 0), pipeline_mode=resident),
                  pl.BlockSpec((d, d), lambda i: (0, 0), pipeline_mode=resident)],
        out_specs=pl.BlockSpec((tm, d), lambda i: (i, 0)),
        compiler_params=pltpu.CompilerParams(
            dimension_semantics=("arbitrary",),
            vmem_limit_bytes=_vmem_limit(vm)),
        name="merge",
    )(att, hg, gates, gates, x2d, g1, wpa_bf, wpb_bf, wo_bf)


def _ffn_kernel(x_ref, sh_ref, sc_ref, g2_ref, nw_ref, nf_ref, wa_ref, wb_ref, wd_ref, cw_ref, cb_ref,
                p1_ref, p2_ref, y_ref, tail_ref, *rest, tm, ta, per_row, seq_rows, emit):
    wa_out, wb_out, wd_out = rest[:3] if emit else (None, None, None)
    h_scr, acc_scr, carry_scr = rest[-3:]
    wa_ref = _bf16_weight(wa_ref, wa_out)
    wb_ref = _bf16_weight(wb_ref, wb_out)
    wd_ref = _bf16_weight(wd_ref, wd_out)
    i = pl.program_id(0)
    f = pl.program_id(1)

    @pl.when(f == 0)
    def _():
        _norm_mod_rows(x_ref, nw_ref, sc_ref, sh_ref, h_scr, tm, per_row)
        acc_scr[...] = jnp.zeros_like(acc_scr)

    if seq_rows is None:
        @pl.when(i == 0)
        def _():
            carry_scr[f] = p1_ref[...]

    tf = wa_ref.shape[1]
    groups = [slice(c0, c0 + MXU_COLS) for c0 in range(0, tf, MXU_COLS)]
    rc = min(tm, FFN_ROWS)
    row = lax.broadcasted_iota(jnp.int32, (rc, MXU_COLS), 0)
    if seq_rows is None:
        tails = [carry_scr[f, :, cs] for cs in groups]
    for r0 in range(0, tm, rc):
        rs = slice(r0, r0 + rc)
        h = h_scr[rs, :]
        a_parts = [jnp.dot(h, wa_ref[:, cs], preferred_element_type=F32) for cs in groups]
        b_parts = [jnp.dot(h, wb_ref[:, cs], preferred_element_type=F32) for cs in groups]
        ys = []
        for gi, (cs, a, b) in enumerate(zip(groups, a_parts, b_parts)):
            r1 = pltpu.roll(a, 1, 0)
            r2 = pltpu.roll(a, 2, 0)
            if seq_rows is None:
                prev = tails[gi]
                a1 = jnp.where(row == 0, prev[SUBLANES - 1:SUBLANES, :], r1)
                a2 = jnp.where(row == 0, prev[SUBLANES - 2:SUBLANES - 1, :],
                               jnp.where(row == 1, prev[SUBLANES - 1:SUBLANES, :], r2))
                tails[gi] = a[rc - SUBLANES:, :]
            else:
                t = row & (seq_rows - 1)
                a1 = jnp.where(t == 0, p1_ref[rs, cs], r1)
                a2 = jnp.where(t <= 1, p2_ref[rs, cs], r2)
                tail_ref[rs, cs] = a
            cw = cw_ref[:, cs]
            u = cb_ref[:, cs] + a2 * cw[0:1, :] + a1 * cw[1:2, :] + a * cw[2:3, :]
            ys.append((_silu(u) * b).astype(BF16))
        y = jnp.concatenate(ys, axis=1)
        acc_scr[rs, :] += jnp.dot(y, wd_ref[...], preferred_element_type=F32)
    if seq_rows is None:
        for cs, tail in zip(groups, tails):
            carry_scr[f, :, cs] = tail
            tail_ref[:, cs] = tail

    @pl.when(f == pl.num_programs(1) - 1)
    def _():
        step = min(tm, 128)

        def body(r, carry):
            rs = pl.ds(pl.multiple_of(r * step, step), step)
            g2 = g2_ref[rs, :] if per_row else g2_ref[...]
            x2 = x_ref[rs, :] + g2 * acc_scr[rs, :]
            ms = jnp.mean(x2 * x2, axis=-1, keepdims=True)
            y_ref[rs, :] = x2 * lax.rsqrt(ms + EPS) * nf_ref[...]
            return carry

        lax.fori_loop(0, tm // step, body, 0)


def _ffn(x1, sh, sc, g2, nw, nf, wa, wb, wd, conv_w, conv_b, p1, p2, tm, seq_rows):
    rows, d = x1.shape
    dff = wa.shape[1]
    per_row = sh.shape[0] != 1
    emit = wa.dtype == F32
    assert not emit or rows == tm
    tf = MXU_COLS if emit else COL_BLK
    nf_blk = dff // tf
    ta = SUBLANES if seq_rows is None else tm
    prow = SUBLANES if seq_rows is None else tm
    kern = functools.partial(_ffn_kernel, tm=tm, ta=ta, per_row=per_row, seq_rows=seq_rows, emit=emit)
    pspec = pl.BlockSpec((prow, tf), lambda i, f: (0 if seq_rows is None else i, f))
    w_specs = [pl.BlockSpec((d, tf), lambda i, f: (0, f)),
               pl.BlockSpec((d, tf), lambda i, f: (0, f)),
               pl.BlockSpec((tf, d), lambda i, f: (f, 0))]
    vm = (4 * tm * d * 4 + tm * d * 2 + tm * d * 4 + 3 * d * tf * (14 if emit else 4) + 10 * tm * tf * 4
          + (6 * tm * d * 4 if per_row else 0) + 8 * 1024 * 1024)
    return pl.pallas_call(
        kern,
        out_shape=(jax.ShapeDtypeStruct((rows, d), F32),
                   jax.ShapeDtypeStruct((ta * (rows // tm), dff), F32))
        + ((jax.ShapeDtypeStruct(wa.shape, BF16), jax.ShapeDtypeStruct(wb.shape, BF16),
            jax.ShapeDtypeStruct(wd.shape, BF16)) if emit else ()),
        grid=(rows // tm, nf_blk),
        in_specs=[pl.BlockSpec((tm, d), lambda i, f: (i, 0)),
                  _mod_spec(per_row, tm, d), _mod_spec(per_row, tm, d), _mod_spec(per_row, tm, d),
                  pl.BlockSpec((1, d), lambda i, f: (0, 0)),
                  pl.BlockSpec((1, d), lambda i, f: (0, 0))]
        + w_specs
        + [pl.BlockSpec((CONV_W, tf), lambda i, f: (0, f)),
           pl.BlockSpec((1, tf), lambda i, f: (0, f)),
           pspec, pspec],
        out_specs=(pl.BlockSpec((tm, d), lambda i, f: (i, 0)),
                   pl.BlockSpec((ta, tf), lambda i, f: (i, f))) + (tuple(w_specs) if emit else ()),
        scratch_shapes=[pltpu.VMEM((tm, d), BF16), pltpu.VMEM((tm, d), F32),
                        pltpu.VMEM((nf_blk, SUBLANES, tf), F32)],
        compiler_params=pltpu.CompilerParams(
            dimension_semantics=("arbitrary", "arbitrary"),
            vmem_limit_bytes=_vmem_limit(vm)),
        name="ffn",
    )(x1, sh, sc, g2, nw, nf, wa, wb, wd, conv_w, conv_b, p1, p2)


def kernel(x_prompt, x_sample, cache_kv_w128, cache_kv_w512, cache_kv_w2048, state_hgrn, state_conv,
           c_prompt, c_sample, w_ada, b_ada, norm1_w, w_in, hg_lb, hg_norm_w, w_pa, w_pb, w_o,
           norm2_w, w_ffn_a, w_ffn_b, conv_w, conv_b, w_ffn_down, norm_f_w):
    d = D_MODEL
    bp, t, _ = x_prompt.shape
    bs, ts, _ = x_sample.shape
    assert bp == 1 and ts == DEC_SEQ and w_ada.shape[0] == 1
    rows_s = bs * ts

    wpa_bf, wpb_bf, wo_bf = w_pa[0].astype(BF16), w_pb[0].astype(BF16), w_o[0].astype(BF16)
    nw1, nw2, nwf = norm1_w[0].reshape(1, d), norm2_w[0].reshape(1, d), norm_f_w.reshape(1, d)
    hg_nw = hg_norm_w[0].reshape(1, HG_DV)
    cw, cb = conv_w[0], conv_b[0].reshape(1, D_FF)

    n_seq = bp + bs
    pad_rows = -n_seq % SUBLANES
    c_all = jnp.concatenate([c_prompt, c_sample, jnp.zeros((pad_rows, d), F32)], axis=0)
    mod = _ada(c_all, w_ada[0], b_ada[0])
    mod_p = [mod[0:bp, k * d:(k + 1) * d] for k in range(N_MOD)]
    mod_s = [jnp.repeat(mod[bp:n_seq, k * d:(k + 1) * d], ts, axis=0) for k in range(N_MOD)]

    xs = x_sample.reshape(rows_s, d)
    rope_s = _rope_tables(PAST_LEN + np.arange(rows_s) % ts)
    hs = _norm_mod(xs, mod_s[0], mod_s[1], nw1, tm=rows_s)
    (zqk_s, zr_s, gates_s), w_in_bf = _inproj(hs, w_in[0], rope_s, tm=rows_s)
    pad8 = lambda z: jnp.pad(z.reshape(bs, ts, z.shape[1]), ((0, 0), (0, DEC_PAD - ts), (0, 0)))
    zqk_s8, zr_s8 = pad8(zqk_s), pad8(zr_s)
    cache_in = (cache_kv_w128, cache_kv_w512, cache_kv_w2048)
    caches = [c.reshape(bs, c.shape[2] * 2 * H_G, HEAD_DIM) for c in cache_in]
    att_s8, *new_caches = _attention_decode(zqk_s8, zr_s8, caches)
    att_s = att_s8[:, :ts].reshape(rows_s, ATT_OUT).astype(BF16)
    kv_s = [n.reshape(c.shape) for n, c in zip(new_caches, cache_in)]
    hg_s8, s_s = _hgrn_decode(zr_s8, hg_lb, hg_nw, state_hgrn[0])
    hg_s = hg_s8[:, :ts].reshape(rows_s, HG_WV).astype(BF16)
    x1s = _merge(att_s, hg_s, gates_s, xs, mod_s[2], wpa_bf, wpb_bf, wo_bf, tm=rows_s)
    buf = state_conv[0]
    zrow = jnp.zeros((bs, 1, D_FF), F32)
    p1 = jnp.concatenate([buf[:, 1:2], zrow, zrow, zrow], axis=1).reshape(rows_s, D_FF)
    p2 = jnp.concatenate([buf[:, 0:1], buf[:, 1:2], zrow, zrow], axis=1).reshape(rows_s, D_FF)
    ys, a_s, wa_bf, wb_bf, wd_bf = _ffn(x1s, mod_s[3], mod_s[4], mod_s[5], nw2, nwf,
                                        w_ffn_a[0], w_ffn_b[0], w_ffn_down[0], cw, cb,
                                        p1, p2, tm=rows_s, seq_rows=ts)

    xp = x_prompt.reshape(t, d)
    rope_p = _rope_tables(np.arange(t))
    hp = _norm_mod(xp, mod_p[0], mod_p[1], nw1, tm=1024)
    (zqk_p, zr_p, gates_p), _ = _inproj(hp, w_in_bf, rope_p, tm=2048)
    att_p = _attention_prompt(zqk_p, zr_p)
    hg_p, s_p = _hgrn_prompt(zr_p, hg_lb, hg_nw, tb=2048)
    x1p = _merge(att_p, hg_p, gates_p, xp, mod_p[2], wpa_bf, wpb_bf, wo_bf, tm=512)
    conv0 = jnp.zeros((SUBLANES, D_FF), F32)
    yp, tail_p = _ffn(x1p, mod_p[3], mod_p[4], mod_p[5], nw2, nwf, wa_bf, wb_bf, wd_bf, cw, cb,
                      conv0, conv0, tm=512, seq_rows=None)

    kv_p = []
    for g, (win, _) in enumerate(ATT_GROUPS):
        keep = min(win, t)
        ks = zqk_p[t - keep:, OFF_KA + g * ATT_OUT:OFF_KA + (g + 1) * ATT_OUT]
        vs = zr_p[t - keep:, R_VA + g * ATT_OUT:R_VA + (g + 1) * ATT_OUT]
        kv_p.append(jnp.stack([ks.reshape(keep, H_G, HEAD_DIM), vs.reshape(keep, H_G, HEAD_DIM)],
                              axis=1)[None, None])
    hgrn_p = s_p[None, None]
    conv_p = tail_p[tail_p.shape[0] - (CONV_W - 1):][None, None]

    hgrn_s = s_s[None]
    conv_s = a_s.reshape(bs, ts, D_FF)[:, ts - (CONV_W - 1):][None]

    return (yp.reshape(bp, t, d), ys.reshape(bs, ts, d),
            kv_p[0], kv_p[1], kv_p[2], hgrn_p, conv_p,
            kv_s[0], kv_s[1], kv_s[2], hgrn_s, conv_s)
```

```python
import functools

import numpy as np
import jax
import jax.numpy as jnp
from jax import lax
from jax.experimental import pallas as pl
from jax.experimental.pallas import tpu as pltpu

F32 = jnp.float32
BF16 = jnp.bfloat16

D_MODEL = 2048
SEQ = 16384
DEC_BATCH = 32
DEC_SEQ = 4
PAST_LEN = 16384

HEAD_DIM = 128
ATT_GROUPS = ((128, 1), (512, 4), (2048, 16))
H_G = 4
N_ATT_HEADS = H_G * len(ATT_GROUPS)
ATT_W = N_ATT_HEADS * HEAD_DIM
ATT_OUT = H_G * HEAD_DIM
ROT_DIM = HEAD_DIM // 4
ROPE_THETA = 500000.0
ATT_SCALE = HEAD_DIM ** -0.5

HG_HEADS = 8
HG_DK = 128
HG_DV = 128
HG_WK = HG_HEADS * HG_DK
HG_WV = HG_HEADS * HG_DV

D_FF = 5632
CONV_W = 3
N_MOD = 6
EPS = 1e-6

IN_SIZES = (ATT_W, ATT_W, ATT_W, HG_WK, HG_WK, HG_WV, HG_WV, D_MODEL, D_MODEL)
IN_TOTAL = sum(IN_SIZES)
IN_OFFS = tuple(int(s) for s in np.cumsum((0,) + IN_SIZES)[:-1])
OFF_QA, OFF_KA, OFF_VA, OFF_QH, OFF_FH, OFF_IH, OFF_GH, OFF_GA, OFF_GB = IN_OFFS
QK_COLS = 2 * ATT_W
R_VA, R_QH, R_FH, R_IH, R_GH, R_GA, R_GB = (o - QK_COLS for o in IN_OFFS[2:])

V7X_VMEM_BYTES = 64 * 1024 * 1024
SUBLANES = 8
LANES = 128
DEC_PAD = SUBLANES

COL_BLK = 512
MXU_COLS = 256
ROPE_ROWS = 256
FFN_ROWS = 512
FFN_EDGE_ROWS = 256
HG_CHUNK = 128
ATT_QB = 128
ATT_SB = 2048


def _vmem_limit(nbytes):
    return int(min(V7X_VMEM_BYTES - 8 * 1024 * 1024, max(nbytes, 16 * 1024 * 1024)))


def _silu(x):
    return x * jax.nn.sigmoid(x)


def _unroll_for(trips, max_unroll=6):
    return next(u for u in range(min(max_unroll, trips), 0, -1) if trips % u == 0)


def _ada_kernel(c_ref, w_ref, b_ref, o_ref):
    c = c_ref[...]
    s = _silu(c).astype(BF16)
    o_ref[...] = jnp.dot(s, w_ref[...].astype(BF16), preferred_element_type=F32) + b_ref[...]


def _ada(c_all, w_ada, b_ada):
    rows, d = c_all.shape
    n = w_ada.shape[1]
    tn = 1024
    return pl.pallas_call(
        _ada_kernel,
        out_shape=jax.ShapeDtypeStruct((rows, n), F32),
        grid=(n // tn,),
        in_specs=[pl.BlockSpec((rows, d), lambda j: (0, 0)),
                  pl.BlockSpec((d, tn), lambda j: (0, j)),
                  pl.BlockSpec((1, tn), lambda j: (0, j))],
        out_specs=pl.BlockSpec((rows, tn), lambda j: (0, j)),
        compiler_params=pltpu.CompilerParams(
            dimension_semantics=("arbitrary",),
            vmem_limit_bytes=_vmem_limit(2 * d * tn * 4 + d * tn * 2 + 4 * rows * (d + tn) * 4)),
        name="ada",
    )(c_all, w_ada, b_ada.reshape(1, n))


def _norm_mod_rows(x_ref, nw_ref, sc_ref, sh_ref, h_ref, tm, per_row):
    step = min(tm, 128)

    def body(r, carry):
        rs = pl.ds(pl.multiple_of(r * step, step), step)
        x = x_ref[rs, :]
        ms = jnp.mean(x * x, axis=-1, keepdims=True)
        y = x * lax.rsqrt(ms + EPS) * nw_ref[...]
        if per_row:
            h = y * (1.0 + sc_ref[rs, :]) + sh_ref[rs, :]
        else:
            h = y * (1.0 + sc_ref[...]) + sh_ref[...]
        h_ref[rs, :] = h.astype(BF16)
        return carry

    lax.fori_loop(0, tm // step, body, 0)


def _mod_spec(per_row, tm, d):
    if per_row:
        return pl.BlockSpec((tm, d), lambda i, j: (i, 0))
    return pl.BlockSpec((1, d), lambda i, j: (0, 0))


def _norm_kernel(x_ref, sh_ref, sc_ref, nw_ref, h_ref, *, tm, per_row):
    _norm_mod_rows(x_ref, nw_ref, sc_ref, sh_ref, h_ref, tm, per_row)


def _norm_mod(x2d, sh, sc, nw, tm):
    rows, d = x2d.shape
    per_row = sh.shape[0] != 1
    mod = (pl.BlockSpec((tm, d), lambda i: (i, 0)) if per_row else pl.BlockSpec((1, d), lambda i: (0, 0)))
    return pl.pallas_call(
        functools.partial(_norm_kernel, tm=tm, per_row=per_row),
        out_shape=jax.ShapeDtypeStruct((rows, d), BF16),
        grid=(rows // tm,),
        in_specs=[pl.BlockSpec((tm, d), lambda i: (i, 0)), mod, mod,
                  pl.BlockSpec((1, d), lambda i: (0, 0))],
        out_specs=pl.BlockSpec((tm, d), lambda i: (i, 0)),
        compiler_params=pltpu.CompilerParams(
            dimension_semantics=("arbitrary",),
            vmem_limit_bytes=_vmem_limit(2 * tm * d * 6 + (4 * tm * d * 4 if per_row else 0) + 8 * 1024 * 1024)),
        name="norm_mod",
    )(x2d, sh, sc, nw)


def _bf16_weight(w_ref, wb_ref):
    if wb_ref is None:
        return w_ref
    w = w_ref[...].astype(BF16)
    wb_ref[...] = w
    return w


def _proj_rope_kernel(h_ref, w_ref, cos_ref, s1_ref, s2_ref, z_ref, wb_ref=None):
    w = _bf16_weight(w_ref, wb_ref)
    tm = h_ref.shape[0]
    rstep = min(tm, ROPE_ROWS)
    for r0 in range(0, tm, rstep):
        rs = slice(r0, r0 + rstep)
        h = h_ref[rs, :]
        c, s1, s2 = cos_ref[rs, :], s1_ref[rs, :], s2_ref[rs, :]
        for c0 in range(0, z_ref.shape[1], MXU_COLS):
            grp = jnp.dot(h, w[:, c0:c0 + MXU_COLS], preferred_element_type=F32)
            for hh in range(MXU_COLS // HEAD_DIM):
                blk = grp[:, hh * HEAD_DIM:(hh + 1) * HEAD_DIM]
                z_ref[rs, c0 + hh * HEAD_DIM:c0 + (hh + 1) * HEAD_DIM] = (
                    blk * c + pltpu.roll(blk, HEAD_DIM - ROT_DIM // 2, 1) * s1
                    + pltpu.roll(blk, ROT_DIM // 2, 1) * s2)


def _proj_plain_kernel(h_ref, w_ref, z_ref, wb_ref=None):
    w = _bf16_weight(w_ref, wb_ref)
    z_ref[...] = jnp.dot(h_ref[...], w[...], preferred_element_type=F32)


def _proj_gate_kernel(h_ref, w_ref, z_ref, wb_ref=None):
    w = _bf16_weight(w_ref, wb_ref)
    tm = h_ref.shape[0]
    rstep = min(tm, ROPE_ROWS)
    for r0 in range(0, tm, rstep):
        rs = slice(r0, r0 + rstep)
        h = h_ref[rs, :]
        for c0 in range(0, z_ref.shape[1], MXU_COLS):
            cs = slice(c0, c0 + MXU_COLS)
            z_ref[rs, cs] = jax.nn.sigmoid(jnp.dot(h, w[:, cs], preferred_element_type=F32)).astype(BF16)


def _inproj(h, w_in, rope, tm):
    rows, d = h.shape
    emit = not isinstance(w_in, tuple)
    assert not emit or rows == tm

    def call(kern, name, idx, col0, ncols, tn, out_dtype, tables=()):
        w = w_in if emit else w_in[idx]
        vm = (2 * tm * d * 2 + 2 * d * tn * jnp.dtype(w.dtype).itemsize + (3 * d * tn * 2 if emit else 0)
              + 2 * len(tables) * tm * HEAD_DIM * 4 + 2 * tm * tn * jnp.dtype(out_dtype).itemsize
              + 2 * tm * min(tn, COL_BLK) * 4 + 4 * 1024 * 1024)
        tab = pl.BlockSpec((tm, HEAD_DIM), lambda i, j: (i, 0))
        out_shape = [jax.ShapeDtypeStruct((rows, ncols), out_dtype)]
        out_specs = [pl.BlockSpec((tm, tn), lambda i, j: (i, j))]
        if emit:
            out_shape.append(jax.ShapeDtypeStruct((d, ncols), BF16))
            out_specs.append(pl.BlockSpec((d, tn), lambda i, j: (0, j)))
        outs = pl.pallas_call(
            kern,
            out_shape=out_shape,
            grid=(rows // tm, ncols // tn),
            in_specs=[pl.BlockSpec((tm, d), lambda i, j: (i, 0)),
                      pl.BlockSpec((d, tn), lambda i, j, c=(col0 // tn if emit else 0): (0, c + j))]
            + [tab] * len(tables),
            out_specs=out_specs,
            compiler_params=pltpu.CompilerParams(dimension_semantics=("arbitrary", "arbitrary"),
                                                 vmem_limit_bytes=_vmem_limit(vm)),
            name=name,
        )(h, w, *tables)
        return outs if emit else (outs[0], None)

    z_qk, w0 = call(_proj_rope_kernel, "proj_rope", 0, 0, QK_COLS, 2 * COL_BLK, F32, rope)
    z_rest, w1 = call(_proj_plain_kernel, "proj_plain", 1, QK_COLS, R_GA, COL_BLK, F32)
    gates, w2 = call(_proj_gate_kernel, "proj_gate", 2, QK_COLS + R_GA, 2 * D_MODEL, COL_BLK, BF16)
    return (z_qk, z_rest, gates), ((w0, w1, w2) if emit else None)


def _rope_tables(pos):
    half = ROT_DIM // 2
    inv_freq = ROPE_THETA ** (-np.arange(half, dtype=np.float64) * 2.0 / ROT_DIM)
    ang = np.asarray(pos, dtype=np.float64)[:, None] * inv_freq[None, :]
    cos, sin = jnp.asarray(np.cos(ang), dtype=F32), jnp.asarray(np.sin(ang), dtype=F32)
    rows = pos.shape[0]
    ones = jnp.ones((rows, HEAD_DIM - ROT_DIM), F32)
    zer = jnp.zeros((rows, HEAD_DIM - ROT_DIM), F32)
    zh = jnp.zeros((rows, half), F32)
    c = jnp.concatenate([cos, cos, ones], axis=1)
    s1 = jnp.concatenate([-sin, zh, zer], axis=1)
    s2 = jnp.concatenate([zh, sin, zer], axis=1)
    return c, s1, s2


def _attn_tile(q, ka, kb, va, vb, valid):
    k = jnp.concatenate([ka, kb], axis=0).astype(BF16)
    v = jnp.concatenate([va, vb], axis=0).astype(BF16)
    s = lax.dot_general((q * ATT_SCALE).astype(BF16), k, (((1,), (1,)), ((), ())),
                        preferred_element_type=F32)
    s = jnp.where(valid, s, -jnp.inf)
    m = jnp.max(s, axis=1, keepdims=True)
    p = jnp.exp(s - m)
    l = jnp.sum(p, axis=1, keepdims=True)
    o = jnp.dot(p.astype(BF16), v, preferred_element_type=F32) / l
    return o, m + jnp.log(l)


def _attn_kernel(*refs):
    q_refs = refs[0:3]
    kc_refs = refs[3:6]
    kp_refs = refs[6:9]
    vc_refs = refs[9:12]
    vp_refs = refs[12:15]
    o_ref = refs[15]
    og_scr, lse_scr = refs[16], refs[17]
    n = pl.program_id(0)

    row = lax.broadcasted_iota(jnp.int32, (ATT_QB, 2 * ATT_QB), 0)
    col = lax.broadcasted_iota(jnp.int32, (ATT_QB, 2 * ATT_QB), 1)
    band = (col >= row) & (col <= row + ATT_QB)
    band_first = band & (col >= jnp.where(n > 0, 0, ATT_QB))

    for g, (_, dil) in enumerate(ATT_GROUPS):
        q_ref, kc_ref, kp_ref, vc_ref, vp_ref = q_refs[g], kc_refs[g], kp_refs[g], vc_refs[g], vp_refs[g]
        nq = ATT_SB // (dil * ATT_QB)

        def rows(start, count):
            if dil == 1:
                return pl.ds(pl.multiple_of(start, ATT_QB), count)
            return pl.ds(start, count, stride=dil)

        def put(g_, tok0, o, lse):
            og_scr[g_, rows(tok0, ATT_QB), :] = o
            lse_scr[g_, rows(tok0, ATT_QB), :] = jnp.broadcast_to(lse, (ATT_QB, HEAD_DIM))

        def first_body(r, carry):
            o, lse = _attn_tile(q_ref[rows(r, ATT_QB), :],
                                kp_ref[rows(r, ATT_QB), :], kc_ref[rows(r, ATT_QB), :],
                                vp_ref[rows(r, ATT_QB), :], vc_ref[rows(r, ATT_QB), :],
                                band_first)
            put(g, r, o, lse)
            return carry

        lax.fori_loop(0, dil, first_body, 0, unroll=_unroll_for(dil, 16))

        if nq > 1:
            def rest_body(t, carry):
                r = t // (nq - 1)
                u = t % (nq - 1) + 1
                q0 = r + dil * (u * ATT_QB)
                k0 = r + dil * ((u - 1) * ATT_QB)
                k1 = r + dil * (u * ATT_QB)
                o, lse = _attn_tile(q_ref[rows(q0, ATT_QB), :],
                                    kc_ref[rows(k0, ATT_QB), :], kc_ref[rows(k1, ATT_QB), :],
                                    vc_ref[rows(k0, ATT_QB), :], vc_ref[rows(k1, ATT_QB), :],
                                    band)
                put(g, q0, o, lse)
                return carry

            lax.fori_loop(0, dil * (nq - 1), rest_body, 0, unroll=_unroll_for(dil * (nq - 1), 16))

    def merge_body(c, carry):
        rs = pl.ds(pl.multiple_of(c * 256, 256), 256)
        l0, l1, l2 = lse_scr[0, rs, :], lse_scr[1, rs, :], lse_scr[2, rs, :]
        m = jnp.maximum(jnp.maximum(l0, l1), l2)
        e0, e1, e2 = jnp.exp(l0 - m), jnp.exp(l1 - m), jnp.exp(l2 - m)
        num = e0 * og_scr[0, rs, :] + e1 * og_scr[1, rs, :] + e2 * og_scr[2, rs, :]
        o_ref[rs, :] = (num / (e0 + e1 + e2)).astype(o_ref.dtype)
        return carry

    lax.fori_loop(0, ATT_SB // 256, merge_body, 0)


def _attention_prompt(z_qk, z_rest):
    t = z_qk.shape[0]
    nsb = t // ATT_SB
    in_specs, args = [], []

    def cur(colblk):
        return pl.BlockSpec((ATT_SB, HEAD_DIM), lambda n, h, c=colblk: (n, c + h))

    def prev(colblk, dil):
        rows_p = dil * ATT_QB
        per = ATT_SB // rows_p
        return pl.BlockSpec((rows_p, HEAD_DIM),
                            lambda n, h, c=colblk, per=per: (jnp.maximum(n * per - 1, 0), c + h))

    for g in range(3):
        in_specs.append(cur(OFF_QA // HEAD_DIM + g * H_G)); args.append(z_qk)
    for z, base in ((z_qk, OFF_KA), (z_rest, R_VA)):
        for g in range(3):
            in_specs.append(cur(base // HEAD_DIM + g * H_G)); args.append(z)
        for g, (_, dil) in enumerate(ATT_GROUPS):
            in_specs.append(prev(base // HEAD_DIM + g * H_G, dil)); args.append(z)
    blk = ATT_SB * HEAD_DIM * 4
    prev_rows = sum(d * ATT_QB for _, d in ATT_GROUPS)
    vm = 2 * (9 * blk + 2 * prev_rows * HEAD_DIM * 4) + 2 * blk + 6 * blk + 8 * 1024 * 1024
    return pl.pallas_call(
        _attn_kernel,
        out_shape=jax.ShapeDtypeStruct((t, ATT_OUT), BF16),
        grid=(nsb, H_G),
        in_specs=in_specs,
        out_specs=pl.BlockSpec((ATT_SB, HEAD_DIM), lambda n, h: (n, h)),
        scratch_shapes=[pltpu.VMEM((3, ATT_SB, HEAD_DIM), F32),
                        pltpu.VMEM((3, ATT_SB, HEAD_DIM), F32)],
        compiler_params=pltpu.CompilerParams(
            dimension_semantics=("arbitrary", "arbitrary"),
            vmem_limit_bytes=_vmem_limit(vm)),
        name="attn_prompt",
    )(*args)


def _attn_dec_kernel(q_ref, kn_ref, vn_ref, c0_ref, c1_ref, c2_ref, o_ref, n0_ref, n1_ref, n2_ref):
    caches = (c0_ref, c1_ref, c2_ref)
    news = (n0_ref, n1_ref, n2_ref)
    rows_per_pos = 2 * H_G
    zpad = jnp.zeros((LANES - DEC_PAD, HEAD_DIM), F32)

    for g in range(len(ATT_GROUPS)):
        c_ref, n_ref = caches[g], news[g]
        n_rows = c_ref.shape[1]
        keep = n_rows - DEC_SEQ * rows_per_pos
        n_ref[0, 0:keep, :] = c_ref[0, DEC_SEQ * rows_per_pos:n_rows, :]
        for t in range(DEC_SEQ):
            for h in range(H_G):
                hs = slice((g * H_G + h) * HEAD_DIM, (g * H_G + h + 1) * HEAD_DIM)
                r = keep + t * rows_per_pos + h
                n_ref[0, r:r + 1, :] = kn_ref[0, t:t + 1, hs]
                n_ref[0, r + H_G:r + H_G + 1, :] = vn_ref[0, t:t + 1, hs]

    for h in range(H_G):
        outs, lses = [], []
        for g, (win, dil) in enumerate(ATT_GROUPS):
            c_ref = caches[g]
            p_len = c_ref.shape[1] // rows_per_pos
            hs = slice((g * H_G + h) * HEAD_DIM, (g * H_G + h + 1) * HEAD_DIM)
            q = (q_ref[0, :, hs] * ATT_SCALE).astype(BF16)
            k = jnp.concatenate([c_ref[0, pl.ds(h, p_len, stride=rows_per_pos), :], kn_ref[0, :, hs], zpad],
                                axis=0).astype(BF16)
            v = jnp.concatenate([c_ref[0, pl.ds(H_G + h, p_len, stride=rows_per_pos), :],
                                 vn_ref[0, :, hs], zpad], axis=0).astype(BF16)
            s = lax.dot_general(q, k, (((1,), (1,)), ((), ())), preferred_element_type=F32)
            tq = lax.broadcasted_iota(jnp.int32, s.shape, 0) & (DEC_SEQ - 1)
            col = lax.broadcasted_iota(jnp.int32, s.shape, 1)
            delta = p_len + tq - col
            valid = ((delta >= 0) & ((delta & (dil - 1)) == 0) & (delta <= win)
                     & (col < p_len + DEC_SEQ))
            s = jnp.where(valid, s, -jnp.inf)
            m = jnp.max(s, axis=1, keepdims=True)
            p = jnp.exp(s - m)
            l = jnp.sum(p, axis=1, keepdims=True)
            outs.append(jnp.dot(p.astype(BF16), v, preferred_element_type=F32) / l)
            lses.append(m + jnp.log(l))
        m = jnp.maximum(jnp.maximum(lses[0], lses[1]), lses[2])
        e = [jnp.exp(x - m) for x in lses]
        num = e[0] * outs[0] + e[1] * outs[1] + e[2] * outs[2]
        o_ref[0, :, h * HEAD_DIM:(h + 1) * HEAD_DIM] = num / (e[0] + e[1] + e[2])


def _attention_decode(zqk8, zrest8, caches):
    b = zqk8.shape[0]
    qspec = lambda c: pl.BlockSpec((1, DEC_PAD, ATT_W), lambda i, c=c: (i, 0, c))
    cspecs = [pl.BlockSpec((1, c.shape[1], c.shape[2]), lambda i: (i, 0, 0)) for c in caches]
    vm = 4 * sum(c.shape[1] * c.shape[2] * 4 for c in caches) + 12 * 1024 * 1024
    return pl.pallas_call(
        _attn_dec_kernel,
        out_shape=[jax.ShapeDtypeStruct((b, DEC_PAD, ATT_OUT), F32)]
        + [jax.ShapeDtypeStruct(c.shape, F32) for c in caches],
        grid=(b,),
        in_specs=[qspec(OFF_QA // ATT_W), qspec(OFF_KA // ATT_W), qspec(R_VA // ATT_W)] + cspecs,
        out_specs=[pl.BlockSpec((1, DEC_PAD, ATT_OUT), lambda i: (i, 0, 0))] + cspecs,
        compiler_params=pltpu.CompilerParams(
            dimension_semantics=("arbitrary",),
            vmem_limit_bytes=_vmem_limit(vm)),
        name="attn_decode",
    )(zqk8, zqk8, zrest8, *caches)


def _hgrn_consts(c):
    t = np.arange(c)[:, None]
    u = np.arange(c)[None, :]
    seg = [(u <= t)]
    cross = []
    m = c // 2
    while m >= 1:
        blk_t, off_t = t // (2 * m), t % (2 * m)
        piv = blk_t * 2 * m + m - 1
        upper = off_t >= m
        if 1 < m < SUBLANES:
            seg.append(np.where(upper, (u > piv) & (u <= t), (u > t) & (u <= piv)))
        cross.append((t // (2 * m) == u // (2 * m)) & (t % (2 * m) >= m) & (u % (2 * m) < m))
        m //= 2
    cross.append(t == u)
    return (np.stack(seg).astype(np.float32).reshape(-1, c),
            np.stack(cross).astype(np.float32))


def _hgrn_level_weights(b, dsum, fg, c):
    out = []
    m = c // 2
    low = 1
    while m >= 1:
        if m >= SUBLANES:
            parts = []
            for k0 in range(0, c, 2 * m):
                piv = b[k0 + m - 1:k0 + m, :]
                parts += [piv - b[k0:k0 + m, :], b[k0 + m:k0 + 2 * m, :] - piv]
            out.append(jnp.exp2(jnp.concatenate(parts, axis=0)))
        elif m > 1:
            out.append(jnp.exp2(dsum[low * c:(low + 1) * c]))
            low += 1
        else:
            odd = (lax.broadcasted_iota(jnp.int32, fg.shape, 0) & 1) == 1
            out.append(jnp.where(odd, fg, 1.0))
        m //= 2
    return out


def _hgrn_kernel(q_ref, f_ref, i_ref, g_ref, lb_ref, nw_ref, seg_ref, cross_ref, o_ref, s_out_ref, s_scr,
                 *, tb, n_lvl):
    n = pl.program_id(1)
    c = HG_CHUNK

    @pl.when(n == 0)
    def _():
        s_scr[...] = jnp.zeros_like(s_scr)

    lb = lb_ref[...]
    mx = jnp.maximum(lb[0:1, :], lb[1:2, :])
    e0, e1 = jnp.exp(lb[0:1, :] - mx), jnp.exp(lb[1:2, :] - mx)
    lower = e0 / (e0 + e1)
    nw = nw_ref[...]
    seg = seg_ref[...]

    chunks = range(tb // c)
    nt = (((1,), (1,)), ((), ()))
    qs, ks, vs, ws, w_pre, w_suf = [], [], [], [], [], []
    for ch in chunks:
        rs = slice(ch * c, (ch + 1) * c)
        fg = lower + (1.0 - lower) * jax.nn.sigmoid(f_ref[rs, :])
        lg = jnp.log2(fg)
        lg_hi = lg.astype(BF16)
        lg_lo = (lg - lg_hi.astype(F32)).astype(BF16)
        dsum = (jnp.dot(seg, lg_hi, preferred_element_type=F32)
                + jnp.dot(seg, lg_lo, preferred_element_type=F32))
        b = dsum[0:c]
        ws.append(_hgrn_level_weights(b, dsum, fg, c))
        w_pre.append(jnp.exp2(b))
        w_suf.append(jnp.exp2(b[c - 1:c, :] - b))
        qs.append(_silu(q_ref[rs, :]))
        ks.append(1.0 - fg)
        vs.append(i_ref[rs, :].astype(BF16))
    qb = [qs[ch].astype(BF16) for ch in chunks]
    kb = [ks[ch].astype(BF16) for ch in chunks]
    atts = [cross_ref[n_lvl] * lax.dot_general(qb[ch], kb[ch], nt, preferred_element_type=F32)
            for ch in chunks]
    for lv in range(n_lvl):
        for ch in chunks:
            wl = ws[ch][lv].astype(BF16)
            atts[ch] = atts[ch] + cross_ref[lv] * lax.dot_general(
                qb[ch] * wl, kb[ch] * wl, nt, preferred_element_type=F32)
    o_intra = [jnp.dot(atts[ch].astype(BF16), vs[ch], preferred_element_type=F32) for ch in chunks]
    q_dec = [(qs[ch] * w_pre[ch]).astype(BF16) for ch in chunks]
    k_end_t = [(ks[ch] * w_suf[ch]).T.astype(BF16) for ch in chunks]
    decay_col = [jnp.broadcast_to(w_pre[ch][c - 1:c, :], (HG_DK, HG_DV)).T for ch in chunks]
    kv = [jnp.dot(k_end_t[ch], vs[ch], preferred_element_type=F32) for ch in chunks]
    s_cur = s_scr[...]
    for ch in chunks:
        rs = slice(ch * c, (ch + 1) * c)
        o = o_intra[ch] + jnp.dot(q_dec[ch], s_cur.astype(BF16), preferred_element_type=F32)
        s_cur = decay_col[ch] * s_cur + kv[ch]
        ms = jnp.mean(o * o, axis=-1, keepdims=True)
        o_ref[rs, :] = (o * lax.rsqrt(ms + EPS) * nw * _silu(g_ref[rs, :])).astype(o_ref.dtype)
    s_scr[...] = s_cur

    @pl.when(n == pl.num_programs(1) - 1)
    def _():
        s_out_ref[0] = s_scr[...]


def _hgrn_prompt(z, hg_lb, hg_norm_w, tb):
    t = z.shape[0]
    seg_np, cross_np = _hgrn_consts(HG_CHUNK)
    n_lvl = cross_np.shape[0] - 1
    seg = jnp.asarray(seg_np, dtype=BF16)
    cross = jnp.asarray(cross_np, dtype=F32)
    col = lambda off: pl.BlockSpec((tb, HG_DK), lambda h, n, o=off // HG_DK: (n, o + h))
    kern = functools.partial(_hgrn_kernel, tb=tb, n_lvl=n_lvl)
    vm = 2 * 5 * tb * HG_DK * 4 + 2 * seg_np.size * 2 + 2 * cross_np.size * 4 + 16 * 1024 * 1024
    return pl.pallas_call(
        kern,
        out_shape=(jax.ShapeDtypeStruct((t, HG_WV), BF16),
                   jax.ShapeDtypeStruct((HG_HEADS, HG_DK, HG_DV), F32)),
        grid=(HG_HEADS, t // tb),
        in_specs=[col(R_QH), col(R_FH), col(R_IH), col(R_GH),
                  pl.BlockSpec((2, HG_DK), lambda h, n: (0, h)),
                  pl.BlockSpec((1, HG_DV), lambda h, n: (0, 0)),
                  pl.BlockSpec(seg_np.shape, lambda h, n: (0, 0)),
                  pl.BlockSpec(cross_np.shape, lambda h, n: (0, 0, 0))],
        out_specs=(pl.BlockSpec((tb, HG_DV), lambda h, n: (n, h)),
                   pl.BlockSpec((1, HG_DK, HG_DV), lambda h, n: (h, 0, 0))),
        scratch_shapes=[pltpu.VMEM((HG_DK, HG_DV), F32)],
        compiler_params=pltpu.CompilerParams(
            dimension_semantics=("arbitrary", "arbitrary"),
            vmem_limit_bytes=_vmem_limit(vm)),
        name="hgrn_prompt",
    )(z, z, z, z, hg_lb, hg_norm_w, seg, cross)


def _hgrn_dec_kernel(q_ref, f_ref, i_ref, g_ref, lb_ref, nw_ref, s_ref, o_ref, s_out_ref, *, heads):
    lb = lb_ref[...]
    mx = jnp.maximum(lb[0:1, :], lb[1:2, :])
    e0, e1 = jnp.exp(lb[0:1, :] - mx), jnp.exp(lb[1:2, :] - mx)
    lower = e0 / (e0 + e1)
    nw = nw_ref[...]
    zpad = jnp.zeros((HG_DK - DEC_PAD, HG_DK), F32)
    row = lax.broadcasted_iota(jnp.int32, (DEC_PAD, HG_DV), 0)

    def before(x, d):
        return pltpu.roll(x, d, 0)

    def after(x, d):
        return pltpu.roll(x, DEC_PAD - d, 0)

    def split(x):
        hi = x.astype(BF16)
        return hi, (x - hi.astype(F32)).astype(BF16)

    def dot3(a, b):
        (ah, al), (bh, bl) = split(a), split(b)
        return (jnp.dot(ah, bh, preferred_element_type=F32) + jnp.dot(ah, bl, preferred_element_type=F32)
                + jnp.dot(al, bh, preferred_element_type=F32))

    for h in range(heads):
        hs = slice(h * HG_DK, (h + 1) * HG_DK)
        q = _silu(q_ref[0, :, hs])
        fg = lower[:, hs] + (1.0 - lower[:, hs]) * jax.nn.sigmoid(f_ref[0, :, hs])
        k = 1.0 - fg
        v = i_ref[0, :, hs]
        s0 = s_ref[0, h]
        p = [jnp.ones_like(fg), fg]
        for d in range(2, DEC_SEQ + 1):
            p.append(p[d - 1] * before(fg, d - 1))
        f_cum = p[DEC_SEQ]
        for t in range(DEC_SEQ - 1):
            f_cum = jnp.where(row == t, p[t + 1], f_cum)
        g_suf = jnp.ones_like(fg)
        run = jnp.ones_like(fg)
        for d in range(1, DEC_SEQ):
            run = run * after(fg, d)
            g_suf = jnp.where(row == DEC_SEQ - 1 - d, run, g_suf)
        o = dot3(q * f_cum, s0)
        for d in range(DEC_SEQ):
            coef = jnp.sum(q * p[d] * (before(k, d) if d else k), axis=1, keepdims=True)
            o = o + jnp.where(row >= d, coef, 0.0) * (before(v, d) if d else v)
        kd = jnp.where(row < DEC_SEQ, k * g_suf, 0.0)
        kd_t = jnp.concatenate([kd, zpad], axis=0).T
        v_pad = jnp.concatenate([v, zpad], axis=0)
        decay_col = jnp.broadcast_to(f_cum[DEC_SEQ - 1:DEC_SEQ, :], (HG_DK, HG_DV)).T
        s_out_ref[0, h] = decay_col * s0 + dot3(kd_t, v_pad)
        ms = jnp.mean(o * o, axis=-1, keepdims=True)
        o_ref[0, :, hs] = o * lax.rsqrt(ms + EPS) * nw * _silu(g_ref[0, :, hs])


def _hgrn_decode(z8, hg_lb, hg_norm_w, state):
    b = z8.shape[0]
    heads = COL_BLK // HG_DK
    nhb = HG_HEADS // heads
    col = lambda off: pl.BlockSpec((1, DEC_PAD, COL_BLK), lambda i, j, o=off // COL_BLK: (i, 0, o + j))
    kern = functools.partial(_hgrn_dec_kernel, heads=heads)
    sspec = pl.BlockSpec((1, heads, HG_DK, HG_DV), lambda i, j: (i, j, 0, 0))
    return pl.pallas_call(
        kern,
        out_shape=(jax.ShapeDtypeStruct((b, DEC_PAD, HG_WV), F32),
                   jax.ShapeDtypeStruct(state.shape, F32)),
        grid=(b, nhb),
        in_specs=[col(R_QH), col(R_FH), col(R_IH), col(R_GH),
                  pl.BlockSpec((2, COL_BLK), lambda i, j: (0, j)),
                  pl.BlockSpec((1, HG_DV), lambda i, j: (0, 0)),
                  sspec],
        out_specs=(pl.BlockSpec((1, DEC_PAD, COL_BLK), lambda i, j: (i, 0, j)), sspec),
        compiler_params=pltpu.CompilerParams(
            dimension_semantics=("arbitrary", "arbitrary"),
            vmem_limit_bytes=_vmem_limit(0)),
        name="hgrn_decode",
    )(z8, z8, z8, z8, hg_lb, hg_norm_w, state)


def _merge_kernel(att_ref, hg_ref, ga_ref, gb_ref, x_ref, g1_ref, wpa_ref, wpb_ref, wo_ref, o_ref):
    att, hg = att_ref[...], hg_ref[...]
    parts = []
    for c0 in range(0, wpa_ref.shape[1], MXU_COLS):
        cs = slice(c0, c0 + MXU_COLS)
        ya = jnp.dot(att, wpa_ref[:, cs], preferred_element_type=F32)
        yb = jnp.dot(hg, wpb_ref[:, cs], preferred_element_type=F32)
        parts.append((ga_ref[:, cs].astype(F32) * ya + gb_ref[:, cs].astype(F32) * yb).astype(BF16))
    ymix = jnp.concatenate(parts, axis=1)
    o_ref[...] = x_ref[...] + g1_ref[...] * jnp.dot(ymix, wo_ref[...], preferred_element_type=F32)


def _merge(att, hg, gates, x2d, g1, wpa_bf, wpb_bf, wo_bf, tm):
    rows, d = x2d.shape
    per_row = g1.shape[0] != 1
    resident = pl.Buffered(1)
    g1_spec = (pl.BlockSpec((tm, d), lambda i: (i, 0)) if per_row else pl.BlockSpec((1, d), lambda i: (0, 0)))
    vm = (2 * tm * (ATT_OUT + HG_WV + 2 * d) * 2 + 4 * tm * d * 4 + (ATT_OUT * d + HG_WV * d + d * d) * 2
          + (2 * tm * d * 4 if per_row else 0) + tm * d * 10 + 4 * 1024 * 1024)
    return pl.pallas_call(
        _merge_kernel,
        out_shape=jax.ShapeDtypeStruct((rows, d), F32),
        grid=(rows // tm,),
        in_specs=[pl.BlockSpec((tm, ATT_OUT), lambda i: (i, 0)),
                  pl.BlockSpec((tm, HG_WV), lambda i: (i, 0)),
                  pl.BlockSpec((tm, d), lambda i: (i, 0)),
                  pl.BlockSpec((tm, d), lambda i: (i, 1)),
                  pl.BlockSpec((tm, d), lambda i: (i, 0)),
                  g1_spec,
                  pl.BlockSpec((ATT_OUT, d), lambda i: (0, 0), pipeline_mode=resident),
                  pl.BlockSpec((HG_WV, d), lambda i: (0, 0), pipeline_mode=resident),
                  pl.BlockSpec((d, d), lambda i: (0, 0), pipeline_mode=resident)],
        out_specs=pl.BlockSpec((tm, d), lambda i: (i, 0)),
        compiler_params=pltpu.CompilerParams(
            dimension_semantics=("arbitrary",),
            vmem_limit_bytes=_vmem_limit(vm)),
        name="merge",
    )(att, hg, gates, gates, x2d, g1, wpa_bf, wpb_bf, wo_bf)


def _ffn_kernel(x_ref, sh_ref, sc_ref, g2_ref, nw_ref, nf_ref, wa_ref, wb_ref, wd_ref, cw_ref, cb_ref,
                p1_ref, p2_ref, y_ref, tail_ref, *rest, tm, ta, per_row, seq_rows, emit):
    wa_out, wb_out, wd_out = rest[:3] if emit else (None, None, None)
    h_scr, acc_scr, carry_scr = rest[-3:]
    wa_ref = _bf16_weight(wa_ref, wa_out)
    wb_ref = _bf16_weight(wb_ref, wb_out)
    wd_ref = _bf16_weight(wd_ref, wd_out)
    i = pl.program_id(0)
    f = pl.program_id(1)

    if seq_rows is None:
        @pl.when(i == 0)
        def _():
            carry_scr[f] = p1_ref[...]

    tf = wa_ref.shape[1]
    groups = [slice(c0, c0 + MXU_COLS) for c0 in range(0, tf, MXU_COLS)]
    last_f = pl.num_programs(1) - 1

    def ffn_rows(rc, first, last):
        row = lax.broadcasted_iota(jnp.int32, (rc, MXU_COLS), 0)
        if seq_rows is None:
            tails = [carry_scr[f, :, cs] for cs in groups]
        for r0 in range(0, tm, rc):
            rs = slice(r0, r0 + rc)
            if first:
                x = x_ref[rs, :]
                ms = jnp.mean(x * x, axis=-1, keepdims=True)
                xn = x * lax.rsqrt(ms + EPS) * nw_ref[...]
                sc, sh = (sc_ref[rs, :], sh_ref[rs, :]) if per_row else (sc_ref[...], sh_ref[...])
                h = (xn * (1.0 + sc) + sh).astype(BF16)
                h_scr[rs, :] = h
            else:
                h = h_scr[rs, :]
            a_parts = [jnp.dot(h, wa_ref[:, cs], preferred_element_type=F32) for cs in groups]
            b_parts = [jnp.dot(h, wb_ref[:, cs], preferred_element_type=F32) for cs in groups]
            ys = []
            for gi, (cs, a, b) in enumerate(zip(groups, a_parts, b_parts)):
                r1 = pltpu.roll(a, 1, 0)
                r2 = pltpu.roll(a, 2, 0)
                if seq_rows is None:
                    prev = tails[gi]
                    a1 = jnp.where(row == 0, prev[SUBLANES - 1:SUBLANES, :], r1)
                    a2 = jnp.where(row == 0, prev[SUBLANES - 2:SUBLANES - 1, :],
                                   jnp.where(row == 1, prev[SUBLANES - 1:SUBLANES, :], r2))
                    tails[gi] = a[rc - SUBLANES:, :]
                else:
                    t = row & (seq_rows - 1)
                    a1 = jnp.where(t == 0, p1_ref[rs, cs], r1)
                    a2 = jnp.where(t <= 1, p2_ref[rs, cs], r2)
                    tail_ref[rs, cs] = a
                cw = cw_ref[:, cs]
                u = cb_ref[:, cs] + a2 * cw[0:1, :] + a1 * cw[1:2, :] + a * cw[2:3, :]
                ys.append((_silu(u) * b).astype(BF16))
            down = jnp.dot(jnp.concatenate(ys, axis=1), wd_ref[...], preferred_element_type=F32)
            if first:
                acc_scr[rs, :] = down
            elif not last:
                acc_scr[rs, :] += down
            else:
                g2 = g2_ref[rs, :] if per_row else g2_ref[...]
                x2 = x_ref[rs, :] + g2 * (acc_scr[rs, :] + down)
                ms = jnp.mean(x2 * x2, axis=-1, keepdims=True)
                y_ref[rs, :] = x2 * lax.rsqrt(ms + EPS) * nf_ref[...]
        if seq_rows is None:
            for cs, tail in zip(groups, tails):
                carry_scr[f, :, cs] = tail
                tail_ref[:, cs] = tail

    @pl.when(f == 0)
    def _():
        ffn_rows(min(tm, FFN_EDGE_ROWS), True, False)

    @pl.when((f > 0) & (f < last_f))
    def _():
        ffn_rows(min(tm, FFN_ROWS), False, False)

    @pl.when(f == last_f)
    def _():
        ffn_rows(min(tm, FFN_EDGE_ROWS), False, True)


def _ffn(x1, sh, sc, g2, nw, nf, wa, wb, wd, conv_w, conv_b, p1, p2, tm, seq_rows):
    rows, d = x1.shape
    dff = wa.shape[1]
    per_row = sh.shape[0] != 1
    emit = wa.dtype == F32
    assert not emit or rows == tm
    tf = MXU_COLS if emit else COL_BLK
    nf_blk = dff // tf
    ta = SUBLANES if seq_rows is None else tm
    prow = SUBLANES if seq_rows is None else tm
    kern = functools.partial(_ffn_kernel, tm=tm, ta=ta, per_row=per_row, seq_rows=seq_rows, emit=emit)
    pspec = pl.BlockSpec((prow, tf), lambda i, f: (0 if seq_rows is None else i, f))
    w_specs = [pl.BlockSpec((d, tf), lambda i, f: (0, f)),
               pl.BlockSpec((d, tf), lambda i, f: (0, f)),
               pl.BlockSpec((tf, d), lambda i, f: (f, 0))]
    vm = (4 * tm * d * 4 + tm * d * 2 + tm * d * 4 + 3 * d * tf * (14 if emit else 4) + 10 * tm * tf * 4
          + (6 * tm * d * 4 if per_row else 0) + 8 * 1024 * 1024)
    return pl.pallas_call(
        kern,
        out_shape=(jax.ShapeDtypeStruct((rows, d), F32),
                   jax.ShapeDtypeStruct((ta * (rows // tm), dff), F32))
        + ((jax.ShapeDtypeStruct(wa.shape, BF16), jax.ShapeDtypeStruct(wb.shape, BF16),
            jax.ShapeDtypeStruct(wd.shape, BF16)) if emit else ()),
        grid=(rows // tm, nf_blk),
        in_specs=[pl.BlockSpec((tm, d), lambda i, f: (i, 0)),
                  _mod_spec(per_row, tm, d), _mod_spec(per_row, tm, d), _mod_spec(per_row, tm, d),
                  pl.BlockSpec((1, d), lambda i, f: (0, 0)),
                  pl.BlockSpec((1, d), lambda i, f: (0, 0))]
        + w_specs
        + [pl.BlockSpec((CONV_W, tf), lambda i, f: (0, f)),
           pl.BlockSpec((1, tf), lambda i, f: (0, f)),
           pspec, pspec],
        out_specs=(pl.BlockSpec((tm, d), lambda i, f: (i, 0)),
                   pl.BlockSpec((ta, tf), lambda i, f: (i, f))) + (tuple(w_specs) if emit else ()),
        scratch_shapes=[pltpu.VMEM((tm, d), BF16), pltpu.VMEM((tm, d), F32),
                        pltpu.VMEM((nf_blk, SUBLANES, tf), F32)],
        compiler_params=pltpu.CompilerParams(
            dimension_semantics=("arbitrary", "arbitrary"),
            vmem_limit_bytes=_vmem_limit(vm)),
        name="ffn",
    )(x1, sh, sc, g2, nw, nf, wa, wb, wd, conv_w, conv_b, p1, p2)


def kernel(x_prompt, x_sample, cache_kv_w128, cache_kv_w512, cache_kv_w2048, state_hgrn, state_conv,
           c_prompt, c_sample, w_ada, b_ada, norm1_w, w_in, hg_lb, hg_norm_w, w_pa, w_pb, w_o,
           norm2_w, w_ffn_a, w_ffn_b, conv_w, conv_b, w_ffn_down, norm_f_w):
    d = D_MODEL
    bp, t, _ = x_prompt.shape
    bs, ts, _ = x_sample.shape
    assert bp == 1 and ts == DEC_SEQ and w_ada.shape[0] == 1
    rows_s = bs * ts

    wpa_bf, wpb_bf, wo_bf = w_pa[0].astype(BF16), w_pb[0].astype(BF16), w_o[0].astype(BF16)
    nw1, nw2, nwf = norm1_w[0].reshape(1, d), norm2_w[0].reshape(1, d), norm_f_w.reshape(1, d)
    hg_nw = hg_norm_w[0].reshape(1, HG_DV)
    cw, cb = conv_w[0], conv_b[0].reshape(1, D_FF)

    n_seq = bp + bs
    pad_rows = -n_seq % SUBLANES
    c_all = jnp.concatenate([c_prompt, c_sample, jnp.zeros((pad_rows, d), F32)], axis=0)
    mod = _ada(c_all, w_ada[0], b_ada[0])
    mod_p = [mod[0:bp, k * d:(k + 1) * d] for k in range(N_MOD)]
    mod_s = [jnp.repeat(mod[bp:n_seq, k * d:(k + 1) * d], ts, axis=0) for k in range(N_MOD)]

    xs = x_sample.reshape(rows_s, d)
    rope_s = _rope_tables(PAST_LEN + np.arange(rows_s) % ts)
    hs = _norm_mod(xs, mod_s[0], mod_s[1], nw1, tm=rows_s)
    (zqk_s, zr_s, gates_s), w_in_bf = _inproj(hs, w_in[0], rope_s, tm=rows_s)
    pad8 = lambda z: jnp.pad(z.reshape(bs, ts, z.shape[1]), ((0, 0), (0, DEC_PAD - ts), (0, 0)))
    zqk_s8, zr_s8 = pad8(zqk_s), pad8(zr_s)
    cache_in = (cache_kv_w128, cache_kv_w512, cache_kv_w2048)
    caches = [c.reshape(bs, c.shape[2] * 2 * H_G, HEAD_DIM) for c in cache_in]
    att_s8, *new_caches = _attention_decode(zqk_s8, zr_s8, caches)
    att_s = att_s8[:, :ts].reshape(rows_s, ATT_OUT).astype(BF16)
    kv_s = [n.reshape(c.shape) for n, c in zip(new_caches, cache_in)]
    hg_s8, s_s = _hgrn_decode(zr_s8, hg_lb, hg_nw, state_hgrn[0])
    hg_s = hg_s8[:, :ts].reshape(rows_s, HG_WV).astype(BF16)
    x1s = _merge(att_s, hg_s, gates_s, xs, mod_s[2], wpa_bf, wpb_bf, wo_bf, tm=rows_s)
    buf = state_conv[0]
    zrow = jnp.zeros((bs, 1, D_FF), F32)
    p1 = jnp.concatenate([buf[:, 1:2], zrow, zrow, zrow], axis=1).reshape(rows_s, D_FF)
    p2 = jnp.concatenate([buf[:, 0:1], buf[:, 1:2], zrow, zrow], axis=1).reshape(rows_s, D_FF)
    ys, a_s, wa_bf, wb_bf, wd_bf = _ffn(x1s, mod_s[3], mod_s[4], mod_s[5], nw2, nwf,
                                        w_ffn_a[0], w_ffn_b[0], w_ffn_down[0], cw, cb,
                                        p1, p2, tm=rows_s, seq_rows=ts)

    xp = x_prompt.reshape(t, d)
    rope_p = _rope_tables(np.arange(t))
    hp = _norm_mod(xp, mod_p[0], mod_p[1], nw1, tm=1024)
    (zqk_p, zr_p, gates_p), _ = _inproj(hp, w_in_bf, rope_p, tm=2048)
    att_p = _attention_prompt(zqk_p, zr_p)
    hg_p, s_p = _hgrn_prompt(zr_p, hg_lb, hg_nw, tb=2048)
    x1p = _merge(att_p, hg_p, gates_p, xp, mod_p[2], wpa_bf, wpb_bf, wo_bf, tm=512)
    conv0 = jnp.zeros((SUBLANES, D_FF), F32)
    yp, tail_p = _ffn(x1p, mod_p[3], mod_p[4], mod_p[5], nw2, nwf, wa_bf, wb_bf, wd_bf, cw, cb,
                      conv0, conv0, tm=512, seq_rows=None)

    kv_p = []
    for g, (win, _) in enumerate(ATT_GROUPS):
        keep = min(win, t)
        ks = zqk_p[t - keep:, OFF_KA + g * ATT_OUT:OFF_KA + (g + 1) * ATT_OUT]
        vs = zr_p[t - keep:, R_VA + g * ATT_OUT:R_VA + (g + 1) * ATT_OUT]
        kv_p.append(jnp.stack([ks.reshape(keep, H_G, HEAD_DIM), vs.reshape(keep, H_G, HEAD_DIM)],
                              axis=1)[None, None])
    hgrn_p = s_p[None, None]
    conv_p = tail_p[tail_p.shape[0] - (CONV_W - 1):][None, None]

    hgrn_s = s_s[None]
    conv_s = a_s.reshape(bs, ts, D_FF)[:, ts - (CONV_W - 1):][None]

    return (yp.reshape(bp, t, d), ys.reshape(bs, ts, d),
            kv_p[0], kv_p[1], kv_p[2], hgrn_p, conv_p,
            kv_s[0], kv_s[1], kv_s[2], hgrn_s, conv_s)
```

```python
import functools

import numpy as np
import jax
import jax.numpy as jnp
from jax import lax
from jax.experimental import pallas as pl
from jax.experimental.pallas import tpu as pltpu

F32 = jnp.float32
BF16 = jnp.bfloat16

D_MODEL = 2048
SEQ = 16384
DEC_BATCH = 32
DEC_SEQ = 4
PAST_LEN = 16384

HEAD_DIM = 128
ATT_GROUPS = ((128, 1), (512, 4), (2048, 16))
H_G = 4
N_ATT_HEADS = H_G * len(ATT_GROUPS)
ATT_W = N_ATT_HEADS * HEAD_DIM
ATT_OUT = H_G * HEAD_DIM
ROT_DIM = HEAD_DIM // 4
ROPE_THETA = 500000.0
ATT_SCALE = HEAD_DIM ** -0.5

HG_HEADS = 8
HG_DK = 128
HG_DV = 128
HG_WK = HG_HEADS * HG_DK
HG_WV = HG_HEADS * HG_DV

D_FF = 5632
CONV_W = 3
N_MOD = 6
EPS = 1e-6

IN_SIZES = (ATT_W, ATT_W, ATT_W, HG_WK, HG_WK, HG_WV, HG_WV, D_MODEL, D_MODEL)
IN_TOTAL = sum(IN_SIZES)
IN_OFFS = tuple(int(s) for s in np.cumsum((0,) + IN_SIZES)[:-1])
OFF_QA, OFF_KA, OFF_VA, OFF_QH, OFF_FH, OFF_IH, OFF_GH, OFF_GA, OFF_GB = IN_OFFS
QK_COLS = 2 * ATT_W
R_VA, R_QH, R_FH, R_IH, R_GH, R_GA, R_GB = (o - QK_COLS for o in IN_OFFS[2:])

V7X_VMEM_BYTES = 64 * 1024 * 1024
SUBLANES = 8
LANES = 128
DEC_PAD = SUBLANES

COL_BLK = 512
MXU_COLS = 256
ROPE_ROWS = 256
FFN_ROWS = 512
FFN_EDGE_ROWS = 256
HG_CHUNK = 128
ATT_QB = 128
ATT_SB = 2048

TM_NORM = 1024
TM_PROJ = 2048
TM_MERGE = 512
TM_FFN = 512
TB_HGRN = 2048


def _vmem_limit(nbytes):
    return int(min(V7X_VMEM_BYTES - 8 * 1024 * 1024, max(nbytes, 16 * 1024 * 1024)))


def _silu(x):
    return x * jax.nn.sigmoid(x)


def _unroll_for(trips, max_unroll=6):
    return next(u for u in range(min(max_unroll, trips), 0, -1) if trips % u == 0)


def _ada_kernel(c_ref, w_ref, b_ref, o_ref):
    c = c_ref[...]
    s = _silu(c).astype(BF16)
    o_ref[...] = jnp.dot(s, w_ref[...].astype(BF16), preferred_element_type=F32) + b_ref[...]


def _ada(c_all, w_ada, b_ada):
    rows, d = c_all.shape
    n = w_ada.shape[1]
    tn = 1024
    return pl.pallas_call(
        _ada_kernel,
        out_shape=jax.ShapeDtypeStruct((rows, n), F32),
        grid=(n // tn,),
        in_specs=[pl.BlockSpec((rows, d), lambda j: (0, 0)),
                  pl.BlockSpec((d, tn), lambda j: (0, j)),
                  pl.BlockSpec((1, tn), lambda j: (0, j))],
        out_specs=pl.BlockSpec((rows, tn), lambda j: (0, j)),
        compiler_params=pltpu.CompilerParams(
            dimension_semantics=("arbitrary",),
            vmem_limit_bytes=_vmem_limit(2 * d * tn * 4 + d * tn * 2 + 4 * rows * (d + tn) * 4)),
        name="ada",
    )(c_all, w_ada, b_ada.reshape(1, n))


def _norm_mod_rows(x_ref, nw_ref, sc_ref, sh_ref, h_ref, tm, per_row):
    step = min(tm, 128)

    def body(r, carry):
        rs = pl.ds(pl.multiple_of(r * step, step), step)
        x = x_ref[rs, :]
        ms = jnp.mean(x * x, axis=-1, keepdims=True)
        y = x * lax.rsqrt(ms + EPS) * nw_ref[...]
        if per_row:
            h = y * (1.0 + sc_ref[rs, :]) + sh_ref[rs, :]
        else:
            h = y * (1.0 + sc_ref[...]) + sh_ref[...]
        h_ref[rs, :] = h.astype(BF16)
        return carry

    lax.fori_loop(0, tm // step, body, 0)


def _mod_spec(per_row, tm, d):
    if per_row:
        return pl.BlockSpec((tm, d), lambda i, j: (i, 0))
    return pl.BlockSpec((1, d), lambda i, j: (0, 0))


def _norm_kernel(x_ref, sh_ref, sc_ref, nw_ref, h_ref, *, tm, per_row):
    _norm_mod_rows(x_ref, nw_ref, sc_ref, sh_ref, h_ref, tm, per_row)


def _norm_mod(x2d, sh, sc, nw, tm):
    rows, d = x2d.shape
    per_row = sh.shape[0] != 1
    mod = (pl.BlockSpec((tm, d), lambda i: (i, 0)) if per_row else pl.BlockSpec((1, d), lambda i: (0, 0)))
    return pl.pallas_call(
        functools.partial(_norm_kernel, tm=tm, per_row=per_row),
        out_shape=jax.ShapeDtypeStruct((rows, d), BF16),
        grid=(rows // tm,),
        in_specs=[pl.BlockSpec((tm, d), lambda i: (i, 0)), mod, mod,
                  pl.BlockSpec((1, d), lambda i: (0, 0))],
        out_specs=pl.BlockSpec((tm, d), lambda i: (i, 0)),
        compiler_params=pltpu.CompilerParams(
            dimension_semantics=("arbitrary",),
            vmem_limit_bytes=_vmem_limit(2 * tm * d * 6 + (4 * tm * d * 4 if per_row else 0) + 8 * 1024 * 1024)),
        name="norm_mod",
    )(x2d, sh, sc, nw)


def _bf16_weight(w_ref, wb_ref):
    if wb_ref is None:
        return w_ref
    w = w_ref[...].astype(BF16)
    wb_ref[...] = w
    return w


def _proj_rope_kernel(h_ref, w_ref, cos_ref, s1_ref, s2_ref, z_ref, wb_ref=None):
    w = _bf16_weight(w_ref, wb_ref)
    tm = h_ref.shape[0]
    rstep = min(tm, ROPE_ROWS)
    for r0 in range(0, tm, rstep):
        rs = slice(r0, r0 + rstep)
        h = h_ref[rs, :]
        c, s1, s2 = cos_ref[rs, :], s1_ref[rs, :], s2_ref[rs, :]
        for c0 in range(0, z_ref.shape[1], MXU_COLS):
            grp = jnp.dot(h, w[:, c0:c0 + MXU_COLS], preferred_element_type=F32)
            for hh in range(MXU_COLS // HEAD_DIM):
                blk = grp[:, hh * HEAD_DIM:(hh + 1) * HEAD_DIM]
                z_ref[rs, c0 + hh * HEAD_DIM:c0 + (hh + 1) * HEAD_DIM] = (
                    blk * c + pltpu.roll(blk, HEAD_DIM - ROT_DIM // 2, 1) * s1
                    + pltpu.roll(blk, ROT_DIM // 2, 1) * s2)


def _proj_plain_kernel(h_ref, w_ref, z_ref, wb_ref=None):
    w = _bf16_weight(w_ref, wb_ref)
    z_ref[...] = jnp.dot(h_ref[...], w[...], preferred_element_type=F32)


def _proj_gate_kernel(h_ref, w_ref, z_ref, wb_ref=None):
    w = _bf16_weight(w_ref, wb_ref)
    tm = h_ref.shape[0]
    rstep = min(tm, ROPE_ROWS)
    for r0 in range(0, tm, rstep):
        rs = slice(r0, r0 + rstep)
        h = h_ref[rs, :]
        for c0 in range(0, z_ref.shape[1], MXU_COLS):
            cs = slice(c0, c0 + MXU_COLS)
            z_ref[rs, cs] = jax.nn.sigmoid(jnp.dot(h, w[:, cs], preferred_element_type=F32)).astype(BF16)


def _inproj(h, w_in, rope, tm):
    rows, d = h.shape
    emit = not isinstance(w_in, tuple)
    assert not emit or rows == tm

    def call(kern, name, idx, col0, ncols, tn, out_dtype, tables=()):
        w = w_in if emit else w_in[idx]
        vm = (2 * tm * d * 2 + 2 * d * tn * jnp.dtype(w.dtype).itemsize + (3 * d * tn * 2 if emit else 0)
              + 2 * len(tables) * tm * HEAD_DIM * 4 + 2 * tm * tn * jnp.dtype(out_dtype).itemsize
              + 2 * tm * min(tn, COL_BLK) * 4 + 4 * 1024 * 1024)
        tab = pl.BlockSpec((tm, HEAD_DIM), lambda i, j: (i, 0))
        out_shape = [jax.ShapeDtypeStruct((rows, ncols), out_dtype)]
        out_specs = [pl.BlockSpec((tm, tn), lambda i, j: (i, j))]
        if emit:
            out_shape.append(jax.ShapeDtypeStruct((d, ncols), BF16))
            out_specs.append(pl.BlockSpec((d, tn), lambda i, j: (0, j)))
        outs = pl.pallas_call(
            kern,
            out_shape=out_shape,
            grid=(rows // tm, ncols // tn),
            in_specs=[pl.BlockSpec((tm, d), lambda i, j: (i, 0)),
                      pl.BlockSpec((d, tn), lambda i, j, c=(col0 // tn if emit else 0): (0, c + j))]
            + [tab] * len(tables),
            out_specs=out_specs,
            compiler_params=pltpu.CompilerParams(dimension_semantics=("arbitrary", "arbitrary"),
                                                 vmem_limit_bytes=_vmem_limit(vm)),
            name=name,
        )(h, w, *tables)
        return outs if emit else (outs[0], None)

    z_qk, w0 = call(_proj_rope_kernel, "proj_rope", 0, 0, QK_COLS, 2 * COL_BLK, F32, rope)
    z_rest, w1 = call(_proj_plain_kernel, "proj_plain", 1, QK_COLS, R_GA, COL_BLK, F32)
    gates, w2 = call(_proj_gate_kernel, "proj_gate", 2, QK_COLS + R_GA, 2 * D_MODEL, COL_BLK, BF16)
    return (z_qk, z_rest, gates), ((w0, w1, w2) if emit else None)


def _rope_tables(pos):
    half = ROT_DIM // 2
    inv_freq = ROPE_THETA ** (-np.arange(half, dtype=np.float64) * 2.0 / ROT_DIM)
    ang = np.asarray(pos, dtype=np.float64)[:, None] * inv_freq[None, :]
    cos, sin = jnp.asarray(np.cos(ang), dtype=F32), jnp.asarray(np.sin(ang), dtype=F32)
    rows = pos.shape[0]
    ones = jnp.ones((rows, HEAD_DIM - ROT_DIM), F32)
    zer = jnp.zeros((rows, HEAD_DIM - ROT_DIM), F32)
    zh = jnp.zeros((rows, half), F32)
    c = jnp.concatenate([cos, cos, ones], axis=1)
    s1 = jnp.concatenate([-sin, zh, zer], axis=1)
    s2 = jnp.concatenate([zh, sin, zer], axis=1)
    return c, s1, s2


def _attn_tile(q, ka, kb, va, vb, valid):
    k = jnp.concatenate([ka, kb], axis=0).astype(BF16)
    v = jnp.concatenate([va, vb], axis=0).astype(BF16)
    s = lax.dot_general((q * ATT_SCALE).astype(BF16), k, (((1,), (1,)), ((), ())),
                        preferred_element_type=F32)
    s = jnp.where(valid, s, -jnp.inf)
    m = jnp.max(s, axis=1, keepdims=True)
    p = jnp.exp(s - m)
    l = jnp.sum(p, axis=1, keepdims=True)
    o = jnp.dot(p.astype(BF16), v, preferred_element_type=F32) / l
    return o, m + jnp.log(l)


def _attn_kernel(*refs):
    q_refs = refs[0:3]
    kc_refs = refs[3:6]
    kp_refs = refs[6:9]
    vc_refs = refs[9:12]
    vp_refs = refs[12:15]
    o_ref = refs[15]
    og_scr, lse_scr = refs[16], refs[17]
    n = pl.program_id(0)

    row = lax.broadcasted_iota(jnp.int32, (ATT_QB, 2 * ATT_QB), 0)
    col = lax.broadcasted_iota(jnp.int32, (ATT_QB, 2 * ATT_QB), 1)
    band = (col >= row) & (col <= row + ATT_QB)
    band_first = band & (col >= jnp.where(n > 0, 0, ATT_QB))

    for g, (_, dil) in enumerate(ATT_GROUPS):
        q_ref, kc_ref, kp_ref, vc_ref, vp_ref = q_refs[g], kc_refs[g], kp_refs[g], vc_refs[g], vp_refs[g]
        nq = ATT_SB // (dil * ATT_QB)

        def rows(start, count):
            if dil == 1:
                return pl.ds(pl.multiple_of(start, ATT_QB), count)
            return pl.ds(start, count, stride=dil)

        def put(g_, tok0, o, lse):
            og_scr[g_, rows(tok0, ATT_QB), :] = o
            lse_scr[g_, rows(tok0, ATT_QB), :] = jnp.broadcast_to(lse, (ATT_QB, HEAD_DIM))

        def first_body(r, carry):
            o, lse = _attn_tile(q_ref[rows(r, ATT_QB), :],
                                kp_ref[rows(r, ATT_QB), :], kc_ref[rows(r, ATT_QB), :],
                                vp_ref[rows(r, ATT_QB), :], vc_ref[rows(r, ATT_QB), :],
                                band_first)
            put(g, r, o, lse)
            return carry

        lax.fori_loop(0, dil, first_body, 0, unroll=_unroll_for(dil, 16))

        if nq > 1:
            def rest_body(t, carry):
                r = t // (nq - 1)
                u = t % (nq - 1) + 1
                q0 = r + dil * (u * ATT_QB)
                k0 = r + dil * ((u - 1) * ATT_QB)
                k1 = r + dil * (u * ATT_QB)
                o, lse = _attn_tile(q_ref[rows(q0, ATT_QB), :],
                                    kc_ref[rows(k0, ATT_QB), :], kc_ref[rows(k1, ATT_QB), :],
                                    vc_ref[rows(k0, ATT_QB), :], vc_ref[rows(k1, ATT_QB), :],
                                    band)
                put(g, q0, o, lse)
                return carry

            lax.fori_loop(0, dil * (nq - 1), rest_body, 0, unroll=_unroll_for(dil * (nq - 1), 16))

    def merge_body(c, carry):
        rs = pl.ds(pl.multiple_of(c * 256, 256), 256)
        l0, l1, l2 = lse_scr[0, rs, :], lse_scr[1, rs, :], lse_scr[2, rs, :]
        m = jnp.maximum(jnp.maximum(l0, l1), l2)
        e0, e1, e2 = jnp.exp(l0 - m), jnp.exp(l1 - m), jnp.exp(l2 - m)
        num = e0 * og_scr[0, rs, :] + e1 * og_scr[1, rs, :] + e2 * og_scr[2, rs, :]
        o_ref[rs, :] = (num / (e0 + e1 + e2)).astype(o_ref.dtype)
        return carry

    lax.fori_loop(0, ATT_SB // 256, merge_body, 0)


def _attention_prompt(z_qk, z_rest):
    t = z_qk.shape[0]
    nsb = t // ATT_SB
    in_specs, args = [], []

    def cur(colblk):
        return pl.BlockSpec((ATT_SB, HEAD_DIM), lambda n, h, c=colblk: (n, c + h))

    def prev(colblk, dil):
        rows_p = dil * ATT_QB
        per = ATT_SB // rows_p
        return pl.BlockSpec((rows_p, HEAD_DIM),
                            lambda n, h, c=colblk, per=per: (jnp.maximum(n * per - 1, 0), c + h))

    for g in range(3):
        in_specs.append(cur(OFF_QA // HEAD_DIM + g * H_G)); args.append(z_qk)
    for z, base in ((z_qk, OFF_KA), (z_rest, R_VA)):
        for g in range(3):
            in_specs.append(cur(base // HEAD_DIM + g * H_G)); args.append(z)
        for g, (_, dil) in enumerate(ATT_GROUPS):
            in_specs.append(prev(base // HEAD_DIM + g * H_G, dil)); args.append(z)
    blk = ATT_SB * HEAD_DIM * 4
    prev_rows = sum(d * ATT_QB for _, d in ATT_GROUPS)
    vm = 2 * (9 * blk + 2 * prev_rows * HEAD_DIM * 4) + 2 * blk + 6 * blk + 8 * 1024 * 1024
    return pl.pallas_call(
        _attn_kernel,
        out_shape=jax.ShapeDtypeStruct((t, ATT_OUT), BF16),
        grid=(nsb, H_G),
        in_specs=in_specs,
        out_specs=pl.BlockSpec((ATT_SB, HEAD_DIM), lambda n, h: (n, h)),
        scratch_shapes=[pltpu.VMEM((3, ATT_SB, HEAD_DIM), F32),
                        pltpu.VMEM((3, ATT_SB, HEAD_DIM), F32)],
        compiler_params=pltpu.CompilerParams(
            dimension_semantics=("arbitrary", "arbitrary"),
            vmem_limit_bytes=_vmem_limit(vm)),
        name="attn_prompt",
    )(*args)


def _attn_dec_kernel(q_ref, kn_ref, vn_ref, c0_ref, c1_ref, c2_ref, o_ref, n0_ref, n1_ref, n2_ref):
    caches = (c0_ref, c1_ref, c2_ref)
    news = (n0_ref, n1_ref, n2_ref)
    rows_per_pos = 2 * H_G
    zpad = jnp.zeros((LANES - DEC_PAD, HEAD_DIM), F32)

    for g in range(len(ATT_GROUPS)):
        c_ref, n_ref = caches[g], news[g]
        n_rows = c_ref.shape[1]
        keep = n_rows - DEC_SEQ * rows_per_pos
        n_ref[0, 0:keep, :] = c_ref[0, DEC_SEQ * rows_per_pos:n_rows, :]
        for t in range(DEC_SEQ):
            for h in range(H_G):
                hs = slice((g * H_G + h) * HEAD_DIM, (g * H_G + h + 1) * HEAD_DIM)
                r = keep + t * rows_per_pos + h
                n_ref[0, r:r + 1, :] = kn_ref[0, t:t + 1, hs]
                n_ref[0, r + H_G:r + H_G + 1, :] = vn_ref[0, t:t + 1, hs]

    for h in range(H_G):
        outs, lses = [], []
        for g, (win, dil) in enumerate(ATT_GROUPS):
            c_ref = caches[g]
            p_len = c_ref.shape[1] // rows_per_pos
            hs = slice((g * H_G + h) * HEAD_DIM, (g * H_G + h + 1) * HEAD_DIM)
            q = (q_ref[0, :, hs] * ATT_SCALE).astype(BF16)
            k = jnp.concatenate([c_ref[0, pl.ds(h, p_len, stride=rows_per_pos), :], kn_ref[0, :, hs], zpad],
                                axis=0).astype(BF16)
            v = jnp.concatenate([c_ref[0, pl.ds(H_G + h, p_len, stride=rows_per_pos), :],
                                 vn_ref[0, :, hs], zpad], axis=0).astype(BF16)
            s = lax.dot_general(q, k, (((1,), (1,)), ((), ())), preferred_element_type=F32)
            tq = lax.broadcasted_iota(jnp.int32, s.shape, 0) & (DEC_SEQ - 1)
            col = lax.broadcasted_iota(jnp.int32, s.shape, 1)
            delta = p_len + tq - col
            valid = ((delta >= 0) & ((delta & (dil - 1)) == 0) & (delta <= win)
                     & (col < p_len + DEC_SEQ))
            s = jnp.where(valid, s, -jnp.inf)
            m = jnp.max(s, axis=1, keepdims=True)
            p = jnp.exp(s - m)
            l = jnp.sum(p, axis=1, keepdims=True)
            outs.append(jnp.dot(p.astype(BF16), v, preferred_element_type=F32) / l)
            lses.append(m + jnp.log(l))
        m = jnp.maximum(jnp.maximum(lses[0], lses[1]), lses[2])
        e = [jnp.exp(x - m) for x in lses]
        num = e[0] * outs[0] + e[1] * outs[1] + e[2] * outs[2]
        o_ref[0, :, h * HEAD_DIM:(h + 1) * HEAD_DIM] = num / (e[0] + e[1] + e[2])


def _attention_decode(zqk8, zrest8, caches):
    b = zqk8.shape[0]
    qspec = lambda c: pl.BlockSpec((1, DEC_PAD, ATT_W), lambda i, c=c: (i, 0, c))
    cspecs = [pl.BlockSpec((1, c.shape[1], c.shape[2]), lambda i: (i, 0, 0)) for c in caches]
    vm = 4 * sum(c.shape[1] * c.shape[2] * 4 for c in caches) + 12 * 1024 * 1024
    return pl.pallas_call(
        _attn_dec_kernel,
        out_shape=[jax.ShapeDtypeStruct((b, DEC_PAD, ATT_OUT), F32)]
        + [jax.ShapeDtypeStruct(c.shape, F32) for c in caches],
        grid=(b,),
        in_specs=[qspec(OFF_QA // ATT_W), qspec(OFF_KA // ATT_W), qspec(R_VA // ATT_W)] + cspecs,
        out_specs=[pl.BlockSpec((1, DEC_PAD, ATT_OUT), lambda i: (i, 0, 0))] + cspecs,
        compiler_params=pltpu.CompilerParams(
            dimension_semantics=("arbitrary",),
            vmem_limit_bytes=_vmem_limit(vm)),
        name="attn_decode",
    )(zqk8, zqk8, zrest8, *caches)


def _hgrn_consts(c):
    t = np.arange(c)[:, None]
    u = np.arange(c)[None, :]
    seg = [(u <= t)]
    cross = []
    m = c // 2
    while m >= 1:
        blk_t, off_t = t // (2 * m), t % (2 * m)
        piv = blk_t * 2 * m + m - 1
        upper = off_t >= m
        if 1 < m < SUBLANES:
            seg.append(np.where(upper, (u > piv) & (u <= t), (u > t) & (u <= piv)))
        cross.append((t // (2 * m) == u // (2 * m)) & (t % (2 * m) >= m) & (u % (2 * m) < m))
        m //= 2
    cross.append(t == u)
    return (np.stack(seg).astype(np.float32).reshape(-1, c),
            np.stack(cross).astype(np.float32))


def _hgrn_level_weights(b, dsum, fg, c):
    out = []
    m = c // 2
    low = 1
    while m >= 1:
        if m >= SUBLANES:
            parts = []
            for k0 in range(0, c, 2 * m):
                piv = b[k0 + m - 1:k0 + m, :]
                parts += [piv - b[k0:k0 + m, :], b[k0 + m:k0 + 2 * m, :] - piv]
            out.append(jnp.exp2(jnp.concatenate(parts, axis=0)))
        elif m > 1:
            out.append(jnp.exp2(dsum[low * c:(low + 1) * c]))
            low += 1
        else:
            odd = (lax.broadcasted_iota(jnp.int32, fg.shape, 0) & 1) == 1
            out.append(jnp.where(odd, fg, 1.0))
        m //= 2
    return out


def _hgrn_kernel(q_ref, f_ref, i_ref, g_ref, lb_ref, nw_ref, seg_ref, cross_ref, o_ref, s_out_ref, s_scr,
                 *, tb, n_lvl):
    n = pl.program_id(1)
    c = HG_CHUNK

    @pl.when(n == 0)
    def _():
        s_scr[...] = jnp.zeros_like(s_scr)

    lb = lb_ref[...]
    mx = jnp.maximum(lb[0:1, :], lb[1:2, :])
    e0, e1 = jnp.exp(lb[0:1, :] - mx), jnp.exp(lb[1:2, :] - mx)
    lower = e0 / (e0 + e1)
    nw = nw_ref[...]
    seg = seg_ref[...]

    chunks = range(tb // c)
    nt = (((1,), (1,)), ((), ()))
    qs, ks, vs, ws, w_pre, w_suf = [], [], [], [], [], []
    for ch in chunks:
        rs = slice(ch * c, (ch + 1) * c)
        fg = lower + (1.0 - lower) * jax.nn.sigmoid(f_ref[rs, :])
        lg = jnp.log2(fg)
        lg_hi = lg.astype(BF16)
        lg_lo = (lg - lg_hi.astype(F32)).astype(BF16)
        dsum = (jnp.dot(seg, lg_hi, preferred_element_type=F32)
                + jnp.dot(seg, lg_lo, preferred_element_type=F32))
        b = dsum[0:c]
        ws.append(_hgrn_level_weights(b, dsum, fg, c))
        w_pre.append(jnp.exp2(b))
        w_suf.append(jnp.exp2(b[c - 1:c, :] - b))
        qs.append(_silu(q_ref[rs, :]))
        ks.append(1.0 - fg)
        vs.append(i_ref[rs, :].astype(BF16))
    qb = [qs[ch].astype(BF16) for ch in chunks]
    kb = [ks[ch].astype(BF16) for ch in chunks]
    atts = [cross_ref[n_lvl] * lax.dot_general(qb[ch], kb[ch], nt, preferred_element_type=F32)
            for ch in chunks]
    for lv in range(n_lvl):
        for ch in chunks:
            wl = ws[ch][lv].astype(BF16)
            atts[ch] = atts[ch] + cross_ref[lv] * lax.dot_general(
                qb[ch] * wl, kb[ch] * wl, nt, preferred_element_type=F32)
    o_intra = [jnp.dot(atts[ch].astype(BF16), vs[ch], preferred_element_type=F32) for ch in chunks]
    q_dec = [(qs[ch] * w_pre[ch]).astype(BF16) for ch in chunks]
    k_end_t = [(ks[ch] * w_suf[ch]).T.astype(BF16) for ch in chunks]
    decay_col = [jnp.broadcast_to(w_pre[ch][c - 1:c, :], (HG_DK, HG_DV)).T for ch in chunks]
    kv = [jnp.dot(k_end_t[ch], vs[ch], preferred_element_type=F32) for ch in chunks]
    s_cur = s_scr[...]
    for ch in chunks:
        rs = slice(ch * c, (ch + 1) * c)
        o = o_intra[ch] + jnp.dot(q_dec[ch], s_cur.astype(BF16), preferred_element_type=F32)
        s_cur = decay_col[ch] * s_cur + kv[ch]
        ms = jnp.mean(o * o, axis=-1, keepdims=True)
        o_ref[rs, :] = (o * lax.rsqrt(ms + EPS) * nw * _silu(g_ref[rs, :])).astype(o_ref.dtype)
    s_scr[...] = s_cur

    @pl.when(n == pl.num_programs(1) - 1)
    def _():
        s_out_ref[0] = s_scr[...]


def _hgrn_prompt(z, hg_lb, hg_norm_w, tb):
    t = z.shape[0]
    seg_np, cross_np = _hgrn_consts(HG_CHUNK)
    n_lvl = cross_np.shape[0] - 1
    seg = jnp.asarray(seg_np, dtype=BF16)
    cross = jnp.asarray(cross_np, dtype=F32)
    col = lambda off: pl.BlockSpec((tb, HG_DK), lambda h, n, o=off // HG_DK: (n, o + h))
    kern = functools.partial(_hgrn_kernel, tb=tb, n_lvl=n_lvl)
    vm = 2 * 5 * tb * HG_DK * 4 + 2 * seg_np.size * 2 + 2 * cross_np.size * 4 + 16 * 1024 * 1024
    return pl.pallas_call(
        kern,
        out_shape=(jax.ShapeDtypeStruct((t, HG_WV), BF16),
                   jax.ShapeDtypeStruct((HG_HEADS, HG_DK, HG_DV), F32)),
        grid=(HG_HEADS, t // tb),
        in_specs=[col(R_QH), col(R_FH), col(R_IH), col(R_GH),
                  pl.BlockSpec((2, HG_DK), lambda h, n: (0, h)),
                  pl.BlockSpec((1, HG_DV), lambda h, n: (0, 0)),
                  pl.BlockSpec(seg_np.shape, lambda h, n: (0, 0)),
                  pl.BlockSpec(cross_np.shape, lambda h, n: (0, 0, 0))],
        out_specs=(pl.BlockSpec((tb, HG_DV), lambda h, n: (n, h)),
                   pl.BlockSpec((1, HG_DK, HG_DV), lambda h, n: (h, 0, 0))),
        scratch_shapes=[pltpu.VMEM((HG_DK, HG_DV), F32)],
        compiler_params=pltpu.CompilerParams(
            dimension_semantics=("arbitrary", "arbitrary"),
            vmem_limit_bytes=_vmem_limit(vm)),
        name="hgrn_prompt",
    )(z, z, z, z, hg_lb, hg_norm_w, seg, cross)


def _hgrn_dec_kernel(q_ref, f_ref, i_ref, g_ref, lb_ref, nw_ref, s_ref, o_ref, s_out_ref, *, heads):
    lb = lb_ref[...]
    mx = jnp.maximum(lb[0:1, :], lb[1:2, :])
    e0, e1 = jnp.exp(lb[0:1, :] - mx), jnp.exp(lb[1:2, :] - mx)
    lower = e0 / (e0 + e1)
    nw = nw_ref[...]
    zpad = jnp.zeros((HG_DK - DEC_PAD, HG_DK), F32)
    row = lax.broadcasted_iota(jnp.int32, (DEC_PAD, HG_DV), 0)

    def before(x, d):
        return pltpu.roll(x, d, 0)

    def after(x, d):
        return pltpu.roll(x, DEC_PAD - d, 0)

    def split(x):
        hi = x.astype(BF16)
        return hi, (x - hi.astype(F32)).astype(BF16)

    def dot3(a, b):
        (ah, al), (bh, bl) = split(a), split(b)
        return (jnp.dot(ah, bh, preferred_element_type=F32) + jnp.dot(ah, bl, preferred_element_type=F32)
                + jnp.dot(al, bh, preferred_element_type=F32))

    for h in range(heads):
        hs = slice(h * HG_DK, (h + 1) * HG_DK)
        q = _silu(q_ref[0, :, hs])
        fg = lower[:, hs] + (1.0 - lower[:, hs]) * jax.nn.sigmoid(f_ref[0, :, hs])
        k = 1.0 - fg
        v = i_ref[0, :, hs]
        s0 = s_ref[0, h]
        p = [jnp.ones_like(fg), fg]
        for d in range(2, DEC_SEQ + 1):
            p.append(p[d - 1] * before(fg, d - 1))
        f_cum = p[DEC_SEQ]
        for t in range(DEC_SEQ - 1):
            f_cum = jnp.where(row == t, p[t + 1], f_cum)
        g_suf = jnp.ones_like(fg)
        run = jnp.ones_like(fg)
        for d in range(1, DEC_SEQ):
            run = run * after(fg, d)
            g_suf = jnp.where(row == DEC_SEQ - 1 - d, run, g_suf)
        o = dot3(q * f_cum, s0)
        for d in range(DEC_SEQ):
            coef = jnp.sum(q * p[d] * (before(k, d) if d else k), axis=1, keepdims=True)
            o = o + jnp.where(row >= d, coef, 0.0) * (before(v, d) if d else v)
        kd = jnp.where(row < DEC_SEQ, k * g_suf, 0.0)
        kd_t = jnp.concatenate([kd, zpad], axis=0).T
        v_pad = jnp.concatenate([v, zpad], axis=0)
        decay_col = jnp.broadcast_to(f_cum[DEC_SEQ - 1:DEC_SEQ, :], (HG_DK, HG_DV)).T
        s_out_ref[0, h] = decay_col * s0 + dot3(kd_t, v_pad)
        ms = jnp.mean(o * o, axis=-1, keepdims=True)
        o_ref[0, :, hs] = o * lax.rsqrt(ms + EPS) * nw * _silu(g_ref[0, :, hs])


def _hgrn_decode(z8, hg_lb, hg_norm_w, state):
    b = z8.shape[0]
    heads = COL_BLK // HG_DK
    nhb = HG_HEADS // heads
    col = lambda off: pl.BlockSpec((1, DEC_PAD, COL_BLK), lambda i, j, o=off // COL_BLK: (i, 0, o + j))
    kern = functools.partial(_hgrn_dec_kernel, heads=heads)
    sspec = pl.BlockSpec((1, heads, HG_DK, HG_DV), lambda i, j: (i, j, 0, 0))
    return pl.pallas_call(
        kern,
        out_shape=(jax.ShapeDtypeStruct((b, DEC_PAD, HG_WV), F32),
                   jax.ShapeDtypeStruct(state.shape, F32)),
        grid=(b, nhb),
        in_specs=[col(R_QH), col(R_FH), col(R_IH), col(R_GH),
                  pl.BlockSpec((2, COL_BLK), lambda i, j: (0, j)),
                  pl.BlockSpec((1, HG_DV), lambda i, j: (0, 0)),
                  sspec],
        out_specs=(pl.BlockSpec((1, DEC_PAD, COL_BLK), lambda i, j: (i, 0, j)), sspec),
        compiler_params=pltpu.CompilerParams(
            dimension_semantics=("arbitrary", "arbitrary"),
            vmem_limit_bytes=_vmem_limit(0)),
        name="hgrn_decode",
    )(z8, z8, z8, z8, hg_lb, hg_norm_w, state)


def _merge_kernel(att_ref, hg_ref, ga_ref, gb_ref, x_ref, g1_ref, wpa_ref, wpb_ref, wo_ref, o_ref):
    att, hg = att_ref[...], hg_ref[...]
    parts = []
    for c0 in range(0, wpa_ref.shape[1], MXU_COLS):
        cs = slice(c0, c0 + MXU_COLS)
        ya = jnp.dot(att, wpa_ref[:, cs], preferred_element_type=F32)
        yb = jnp.dot(hg, wpb_ref[:, cs], preferred_element_type=F32)
        parts.append((ga_ref[:, cs].astype(F32) * ya + gb_ref[:, cs].astype(F32) * yb).astype(BF16))
    ymix = jnp.concatenate(parts, axis=1)
    o_ref[...] = x_ref[...] + g1_ref[...] * jnp.dot(ymix, wo_ref[...], preferred_element_type=F32)


def _merge(att, hg, gates, x2d, g1, wpa_bf, wpb_bf, wo_bf, tm):
    rows, d = x2d.shape
    per_row = g1.shape[0] != 1
    resident = pl.Buffered(1)
    g1_spec = (pl.BlockSpec((tm, d), lambda i: (i, 0)) if per_row else pl.BlockSpec((1, d), lambda i: (0, 0)))
    vm = (2 * tm * (ATT_OUT + HG_WV + 2 * d) * 2 + 4 * tm * d * 4 + (ATT_OUT * d + HG_WV * d + d * d) * 2
          + (2 * tm * d * 4 if per_row else 0) + tm * d * 10 + 4 * 1024 * 1024)
    return pl.pallas_call(
        _merge_kernel,
        out_shape=jax.ShapeDtypeStruct((rows, d), F32),
        grid=(rows // tm,),
        in_specs=[pl.BlockSpec((tm, ATT_OUT), lambda i: (i, 0)),
                  pl.BlockSpec((tm, HG_WV), lambda i: (i, 0)),
                  pl.BlockSpec((tm, d), lambda i: (i, 0)),
                  pl.BlockSpec((tm, d), lambda i: (i, 1)),
                  pl.BlockSpec((tm, d), lambda i: (i, 0)),
                  g1_spec,
                  pl.BlockSpec((ATT_OUT, d), lambda i: (0, 0), pipeline_mode=resident),
                  pl.BlockSpec((HG_WV, d), lambda i: (0, 0), pipeline_mode=resident),
                  pl.BlockSpec((d, d), lambda i: (0, 0), pipeline_mode=resident)],
        out_specs=pl.BlockSpec((tm, d), lambda i: (i, 0)),
        compiler_params=pltpu.CompilerParams(
            dimension_semantics=("arbitrary",),
            vmem_limit_bytes=_vmem_limit(vm)),
        name="merge",
    )(att, hg, gates, gates, x2d, g1, wpa_bf, wpb_bf, wo_bf)


def _ffn_kernel(x_ref, sh_ref, sc_ref, g2_ref, nw_ref, nf_ref, wa_ref, wb_ref, wd_ref, cw_ref, cb_ref,
                p1_ref, p2_ref, y_ref, tail_ref, *rest, tm, per_row, seq_rows, emit):
    wa_out, wb_out, wd_out = rest[:3] if emit else (None, None, None)
    h_scr, acc_scr, carry_scr = rest[-3:]
    wa_ref = _bf16_weight(wa_ref, wa_out)
    wb_ref = _bf16_weight(wb_ref, wb_out)
    wd_ref = _bf16_weight(wd_ref, wd_out)
    i = pl.program_id(0)
    f = pl.program_id(1)

    if seq_rows is None:
        @pl.when(i == 0)
        def _():
            carry_scr[f] = p1_ref[...]

    tf = wa_ref.shape[1]
    groups = [slice(c0, c0 + MXU_COLS) for c0 in range(0, tf, MXU_COLS)]
    last_f = pl.num_programs(1) - 1

    def ffn_rows(rc, first, last):
        row = lax.broadcasted_iota(jnp.int32, (rc, MXU_COLS), 0)
        if seq_rows is None:
            tails = [carry_scr[f, :, cs] for cs in groups]
        for r0 in range(0, tm, rc):
            rs = slice(r0, r0 + rc)
            if first:
                x = x_ref[rs, :]
                ms = jnp.mean(x * x, axis=-1, keepdims=True)
                xn = x * lax.rsqrt(ms + EPS) * nw_ref[...]
                sc, sh = (sc_ref[rs, :], sh_ref[rs, :]) if per_row else (sc_ref[...], sh_ref[...])
                h = (xn * (1.0 + sc) + sh).astype(BF16)
                h_scr[rs, :] = h
            else:
                h = h_scr[rs, :]
            a_parts = [jnp.dot(h, wa_ref[:, cs], preferred_element_type=F32) for cs in groups]
            b_parts = [jnp.dot(h, wb_ref[:, cs], preferred_element_type=F32) for cs in groups]
            ys = []
            for gi, (cs, a, b) in enumerate(zip(groups, a_parts, b_parts)):
                r1 = pltpu.roll(a, 1, 0)
                r2 = pltpu.roll(a, 2, 0)
                if seq_rows is None:
                    prev = tails[gi]
                    a1 = jnp.where(row == 0, prev[SUBLANES - 1:SUBLANES, :], r1)
                    a2 = jnp.where(row == 0, prev[SUBLANES - 2:SUBLANES - 1, :],
                                   jnp.where(row == 1, prev[SUBLANES - 1:SUBLANES, :], r2))
                    tails[gi] = a[rc - SUBLANES:, :]
                else:
                    t = row & (seq_rows - 1)
                    a1 = jnp.where(t == 0, p1_ref[rs, cs], r1)
                    a2 = jnp.where(t <= 1, p2_ref[rs, cs], r2)
                    tail_ref[rs, cs] = a
                cw = cw_ref[:, cs]
                u = cb_ref[:, cs] + a2 * cw[0:1, :] + a1 * cw[1:2, :] + a * cw[2:3, :]
                ys.append((_silu(u) * b).astype(BF16))
            down = jnp.dot(jnp.concatenate(ys, axis=1), wd_ref[...], preferred_element_type=F32)
            if first:
                acc_scr[rs, :] = down
            elif not last:
                acc_scr[rs, :] += down
            else:
                g2 = g2_ref[rs, :] if per_row else g2_ref[...]
                x2 = x_ref[rs, :] + g2 * (acc_scr[rs, :] + down)
                ms = jnp.mean(x2 * x2, axis=-1, keepdims=True)
                y_ref[rs, :] = x2 * lax.rsqrt(ms + EPS) * nf_ref[...]
        if seq_rows is None:
            for cs, tail in zip(groups, tails):
                carry_scr[f, :, cs] = tail
                tail_ref[:, cs] = tail

    @pl.when(f == 0)
    def _():
        ffn_rows(min(tm, FFN_EDGE_ROWS), True, False)

    @pl.when((f > 0) & (f < last_f))
    def _():
        ffn_rows(min(tm, FFN_ROWS), False, False)

    @pl.when(f == last_f)
    def _():
        ffn_rows(min(tm, FFN_EDGE_ROWS), False, True)


def _ffn(x1, sh, sc, g2, nw, nf, wa, wb, wd, conv_w, conv_b, p1, p2, tm, seq_rows):
    rows, d = x1.shape
    dff = wa.shape[1]
    per_row = sh.shape[0] != 1
    emit = wa.dtype == F32
    assert not emit or rows == tm
    tf = MXU_COLS if emit else COL_BLK
    nf_blk = dff // tf
    ta = SUBLANES if seq_rows is None else tm
    prow = SUBLANES if seq_rows is None else tm
    kern = functools.partial(_ffn_kernel, tm=tm, per_row=per_row, seq_rows=seq_rows, emit=emit)
    pspec = pl.BlockSpec((prow, tf), lambda i, f: (0 if seq_rows is None else i, f))
    w_specs = [pl.BlockSpec((d, tf), lambda i, f: (0, f)),
               pl.BlockSpec((d, tf), lambda i, f: (0, f)),
               pl.BlockSpec((tf, d), lambda i, f: (f, 0))]
    vm = (4 * tm * d * 4 + tm * d * 2 + tm * d * 4 + 3 * d * tf * (14 if emit else 4) + 10 * tm * tf * 4
          + (6 * tm * d * 4 if per_row else 0) + 8 * 1024 * 1024)
    return pl.pallas_call(
        kern,
        out_shape=(jax.ShapeDtypeStruct((rows, d), F32),
                   jax.ShapeDtypeStruct((ta * (rows // tm), dff), F32))
        + ((jax.ShapeDtypeStruct(wa.shape, BF16), jax.ShapeDtypeStruct(wb.shape, BF16),
            jax.ShapeDtypeStruct(wd.shape, BF16)) if emit else ()),
        grid=(rows // tm, nf_blk),
        in_specs=[pl.BlockSpec((tm, d), lambda i, f: (i, 0)),
                  _mod_spec(per_row, tm, d), _mod_spec(per_row, tm, d), _mod_spec(per_row, tm, d),
                  pl.BlockSpec((1, d), lambda i, f: (0, 0)),
                  pl.BlockSpec((1, d), lambda i, f: (0, 0))]
        + w_specs
        + [pl.BlockSpec((CONV_W, tf), lambda i, f: (0, f)),
           pl.BlockSpec((1, tf), lambda i, f: (0, f)),
           pspec, pspec],
        out_specs=(pl.BlockSpec((tm, d), lambda i, f: (i, 0)),
                   pl.BlockSpec((ta, tf), lambda i, f: (i, f))) + (tuple(w_specs) if emit else ()),
        scratch_shapes=[pltpu.VMEM((tm, d), BF16), pltpu.VMEM((tm, d), F32),
                        pltpu.VMEM((nf_blk, SUBLANES, tf), F32)],
        compiler_params=pltpu.CompilerParams(
            dimension_semantics=("arbitrary", "arbitrary"),
            vmem_limit_bytes=_vmem_limit(vm)),
        name="ffn",
    )(x1, sh, sc, g2, nw, nf, wa, wb, wd, conv_w, conv_b, p1, p2)


def kernel(x_prompt, x_sample, cache_kv_w128, cache_kv_w512, cache_kv_w2048, state_hgrn, state_conv,
           c_prompt, c_sample, w_ada, b_ada, norm1_w, w_in, hg_lb, hg_norm_w, w_pa, w_pb, w_o,
           norm2_w, w_ffn_a, w_ffn_b, conv_w, conv_b, w_ffn_down, norm_f_w):
    d = D_MODEL
    bp, t, _ = x_prompt.shape
    bs, ts, _ = x_sample.shape
    assert bp == 1 and ts == DEC_SEQ and w_ada.shape[0] == 1
    rows_s = bs * ts

    wpa_bf, wpb_bf, wo_bf = w_pa[0].astype(BF16), w_pb[0].astype(BF16), w_o[0].astype(BF16)
    nw1, nw2, nwf = norm1_w[0].reshape(1, d), norm2_w[0].reshape(1, d), norm_f_w.reshape(1, d)
    hg_nw = hg_norm_w[0].reshape(1, HG_DV)
    cw, cb = conv_w[0], conv_b[0].reshape(1, D_FF)

    n_seq = bp + bs
    pad_rows = -n_seq % SUBLANES
    c_all = jnp.concatenate([c_prompt, c_sample, jnp.zeros((pad_rows, d), F32)], axis=0)
    mod = _ada(c_all, w_ada[0], b_ada[0])
    mod_p = [mod[0:bp, k * d:(k + 1) * d] for k in range(N_MOD)]
    mod_s = [jnp.repeat(mod[bp:n_seq, k * d:(k + 1) * d], ts, axis=0) for k in range(N_MOD)]

    xs = x_sample.reshape(rows_s, d)
    rope_s = _rope_tables(PAST_LEN + np.arange(rows_s) % ts)
    hs = _norm_mod(xs, mod_s[0], mod_s[1], nw1, tm=rows_s)
    (zqk_s, zr_s, gates_s), w_in_bf = _inproj(hs, w_in[0], rope_s, tm=rows_s)
    pad8 = lambda z: jnp.pad(z.reshape(bs, ts, z.shape[1]), ((0, 0), (0, DEC_PAD - ts), (0, 0)))
    zqk_s8, zr_s8 = pad8(zqk_s), pad8(zr_s)
    cache_in = (cache_kv_w128, cache_kv_w512, cache_kv_w2048)
    caches = [c.reshape(bs, c.shape[2] * 2 * H_G, HEAD_DIM) for c in cache_in]
    att_s8, *new_caches = _attention_decode(zqk_s8, zr_s8, caches)
    att_s = att_s8[:, :ts].reshape(rows_s, ATT_OUT).astype(BF16)
    kv_s = [n.reshape(c.shape) for n, c in zip(new_caches, cache_in)]
    hg_s8, s_s = _hgrn_decode(zr_s8, hg_lb, hg_nw, state_hgrn[0])
    hg_s = hg_s8[:, :ts].reshape(rows_s, HG_WV).astype(BF16)
    x1s = _merge(att_s, hg_s, gates_s, xs, mod_s[2], wpa_bf, wpb_bf, wo_bf, tm=rows_s)
    buf = state_conv[0]
    zrow = jnp.zeros((bs, 1, D_FF), F32)
    p1 = jnp.concatenate([buf[:, 1:2], zrow, zrow, zrow], axis=1).reshape(rows_s, D_FF)
    p2 = jnp.concatenate([buf[:, 0:1], buf[:, 1:2], zrow, zrow], axis=1).reshape(rows_s, D_FF)
    ys, a_s, wa_bf, wb_bf, wd_bf = _ffn(x1s, mod_s[3], mod_s[4], mod_s[5], nw2, nwf,
                                        w_ffn_a[0], w_ffn_b[0], w_ffn_down[0], cw, cb,
                                        p1, p2, tm=rows_s, seq_rows=ts)

    xp = x_prompt.reshape(t, d)
    rope_p = _rope_tables(np.arange(t))
    hp = _norm_mod(xp, mod_p[0], mod_p[1], nw1, tm=TM_NORM)
    (zqk_p, zr_p, gates_p), _ = _inproj(hp, w_in_bf, rope_p, tm=TM_PROJ)
    att_p = _attention_prompt(zqk_p, zr_p)
    hg_p, s_p = _hgrn_prompt(zr_p, hg_lb, hg_nw, tb=TB_HGRN)
    x1p = _merge(att_p, hg_p, gates_p, xp, mod_p[2], wpa_bf, wpb_bf, wo_bf, tm=TM_MERGE)
    conv0 = jnp.zeros((SUBLANES, D_FF), F32)
    yp, tail_p = _ffn(x1p, mod_p[3], mod_p[4], mod_p[5], nw2, nwf, wa_bf, wb_bf, wd_bf, cw, cb,
                      conv0, conv0, tm=TM_FFN, seq_rows=None)

    kv_p = []
    for g, (win, _) in enumerate(ATT_GROUPS):
        keep = min(win, t)
        ks = zqk_p[t - keep:, OFF_KA + g * ATT_OUT:OFF_KA + (g + 1) * ATT_OUT]
        vs = zr_p[t - keep:, R_VA + g * ATT_OUT:R_VA + (g + 1) * ATT_OUT]
        kv_p.append(jnp.stack([ks.reshape(keep, H_G, HEAD_DIM), vs.reshape(keep, H_G, HEAD_DIM)],
                              axis=1)[None, None])
    hgrn_p = s_p[None, None]
    conv_p = tail_p[tail_p.shape[0] - (CONV_W - 1):][None, None]

    hgrn_s = s_s[None]
    conv_s = a_s.reshape(bs, ts, D_FF)[:, ts - (CONV_W - 1):][None]

    return (yp.reshape(bp, t, d), ys.reshape(bs, ts, d),
            kv_p[0], kv_p[1], kv_p[2], hgrn_p, conv_p,
            kv_s[0], kv_s[1], kv_s[2], hgrn_s, conv_s)
```

```python
import functools

import numpy as np
import jax
import jax.numpy as jnp
from jax import lax
from jax.experimental import pallas as pl
from jax.experimental.pallas import tpu as pltpu

F32 = jnp.float32
BF16 = jnp.bfloat16

D_MODEL = 2048
SEQ = 16384
DEC_BATCH = 32
DEC_SEQ = 4
PAST_LEN = 16384

HEAD_DIM = 128
ATT_GROUPS = ((128, 1), (512, 4), (2048, 16))
H_G = 4
N_ATT_HEADS = H_G * len(ATT_GROUPS)
ATT_W = N_ATT_HEADS * HEAD_DIM
ATT_OUT = H_G * HEAD_DIM
ROT_DIM = HEAD_DIM // 4
ROPE_THETA = 500000.0
ATT_SCALE = HEAD_DIM ** -0.5

HG_HEADS = 8
HG_DK = 128
HG_DV = 128
HG_WK = HG_HEADS * HG_DK
HG_WV = HG_HEADS * HG_DV

D_FF = 5632
CONV_W = 3
N_MOD = 6
EPS = 1e-6

IN_SIZES = (ATT_W, ATT_W, ATT_W, HG_WK, HG_WK, HG_WV, HG_WV, D_MODEL, D_MODEL)
IN_TOTAL = sum(IN_SIZES)
IN_OFFS = tuple(int(s) for s in np.cumsum((0,) + IN_SIZES)[:-1])
OFF_QA, OFF_KA, OFF_VA, OFF_QH, OFF_FH, OFF_IH, OFF_GH, OFF_GA, OFF_GB = IN_OFFS
QK_COLS = 2 * ATT_W
R_VA, R_QH, R_FH, R_IH, R_GH, R_GA, R_GB = (o - QK_COLS for o in IN_OFFS[2:])

V7X_VMEM_BYTES = 64 * 1024 * 1024
SUBLANES = 8
LANES = 128
DEC_PAD = SUBLANES

COL_BLK = 512
MXU_COLS = 256
ROPE_ROWS = 256
FFN_ROWS = 512
FFN_EDGE_ROWS = 256
HG_CHUNK = 128
ATT_QB = 128
ATT_SB = 2048

TM_NORM = 1024
TM_PROJ = 2048
TM_MERGE = 512
TM_FFN = 1024
TB_HGRN = 2048


def _vmem_limit(nbytes):
    return int(min(V7X_VMEM_BYTES - 8 * 1024 * 1024, max(nbytes, 16 * 1024 * 1024)))


def _silu(x):
    return x * jax.nn.sigmoid(x)


def _unroll_for(trips, max_unroll=6):
    return next(u for u in range(min(max_unroll, trips), 0, -1) if trips % u == 0)


def _ada_kernel(c_ref, w_ref, b_ref, o_ref):
    c = c_ref[...]
    s = _silu(c).astype(BF16)
    o_ref[...] = jnp.dot(s, w_ref[...].astype(BF16), preferred_element_type=F32) + b_ref[...]


def _ada(c_all, w_ada, b_ada):
    rows, d = c_all.shape
    n = w_ada.shape[1]
    tn = 1024
    return pl.pallas_call(
        _ada_kernel,
        out_shape=jax.ShapeDtypeStruct((rows, n), F32),
        grid=(n // tn,),
        in_specs=[pl.BlockSpec((rows, d), lambda j: (0, 0)),
                  pl.BlockSpec((d, tn), lambda j: (0, j)),
                  pl.BlockSpec((1, tn), lambda j: (0, j))],
        out_specs=pl.BlockSpec((rows, tn), lambda j: (0, j)),
        compiler_params=pltpu.CompilerParams(
            dimension_semantics=("arbitrary",),
            vmem_limit_bytes=_vmem_limit(2 * d * tn * 4 + d * tn * 2 + 4 * rows * (d + tn) * 4)),
        name="ada",
    )(c_all, w_ada, b_ada.reshape(1, n))


def _norm_mod_rows(x_ref, nw_ref, sc_ref, sh_ref, h_ref, tm, per_row):
    step = min(tm, 128)

    def body(r, carry):
        rs = pl.ds(pl.multiple_of(r * step, step), step)
        x = x_ref[rs, :]
        ms = jnp.mean(x * x, axis=-1, keepdims=True)
        y = x * lax.rsqrt(ms + EPS) * nw_ref[...]
        if per_row:
            h = y * (1.0 + sc_ref[rs, :]) + sh_ref[rs, :]
        else:
            h = y * (1.0 + sc_ref[...]) + sh_ref[...]
        h_ref[rs, :] = h.astype(BF16)
        return carry

    lax.fori_loop(0, tm // step, body, 0)


def _mod_spec(per_row, tm, d):
    if per_row:
        return pl.BlockSpec((tm, d), lambda i, j: (i, 0))
    return pl.BlockSpec((1, d), lambda i, j: (0, 0))


def _norm_kernel(x_ref, sh_ref, sc_ref, nw_ref, h_ref, *, tm, per_row):
    _norm_mod_rows(x_ref, nw_ref, sc_ref, sh_ref, h_ref, tm, per_row)


def _norm_mod(x2d, sh, sc, nw, tm):
    rows, d = x2d.shape
    per_row = sh.shape[0] != 1
    mod = (pl.BlockSpec((tm, d), lambda i: (i, 0)) if per_row else pl.BlockSpec((1, d), lambda i: (0, 0)))
    return pl.pallas_call(
        functools.partial(_norm_kernel, tm=tm, per_row=per_row),
        out_shape=jax.ShapeDtypeStruct((rows, d), BF16),
        grid=(rows // tm,),
        in_specs=[pl.BlockSpec((tm, d), lambda i: (i, 0)), mod, mod,
                  pl.BlockSpec((1, d), lambda i: (0, 0))],
        out_specs=pl.BlockSpec((tm, d), lambda i: (i, 0)),
        compiler_params=pltpu.CompilerParams(
            dimension_semantics=("arbitrary",),
            vmem_limit_bytes=_vmem_limit(2 * tm * d * 6 + (4 * tm * d * 4 if per_row else 0) + 8 * 1024 * 1024)),
        name="norm_mod",
    )(x2d, sh, sc, nw)


def _bf16_weight(w_ref, wb_ref):
    if wb_ref is None:
        return w_ref
    w = w_ref[...].astype(BF16)
    wb_ref[...] = w
    return w


def _proj_rope_kernel(h_ref, w_ref, cos_ref, s1_ref, s2_ref, z_ref, wb_ref=None):
    w = _bf16_weight(w_ref, wb_ref)
    tm = h_ref.shape[0]
    rstep = min(tm, ROPE_ROWS)
    for r0 in range(0, tm, rstep):
        rs = slice(r0, r0 + rstep)
        h = h_ref[rs, :]
        c, s1, s2 = cos_ref[rs, :], s1_ref[rs, :], s2_ref[rs, :]
        for c0 in range(0, z_ref.shape[1], MXU_COLS):
            grp = jnp.dot(h, w[:, c0:c0 + MXU_COLS], preferred_element_type=F32)
            for hh in range(MXU_COLS // HEAD_DIM):
                blk = grp[:, hh * HEAD_DIM:(hh + 1) * HEAD_DIM]
                z_ref[rs, c0 + hh * HEAD_DIM:c0 + (hh + 1) * HEAD_DIM] = (
                    blk * c + pltpu.roll(blk, HEAD_DIM - ROT_DIM // 2, 1) * s1
                    + pltpu.roll(blk, ROT_DIM // 2, 1) * s2)


def _proj_plain_kernel(h_ref, w_ref, z_ref, wb_ref=None):
    w = _bf16_weight(w_ref, wb_ref)
    z_ref[...] = jnp.dot(h_ref[...], w[...], preferred_element_type=F32)


def _proj_gate_kernel(h_ref, w_ref, z_ref, wb_ref=None):
    w = _bf16_weight(w_ref, wb_ref)
    tm = h_ref.shape[0]
    rstep = min(tm, ROPE_ROWS)
    for r0 in range(0, tm, rstep):
        rs = slice(r0, r0 + rstep)
        h = h_ref[rs, :]
        for c0 in range(0, z_ref.shape[1], MXU_COLS):
            cs = slice(c0, c0 + MXU_COLS)
            z_ref[rs, cs] = jax.nn.sigmoid(jnp.dot(h, w[:, cs], preferred_element_type=F32)).astype(BF16)


def _inproj(h, w_in, rope, tm):
    rows, d = h.shape
    emit = not isinstance(w_in, tuple)
    assert not emit or rows == tm

    def call(kern, name, idx, col0, ncols, tn, out_dtype, tables=()):
        w = w_in if emit else w_in[idx]
        vm = (2 * tm * d * 2 + 2 * d * tn * jnp.dtype(w.dtype).itemsize + (3 * d * tn * 2 if emit else 0)
              + 2 * len(tables) * tm * HEAD_DIM * 4 + 2 * tm * tn * jnp.dtype(out_dtype).itemsize
              + 2 * tm * min(tn, COL_BLK) * 4 + 4 * 1024 * 1024)
        tab = pl.BlockSpec((tm, HEAD_DIM), lambda i, j: (i, 0))
        out_shape = [jax.ShapeDtypeStruct((rows, ncols), out_dtype)]
        out_specs = [pl.BlockSpec((tm, tn), lambda i, j: (i, j))]
        if emit:
            out_shape.append(jax.ShapeDtypeStruct((d, ncols), BF16))
            out_specs.append(pl.BlockSpec((d, tn), lambda i, j: (0, j)))
        outs = pl.pallas_call(
            kern,
            out_shape=out_shape,
            grid=(rows // tm, ncols // tn),
            in_specs=[pl.BlockSpec((tm, d), lambda i, j: (i, 0)),
                      pl.BlockSpec((d, tn), lambda i, j, c=(col0 // tn if emit else 0): (0, c + j))]
            + [tab] * len(tables),
            out_specs=out_specs,
            compiler_params=pltpu.CompilerParams(dimension_semantics=("arbitrary", "arbitrary"),
                                                 vmem_limit_bytes=_vmem_limit(vm)),
            name=name,
        )(h, w, *tables)
        return outs if emit else (outs[0], None)

    z_qk, w0 = call(_proj_rope_kernel, "proj_rope", 0, 0, QK_COLS, 2 * COL_BLK, F32, rope)
    z_rest, w1 = call(_proj_plain_kernel, "proj_plain", 1, QK_COLS, R_GA, COL_BLK, F32)
    gates, w2 = call(_proj_gate_kernel, "proj_gate", 2, QK_COLS + R_GA, 2 * D_MODEL, COL_BLK, BF16)
    return (z_qk, z_rest, gates), ((w0, w1, w2) if emit else None)


def _rope_tables(pos):
    half = ROT_DIM // 2
    inv_freq = ROPE_THETA ** (-np.arange(half, dtype=np.float64) * 2.0 / ROT_DIM)
    ang = np.asarray(pos, dtype=np.float64)[:, None] * inv_freq[None, :]
    cos, sin = jnp.asarray(np.cos(ang), dtype=F32), jnp.asarray(np.sin(ang), dtype=F32)
    rows = pos.shape[0]
    ones = jnp.ones((rows, HEAD_DIM - ROT_DIM), F32)
    zer = jnp.zeros((rows, HEAD_DIM - ROT_DIM), F32)
    zh = jnp.zeros((rows, half), F32)
    c = jnp.concatenate([cos, cos, ones], axis=1)
    s1 = jnp.concatenate([-sin, zh, zer], axis=1)
    s2 = jnp.concatenate([zh, sin, zer], axis=1)
    return c, s1, s2


def _attn_tile(q, ka, kb, va, vb, valid):
    k = jnp.concatenate([ka, kb], axis=0).astype(BF16)
    v = jnp.concatenate([va, vb], axis=0).astype(BF16)
    s = lax.dot_general((q * ATT_SCALE).astype(BF16), k, (((1,), (1,)), ((), ())),
                        preferred_element_type=F32)
    s = jnp.where(valid, s, -jnp.inf)
    m = jnp.max(s, axis=1, keepdims=True)
    p = jnp.exp(s - m)
    l = jnp.sum(p, axis=1, keepdims=True)
    o = jnp.dot(p.astype(BF16), v, preferred_element_type=F32) / l
    return o, m + jnp.log(l)


def _attn_kernel(*refs):
    q_refs = refs[0:3]
    kc_refs = refs[3:6]
    kp_refs = refs[6:9]
    vc_refs = refs[9:12]
    vp_refs = refs[12:15]
    o_ref = refs[15]
    og_scr, lse_scr = refs[16], refs[17]
    n = pl.program_id(0)

    row = lax.broadcasted_iota(jnp.int32, (ATT_QB, 2 * ATT_QB), 0)
    col = lax.broadcasted_iota(jnp.int32, (ATT_QB, 2 * ATT_QB), 1)
    band = (col >= row) & (col <= row + ATT_QB)
    band_first = band & (col >= jnp.where(n > 0, 0, ATT_QB))

    for g, (_, dil) in enumerate(ATT_GROUPS):
        q_ref, kc_ref, kp_ref, vc_ref, vp_ref = q_refs[g], kc_refs[g], kp_refs[g], vc_refs[g], vp_refs[g]
        nq = ATT_SB // (dil * ATT_QB)

        def rows(start, count):
            if dil == 1:
                return pl.ds(pl.multiple_of(start, ATT_QB), count)
            return pl.ds(start, count, stride=dil)

        def put(g_, tok0, o, lse):
            og_scr[g_, rows(tok0, ATT_QB), :] = o
            lse_scr[g_, rows(tok0, ATT_QB), :] = jnp.broadcast_to(lse, (ATT_QB, HEAD_DIM))

        def first_body(r, carry):
            o, lse = _attn_tile(q_ref[rows(r, ATT_QB), :],
                                kp_ref[rows(r, ATT_QB), :], kc_ref[rows(r, ATT_QB), :],
                                vp_ref[rows(r, ATT_QB), :], vc_ref[rows(r, ATT_QB), :],
                                band_first)
            put(g, r, o, lse)
            return carry

        lax.fori_loop(0, dil, first_body, 0, unroll=_unroll_for(dil, 16))

        if nq > 1:
            def rest_body(t, carry):
                r = t // (nq - 1)
                u = t % (nq - 1) + 1
                q0 = r + dil * (u * ATT_QB)
                k0 = r + dil * ((u - 1) * ATT_QB)
                k1 = r + dil * (u * ATT_QB)
                o, lse = _attn_tile(q_ref[rows(q0, ATT_QB), :],
                                    kc_ref[rows(k0, ATT_QB), :], kc_ref[rows(k1, ATT_QB), :],
                                    vc_ref[rows(k0, ATT_QB), :], vc_ref[rows(k1, ATT_QB), :],
                                    band)
                put(g, q0, o, lse)
                return carry

            lax.fori_loop(0, dil * (nq - 1), rest_body, 0, unroll=_unroll_for(dil * (nq - 1), 16))

    def merge_body(c, carry):
        rs = pl.ds(pl.multiple_of(c * 256, 256), 256)
        l0, l1, l2 = lse_scr[0, rs, :], lse_scr[1, rs, :], lse_scr[2, rs, :]
        m = jnp.maximum(jnp.maximum(l0, l1), l2)
        e0, e1, e2 = jnp.exp(l0 - m), jnp.exp(l1 - m), jnp.exp(l2 - m)
        num = e0 * og_scr[0, rs, :] + e1 * og_scr[1, rs, :] + e2 * og_scr[2, rs, :]
        o_ref[rs, :] = (num / (e0 + e1 + e2)).astype(o_ref.dtype)
        return carry

    lax.fori_loop(0, ATT_SB // 256, merge_body, 0)


def _attention_prompt(z_qk, z_rest):
    t = z_qk.shape[0]
    nsb = t // ATT_SB
    in_specs, args = [], []

    def cur(colblk):
        return pl.BlockSpec((ATT_SB, HEAD_DIM), lambda n, h, c=colblk: (n, c + h))

    def prev(colblk, dil):
        rows_p = dil * ATT_QB
        per = ATT_SB // rows_p
        return pl.BlockSpec((rows_p, HEAD_DIM),
                            lambda n, h, c=colblk, per=per: (jnp.maximum(n * per - 1, 0), c + h))

    for g in range(3):
        in_specs.append(cur(OFF_QA // HEAD_DIM + g * H_G)); args.append(z_qk)
    for z, base in ((z_qk, OFF_KA), (z_rest, R_VA)):
        for g in range(3):
            in_specs.append(cur(base // HEAD_DIM + g * H_G)); args.append(z)
        for g, (_, dil) in enumerate(ATT_GROUPS):
            in_specs.append(prev(base // HEAD_DIM + g * H_G, dil)); args.append(z)
    blk = ATT_SB * HEAD_DIM * 4
    prev_rows = sum(d * ATT_QB for _, d in ATT_GROUPS)
    vm = 2 * (9 * blk + 2 * prev_rows * HEAD_DIM * 4) + 2 * blk + 6 * blk + 8 * 1024 * 1024
    return pl.pallas_call(
        _attn_kernel,
        out_shape=jax.ShapeDtypeStruct((t, ATT_OUT), BF16),
        grid=(nsb, H_G),
        in_specs=in_specs,
        out_specs=pl.BlockSpec((ATT_SB, HEAD_DIM), lambda n, h: (n, h)),
        scratch_shapes=[pltpu.VMEM((3, ATT_SB, HEAD_DIM), F32),
                        pltpu.VMEM((3, ATT_SB, HEAD_DIM), F32)],
        compiler_params=pltpu.CompilerParams(
            dimension_semantics=("arbitrary", "arbitrary"),
            vmem_limit_bytes=_vmem_limit(vm)),
        name="attn_prompt",
    )(*args)


def _attn_dec_kernel(q_ref, kn_ref, vn_ref, c0_ref, c1_ref, c2_ref, o_ref, n0_ref, n1_ref, n2_ref):
    caches = (c0_ref, c1_ref, c2_ref)
    news = (n0_ref, n1_ref, n2_ref)
    rows_per_pos = 2 * H_G
    zpad = jnp.zeros((LANES - DEC_PAD, HEAD_DIM), F32)

    for g in range(len(ATT_GROUPS)):
        c_ref, n_ref = caches[g], news[g]
        n_rows = c_ref.shape[1]
        keep = n_rows - DEC_SEQ * rows_per_pos
        n_ref[0, 0:keep, :] = c_ref[0, DEC_SEQ * rows_per_pos:n_rows, :]
        for t in range(DEC_SEQ):
            for h in range(H_G):
                hs = slice((g * H_G + h) * HEAD_DIM, (g * H_G + h + 1) * HEAD_DIM)
                r = keep + t * rows_per_pos + h
                n_ref[0, r:r + 1, :] = kn_ref[0, t:t + 1, hs]
                n_ref[0, r + H_G:r + H_G + 1, :] = vn_ref[0, t:t + 1, hs]

    for h in range(H_G):
        outs, lses = [], []
        for g, (win, dil) in enumerate(ATT_GROUPS):
            c_ref = caches[g]
            p_len = c_ref.shape[1] // rows_per_pos
            hs = slice((g * H_G + h) * HEAD_DIM, (g * H_G + h + 1) * HEAD_DIM)
            q = (q_ref[0, :, hs] * ATT_SCALE).astype(BF16)
            k = jnp.concatenate([c_ref[0, pl.ds(h, p_len, stride=rows_per_pos), :], kn_ref[0, :, hs], zpad],
                                axis=0).astype(BF16)
            v = jnp.concatenate([c_ref[0, pl.ds(H_G + h, p_len, stride=rows_per_pos), :],
                                 vn_ref[0, :, hs], zpad], axis=0).astype(BF16)
            s = lax.dot_general(q, k, (((1,), (1,)), ((), ())), preferred_element_type=F32)
            tq = lax.broadcasted_iota(jnp.int32, s.shape, 0) & (DEC_SEQ - 1)
            col = lax.broadcasted_iota(jnp.int32, s.shape, 1)
            delta = p_len + tq - col
            valid = ((delta >= 0) & ((delta & (dil - 1)) == 0) & (delta <= win)
                     & (col < p_len + DEC_SEQ))
            s = jnp.where(valid, s, -jnp.inf)
            m = jnp.max(s, axis=1, keepdims=True)
            p = jnp.exp(s - m)
            l = jnp.sum(p, axis=1, keepdims=True)
            outs.append(jnp.dot(p.astype(BF16), v, preferred_element_type=F32) / l)
            lses.append(m + jnp.log(l))
        m = jnp.maximum(jnp.maximum(lses[0], lses[1]), lses[2])
        e = [jnp.exp(x - m) for x in lses]
        num = e[0] * outs[0] + e[1] * outs[1] + e[2] * outs[2]
        o_ref[0, :, h * HEAD_DIM:(h + 1) * HEAD_DIM] = num / (e[0] + e[1] + e[2])


def _attention_decode(zqk8, zrest8, caches):
    b = zqk8.shape[0]
    qspec = lambda c: pl.BlockSpec((1, DEC_PAD, ATT_W), lambda i, c=c: (i, 0, c))
    cspecs = [pl.BlockSpec((1, c.shape[1], c.shape[2]), lambda i: (i, 0, 0)) for c in caches]
    vm = 4 * sum(c.shape[1] * c.shape[2] * 4 for c in caches) + 12 * 1024 * 1024
    return pl.pallas_call(
        _attn_dec_kernel,
        out_shape=[jax.ShapeDtypeStruct((b, DEC_PAD, ATT_OUT), F32)]
        + [jax.ShapeDtypeStruct(c.shape, F32) for c in caches],
        grid=(b,),
        in_specs=[qspec(OFF_QA // ATT_W), qspec(OFF_KA // ATT_W), qspec(R_VA // ATT_W)] + cspecs,
        out_specs=[pl.BlockSpec((1, DEC_PAD, ATT_OUT), lambda i: (i, 0, 0))] + cspecs,
        compiler_params=pltpu.CompilerParams(
            dimension_semantics=("arbitrary",),
            vmem_limit_bytes=_vmem_limit(vm)),
        name="attn_decode",
    )(zqk8, zqk8, zrest8, *caches)


def _hgrn_consts(c):
    t = np.arange(c)[:, None]
    u = np.arange(c)[None, :]
    seg = [(u <= t)]
    cross = []
    m = c // 2
    while m >= 1:
        blk_t, off_t = t // (2 * m), t % (2 * m)
        piv = blk_t * 2 * m + m - 1
        upper = off_t >= m
        if 1 < m < SUBLANES:
            seg.append(np.where(upper, (u > piv) & (u <= t), (u > t) & (u <= piv)))
        cross.append((t // (2 * m) == u // (2 * m)) & (t % (2 * m) >= m) & (u % (2 * m) < m))
        m //= 2
    cross.append(t == u)
    return (np.stack(seg).astype(np.float32).reshape(-1, c),
            np.stack(cross).astype(np.float32))


def _hgrn_level_weights(b, dsum, fg, c):
    out = []
    m = c // 2
    low = 1
    while m >= 1:
        if m >= SUBLANES:
            parts = []
            for k0 in range(0, c, 2 * m):
                piv = b[k0 + m - 1:k0 + m, :]
                parts += [piv - b[k0:k0 + m, :], b[k0 + m:k0 + 2 * m, :] - piv]
            out.append(jnp.exp2(jnp.concatenate(parts, axis=0)))
        elif m > 1:
            out.append(jnp.exp2(dsum[low * c:(low + 1) * c]))
            low += 1
        else:
            odd = (lax.broadcasted_iota(jnp.int32, fg.shape, 0) & 1) == 1
            out.append(jnp.where(odd, fg, 1.0))
        m //= 2
    return out


def _hgrn_kernel(q_ref, f_ref, i_ref, g_ref, lb_ref, nw_ref, seg_ref, cross_ref, o_ref, s_out_ref, s_scr,
                 *, tb, n_lvl):
    n = pl.program_id(1)
    c = HG_CHUNK

    @pl.when(n == 0)
    def _():
        s_scr[...] = jnp.zeros_like(s_scr)

    lb = lb_ref[...]
    mx = jnp.maximum(lb[0:1, :], lb[1:2, :])
    e0, e1 = jnp.exp(lb[0:1, :] - mx), jnp.exp(lb[1:2, :] - mx)
    lower = e0 / (e0 + e1)
    nw = nw_ref[...]
    seg = seg_ref[...]

    chunks = range(tb // c)
    nt = (((1,), (1,)), ((), ()))
    qs, ks, vs, ws, w_pre, w_suf = [], [], [], [], [], []
    for ch in chunks:
        rs = slice(ch * c, (ch + 1) * c)
        fg = lower + (1.0 - lower) * jax.nn.sigmoid(f_ref[rs, :])
        lg = jnp.log2(fg)
        lg_hi = lg.astype(BF16)
        lg_lo = (lg - lg_hi.astype(F32)).astype(BF16)
        dsum = (jnp.dot(seg, lg_hi, preferred_element_type=F32)
                + jnp.dot(seg, lg_lo, preferred_element_type=F32))
        b = dsum[0:c]
        ws.append(_hgrn_level_weights(b, dsum, fg, c))
        w_pre.append(jnp.exp2(b))
        w_suf.append(jnp.exp2(b[c - 1:c, :] - b))
        qs.append(_silu(q_ref[rs, :]))
        ks.append(1.0 - fg)
        vs.append(i_ref[rs, :].astype(BF16))
    qb = [qs[ch].astype(BF16) for ch in chunks]
    kb = [ks[ch].astype(BF16) for ch in chunks]
    atts = [cross_ref[n_lvl] * lax.dot_general(qb[ch], kb[ch], nt, preferred_element_type=F32)
            for ch in chunks]
    for lv in range(n_lvl):
        for ch in chunks:
            wl = ws[ch][lv].astype(BF16)
            atts[ch] = atts[ch] + cross_ref[lv] * lax.dot_general(
                qb[ch] * wl, kb[ch] * wl, nt, preferred_element_type=F32)
    o_intra = [jnp.dot(atts[ch].astype(BF16), vs[ch], preferred_element_type=F32) for ch in chunks]
    q_dec = [(qs[ch] * w_pre[ch]).astype(BF16) for ch in chunks]
    k_end_t = [(ks[ch] * w_suf[ch]).T.astype(BF16) for ch in chunks]
    decay_col = [jnp.broadcast_to(w_pre[ch][c - 1:c, :], (HG_DK, HG_DV)).T for ch in chunks]
    kv = [jnp.dot(k_end_t[ch], vs[ch], preferred_element_type=F32) for ch in chunks]
    s_cur = s_scr[...]
    for ch in chunks:
        rs = slice(ch * c, (ch + 1) * c)
        o = o_intra[ch] + jnp.dot(q_dec[ch], s_cur.astype(BF16), preferred_element_type=F32)
        s_cur = decay_col[ch] * s_cur + kv[ch]
        ms = jnp.mean(o * o, axis=-1, keepdims=True)
        o_ref[rs, :] = (o * lax.rsqrt(ms + EPS) * nw * _silu(g_ref[rs, :])).astype(o_ref.dtype)
    s_scr[...] = s_cur

    @pl.when(n == pl.num_programs(1) - 1)
    def _():
        s_out_ref[0] = s_scr[...]


def _hgrn_prompt(z, hg_lb, hg_norm_w, tb):
    t = z.shape[0]
    seg_np, cross_np = _hgrn_consts(HG_CHUNK)
    n_lvl = cross_np.shape[0] - 1
    seg = jnp.asarray(seg_np, dtype=BF16)
    cross = jnp.asarray(cross_np, dtype=F32)
    col = lambda off: pl.BlockSpec((tb, HG_DK), lambda h, n, o=off // HG_DK: (n, o + h))
    kern = functools.partial(_hgrn_kernel, tb=tb, n_lvl=n_lvl)
    vm = 2 * 5 * tb * HG_DK * 4 + 2 * seg_np.size * 2 + 2 * cross_np.size * 4 + 16 * 1024 * 1024
    return pl.pallas_call(
        kern,
        out_shape=(jax.ShapeDtypeStruct((t, HG_WV), BF16),
                   jax.ShapeDtypeStruct((HG_HEADS, HG_DK, HG_DV), F32)),
        grid=(HG_HEADS, t // tb),
        in_specs=[col(R_QH), col(R_FH), col(R_IH), col(R_GH),
                  pl.BlockSpec((2, HG_DK), lambda h, n: (0, h)),
                  pl.BlockSpec((1, HG_DV), lambda h, n: (0, 0)),
                  pl.BlockSpec(seg_np.shape, lambda h, n: (0, 0)),
                  pl.BlockSpec(cross_np.shape, lambda h, n: (0, 0, 0))],
        out_specs=(pl.BlockSpec((tb, HG_DV), lambda h, n: (n, h)),
                   pl.BlockSpec((1, HG_DK, HG_DV), lambda h, n: (h, 0, 0))),
        scratch_shapes=[pltpu.VMEM((HG_DK, HG_DV), F32)],
        compiler_params=pltpu.CompilerParams(
            dimension_semantics=("arbitrary", "arbitrary"),
            vmem_limit_bytes=_vmem_limit(vm)),
        name="hgrn_prompt",
    )(z, z, z, z, hg_lb, hg_norm_w, seg, cross)


def _hgrn_dec_kernel(q_ref, f_ref, i_ref, g_ref, lb_ref, nw_ref, s_ref, o_ref, s_out_ref, *, heads):
    lb = lb_ref[...]
    mx = jnp.maximum(lb[0:1, :], lb[1:2, :])
    e0, e1 = jnp.exp(lb[0:1, :] - mx), jnp.exp(lb[1:2, :] - mx)
    lower = e0 / (e0 + e1)
    nw = nw_ref[...]
    zpad = jnp.zeros((HG_DK - DEC_PAD, HG_DK), F32)
    row = lax.broadcasted_iota(jnp.int32, (DEC_PAD, HG_DV), 0)

    def before(x, d):
        return pltpu.roll(x, d, 0)

    def after(x, d):
        return pltpu.roll(x, DEC_PAD - d, 0)

    def split(x):
        hi = x.astype(BF16)
        return hi, (x - hi.astype(F32)).astype(BF16)

    def dot3(a, b):
        (ah, al), (bh, bl) = split(a), split(b)
        return (jnp.dot(ah, bh, preferred_element_type=F32) + jnp.dot(ah, bl, preferred_element_type=F32)
                + jnp.dot(al, bh, preferred_element_type=F32))

    for h in range(heads):
        hs = slice(h * HG_DK, (h + 1) * HG_DK)
        q = _silu(q_ref[0, :, hs])
        fg = lower[:, hs] + (1.0 - lower[:, hs]) * jax.nn.sigmoid(f_ref[0, :, hs])
        k = 1.0 - fg
        v = i_ref[0, :, hs]
        s0 = s_ref[0, h]
        p = [jnp.ones_like(fg), fg]
        for d in range(2, DEC_SEQ + 1):
            p.append(p[d - 1] * before(fg, d - 1))
        f_cum = p[DEC_SEQ]
        for t in range(DEC_SEQ - 1):
            f_cum = jnp.where(row == t, p[t + 1], f_cum)
        g_suf = jnp.ones_like(fg)
        run = jnp.ones_like(fg)
        for d in range(1, DEC_SEQ):
            run = run * after(fg, d)
            g_suf = jnp.where(row == DEC_SEQ - 1 - d, run, g_suf)
        o = dot3(q * f_cum, s0)
        for d in range(DEC_SEQ):
            coef = jnp.sum(q * p[d] * (before(k, d) if d else k), axis=1, keepdims=True)
            o = o + jnp.where(row >= d, coef, 0.0) * (before(v, d) if d else v)
        kd = jnp.where(row < DEC_SEQ, k * g_suf, 0.0)
        kd_t = jnp.concatenate([kd, zpad], axis=0).T
        v_pad = jnp.concatenate([v, zpad], axis=0)
        decay_col = jnp.broadcast_to(f_cum[DEC_SEQ - 1:DEC_SEQ, :], (HG_DK, HG_DV)).T
        s_out_ref[0, h] = decay_col * s0 + dot3(kd_t, v_pad)
        ms = jnp.mean(o * o, axis=-1, keepdims=True)
        o_ref[0, :, hs] = o * lax.rsqrt(ms + EPS) * nw * _silu(g_ref[0, :, hs])


def _hgrn_decode(z8, hg_lb, hg_norm_w, state):
    b = z8.shape[0]
    heads = COL_BLK // HG_DK
    nhb = HG_HEADS // heads
    col = lambda off: pl.BlockSpec((1, DEC_PAD, COL_BLK), lambda i, j, o=off // COL_BLK: (i, 0, o + j))
    kern = functools.partial(_hgrn_dec_kernel, heads=heads)
    sspec = pl.BlockSpec((1, heads, HG_DK, HG_DV), lambda i, j: (i, j, 0, 0))
    return pl.pallas_call(
        kern,
        out_shape=(jax.ShapeDtypeStruct((b, DEC_PAD, HG_WV), F32),
                   jax.ShapeDtypeStruct(state.shape, F32)),
        grid=(b, nhb),
        in_specs=[col(R_QH), col(R_FH), col(R_IH), col(R_GH),
                  pl.BlockSpec((2, COL_BLK), lambda i, j: (0, j)),
                  pl.BlockSpec((1, HG_DV), lambda i, j: (0, 0)),
                  sspec],
        out_specs=(pl.BlockSpec((1, DEC_PAD, COL_BLK), lambda i, j: (i, 0, j)), sspec),
        compiler_params=pltpu.CompilerParams(
            dimension_semantics=("arbitrary", "arbitrary"),
            vmem_limit_bytes=_vmem_limit(0)),
        name="hgrn_decode",
    )(z8, z8, z8, z8, hg_lb, hg_norm_w, state)


def _merge_kernel(att_ref, hg_ref, ga_ref, gb_ref, x_ref, g1_ref, wpa_ref, wpb_ref, wo_ref, o_ref):
    att, hg = att_ref[...], hg_ref[...]
    parts = []
    for c0 in range(0, wpa_ref.shape[1], MXU_COLS):
        cs = slice(c0, c0 + MXU_COLS)
        ya = jnp.dot(att, wpa_ref[:, cs], preferred_element_type=F32)
        yb = jnp.dot(hg, wpb_ref[:, cs], preferred_element_type=F32)
        parts.append((ga_ref[:, cs].astype(F32) * ya + gb_ref[:, cs].astype(F32) * yb).astype(BF16))
    ymix = jnp.concatenate(parts, axis=1)
    o_ref[...] = x_ref[...] + g1_ref[...] * jnp.dot(ymix, wo_ref[...], preferred_element_type=F32)


def _merge(att, hg, gates, x2d, g1, wpa_bf, wpb_bf, wo_bf, tm):
    rows, d = x2d.shape
    per_row = g1.shape[0] != 1
    resident = pl.Buffered(1)
    g1_spec = (pl.BlockSpec((tm, d), lambda i: (i, 0)) if per_row else pl.BlockSpec((1, d), lambda i: (0, 0)))
    vm = (2 * tm * (ATT_OUT + HG_WV + 2 * d) * 2 + 4 * tm * d * 4 + (ATT_OUT * d + HG_WV * d + d * d) * 2
          + (2 * tm * d * 4 if per_row else 0) + tm * d * 10 + 4 * 1024 * 1024)
    return pl.pallas_call(
        _merge_kernel,
        out_shape=jax.ShapeDtypeStruct((rows, d), F32),
        grid=(rows // tm,),
        in_specs=[pl.BlockSpec((tm, ATT_OUT), lambda i: (i, 0)),
                  pl.BlockSpec((tm, HG_WV), lambda i: (i, 0)),
                  pl.BlockSpec((tm, d), lambda i: (i, 0)),
                  pl.BlockSpec((tm, d), lambda i: (i, 1)),
                  pl.BlockSpec((tm, d), lambda i: (i, 0)),
                  g1_spec,
                  pl.BlockSpec((ATT_OUT, d), lambda i: (0, 0), pipeline_mode=resident),
                  pl.BlockSpec((HG_WV, d), lambda i: (0, 0), pipeline_mode=resident),
                  pl.BlockSpec((d, d), lambda i: (0, 0), pipeline_mode=resident)],
        out_specs=pl.BlockSpec((tm, d), lambda i: (i, 0)),
        compiler_params=pltpu.CompilerParams(
            dimension_semantics=("arbitrary",),
            vmem_limit_bytes=_vmem_limit(vm)),
        name="merge",
    )(att, hg, gates, gates, x2d, g1, wpa_bf, wpb_bf, wo_bf)


def _ffn_kernel(x_ref, sh_ref, sc_ref, g2_ref, nw_ref, nf_ref, wa_ref, wb_ref, wd_ref, cw_ref, cb_ref,
                p1_ref, p2_ref, y_ref, tail_ref, *rest, tm, per_row, seq_rows, emit):
    wa_out, wb_out, wd_out = rest[:3] if emit else (None, None, None)
    h_scr, carry_scr = rest[-2:]
    acc_scr = y_ref
    wa_ref = _bf16_weight(wa_ref, wa_out)
    wb_ref = _bf16_weight(wb_ref, wb_out)
    wd_ref = _bf16_weight(wd_ref, wd_out)
    i = pl.program_id(0)
    f = pl.program_id(1)

    if seq_rows is None:
        @pl.when(i == 0)
        def _():
            carry_scr[f] = p1_ref[...]

    tf = wa_ref.shape[1]
    groups = [slice(c0, c0 + MXU_COLS) for c0 in range(0, tf, MXU_COLS)]
    last_f = pl.num_programs(1) - 1

    def ffn_rows(rc, first, last):
        row = lax.broadcasted_iota(jnp.int32, (rc, MXU_COLS), 0)
        if seq_rows is None:
            tails = [carry_scr[f, :, cs] for cs in groups]
        for r0 in range(0, tm, rc):
            rs = slice(r0, r0 + rc)
            if first:
                x = x_ref[rs, :]
                ms = jnp.mean(x * x, axis=-1, keepdims=True)
                xn = x * lax.rsqrt(ms + EPS) * nw_ref[...]
                sc, sh = (sc_ref[rs, :], sh_ref[rs, :]) if per_row else (sc_ref[...], sh_ref[...])
                h = (xn * (1.0 + sc) + sh).astype(BF16)
                h_scr[rs, :] = h
            else:
                h = h_scr[rs, :]
            a_parts = [jnp.dot(h, wa_ref[:, cs], preferred_element_type=F32) for cs in groups]
            b_parts = [jnp.dot(h, wb_ref[:, cs], preferred_element_type=F32) for cs in groups]
            ys = []
            for gi, (cs, a, b) in enumerate(zip(groups, a_parts, b_parts)):
                r1 = pltpu.roll(a, 1, 0)
                r2 = pltpu.roll(a, 2, 0)
                if seq_rows is None:
                    prev = tails[gi]
                    a1 = jnp.where(row == 0, prev[SUBLANES - 1:SUBLANES, :], r1)
                    a2 = jnp.where(row == 0, prev[SUBLANES - 2:SUBLANES - 1, :],
                                   jnp.where(row == 1, prev[SUBLANES - 1:SUBLANES, :], r2))
                    tails[gi] = a[rc - SUBLANES:, :]
                else:
                    t = row & (seq_rows - 1)
                    a1 = jnp.where(t == 0, p1_ref[rs, cs], r1)
                    a2 = jnp.where(t <= 1, p2_ref[rs, cs], r2)
                    tail_ref[rs, cs] = a
                cw = cw_ref[:, cs]
                u = cb_ref[:, cs] + a2 * cw[0:1, :] + a1 * cw[1:2, :] + a * cw[2:3, :]
                ys.append((_silu(u) * b).astype(BF16))
            down = jnp.dot(jnp.concatenate(ys, axis=1), wd_ref[...], preferred_element_type=F32)
            if first:
                acc_scr[rs, :] = down
            elif not last:
                acc_scr[rs, :] += down
            else:
                g2 = g2_ref[rs, :] if per_row else g2_ref[...]
                x2 = x_ref[rs, :] + g2 * (acc_scr[rs, :] + down)
                ms = jnp.mean(x2 * x2, axis=-1, keepdims=True)
                y_ref[rs, :] = x2 * lax.rsqrt(ms + EPS) * nf_ref[...]
        if seq_rows is None:
            for cs, tail in zip(groups, tails):
                carry_scr[f, :, cs] = tail
                tail_ref[:, cs] = tail

    @pl.when(f == 0)
    def _():
        ffn_rows(min(tm, FFN_EDGE_ROWS), True, False)

    @pl.when((f > 0) & (f < last_f))
    def _():
        ffn_rows(min(tm, FFN_ROWS), False, False)

    @pl.when(f == last_f)
    def _():
        ffn_rows(min(tm, FFN_EDGE_ROWS), False, True)


def _ffn(x1, sh, sc, g2, nw, nf, wa, wb, wd, conv_w, conv_b, p1, p2, tm, seq_rows):
    rows, d = x1.shape
    dff = wa.shape[1]
    per_row = sh.shape[0] != 1
    emit = wa.dtype == F32
    assert not emit or rows == tm
    tf = MXU_COLS if emit else COL_BLK
    nf_blk = dff // tf
    ta = SUBLANES if seq_rows is None else tm
    prow = SUBLANES if seq_rows is None else tm
    kern = functools.partial(_ffn_kernel, tm=tm, per_row=per_row, seq_rows=seq_rows, emit=emit)
    pspec = pl.BlockSpec((prow, tf), lambda i, f: (0 if seq_rows is None else i, f))
    w_specs = [pl.BlockSpec((d, tf), lambda i, f: (0, f)),
               pl.BlockSpec((d, tf), lambda i, f: (0, f)),
               pl.BlockSpec((tf, d), lambda i, f: (f, 0))]
    vm = (4 * tm * d * 4 + tm * d * 2 + 3 * d * tf * (14 if emit else 4) + 10 * min(tm, FFN_ROWS) * tf * 4
          + (6 * tm * d * 4 if per_row else 0) + 8 * 1024 * 1024)
    return pl.pallas_call(
        kern,
        out_shape=(jax.ShapeDtypeStruct((rows, d), F32),
                   jax.ShapeDtypeStruct((ta * (rows // tm), dff), F32))
        + ((jax.ShapeDtypeStruct(wa.shape, BF16), jax.ShapeDtypeStruct(wb.shape, BF16),
            jax.ShapeDtypeStruct(wd.shape, BF16)) if emit else ()),
        grid=(rows // tm, nf_blk),
        in_specs=[pl.BlockSpec((tm, d), lambda i, f: (i, 0)),
                  _mod_spec(per_row, tm, d), _mod_spec(per_row, tm, d), _mod_spec(per_row, tm, d),
                  pl.BlockSpec((1, d), lambda i, f: (0, 0)),
                  pl.BlockSpec((1, d), lambda i, f: (0, 0))]
        + w_specs
        + [pl.BlockSpec((CONV_W, tf), lambda i, f: (0, f)),
           pl.BlockSpec((1, tf), lambda i, f: (0, f)),
           pspec, pspec],
        out_specs=(pl.BlockSpec((tm, d), lambda i, f: (i, 0)),
                   pl.BlockSpec((ta, tf), lambda i, f: (i, f))) + (tuple(w_specs) if emit else ()),
        scratch_shapes=[pltpu.VMEM((tm, d), BF16), pltpu.VMEM((nf_blk, SUBLANES, tf), F32)],
        compiler_params=pltpu.CompilerParams(
            dimension_semantics=("arbitrary", "arbitrary"),
            vmem_limit_bytes=_vmem_limit(vm)),
        name="ffn",
    )(x1, sh, sc, g2, nw, nf, wa, wb, wd, conv_w, conv_b, p1, p2)


def kernel(x_prompt, x_sample, cache_kv_w128, cache_kv_w512, cache_kv_w2048, state_hgrn, state_conv,
           c_prompt, c_sample, w_ada, b_ada, norm1_w, w_in, hg_lb, hg_norm_w, w_pa, w_pb, w_o,
           norm2_w, w_ffn_a, w_ffn_b, conv_w, conv_b, w_ffn_down, norm_f_w):
    d = D_MODEL
    bp, t, _ = x_prompt.shape
    bs, ts, _ = x_sample.shape
    assert bp == 1 and ts == DEC_SEQ and w_ada.shape[0] == 1
    rows_s = bs * ts

    wpa_bf, wpb_bf, wo_bf = w_pa[0].astype(BF16), w_pb[0].astype(BF16), w_o[0].astype(BF16)
    nw1, nw2, nwf = norm1_w[0].reshape(1, d), norm2_w[0].reshape(1, d), norm_f_w.reshape(1, d)
    hg_nw = hg_norm_w[0].reshape(1, HG_DV)
    cw, cb = conv_w[0], conv_b[0].reshape(1, D_FF)

    n_seq = bp + bs
    pad_rows = -n_seq % SUBLANES
    c_all = jnp.concatenate([c_prompt, c_sample, jnp.zeros((pad_rows, d), F32)], axis=0)
    mod = _ada(c_all, w_ada[0], b_ada[0])
    mod_p = [mod[0:bp, k * d:(k + 1) * d] for k in range(N_MOD)]
    mod_s = [jnp.repeat(mod[bp:n_seq, k * d:(k + 1) * d], ts, axis=0) for k in range(N_MOD)]

    xs = x_sample.reshape(rows_s, d)
    rope_s = _rope_tables(PAST_LEN + np.arange(rows_s) % ts)
    hs = _norm_mod(xs, mod_s[0], mod_s[1], nw1, tm=rows_s)
    (zqk_s, zr_s, gates_s), w_in_bf = _inproj(hs, w_in[0], rope_s, tm=rows_s)
    pad8 = lambda z: jnp.pad(z.reshape(bs, ts, z.shape[1]), ((0, 0), (0, DEC_PAD - ts), (0, 0)))
    zqk_s8, zr_s8 = pad8(zqk_s), pad8(zr_s)
    cache_in = (cache_kv_w128, cache_kv_w512, cache_kv_w2048)
    caches = [c.reshape(bs, c.shape[2] * 2 * H_G, HEAD_DIM) for c in cache_in]
    att_s8, *new_caches = _attention_decode(zqk_s8, zr_s8, caches)
    att_s = att_s8[:, :ts].reshape(rows_s, ATT_OUT).astype(BF16)
    kv_s = [n.reshape(c.shape) for n, c in zip(new_caches, cache_in)]
    hg_s8, s_s = _hgrn_decode(zr_s8, hg_lb, hg_nw, state_hgrn[0])
    hg_s = hg_s8[:, :ts].reshape(rows_s, HG_WV).astype(BF16)
    x1s = _merge(att_s, hg_s, gates_s, xs, mod_s[2], wpa_bf, wpb_bf, wo_bf, tm=rows_s)
    buf = state_conv[0]
    zrow = jnp.zeros((bs, 1, D_FF), F32)
    p1 = jnp.concatenate([buf[:, 1:2], zrow, zrow, zrow], axis=1).reshape(rows_s, D_FF)
    p2 = jnp.concatenate([buf[:, 0:1], buf[:, 1:2], zrow, zrow], axis=1).reshape(rows_s, D_FF)
    ys, a_s, wa_bf, wb_bf, wd_bf = _ffn(x1s, mod_s[3], mod_s[4], mod_s[5], nw2, nwf,
                                        w_ffn_a[0], w_ffn_b[0], w_ffn_down[0], cw, cb,
                                        p1, p2, tm=rows_s, seq_rows=ts)

    xp = x_prompt.reshape(t, d)
    rope_p = _rope_tables(np.arange(t))
    hp = _norm_mod(xp, mod_p[0], mod_p[1], nw1, tm=TM_NORM)
    (zqk_p, zr_p, gates_p), _ = _inproj(hp, w_in_bf, rope_p, tm=TM_PROJ)
    att_p = _attention_prompt(zqk_p, zr_p)
    hg_p, s_p = _hgrn_prompt(zr_p, hg_lb, hg_nw, tb=TB_HGRN)
    x1p = _merge(att_p, hg_p, gates_p, xp, mod_p[2], wpa_bf, wpb_bf, wo_bf, tm=TM_MERGE)
    conv0 = jnp.zeros((SUBLANES, D_FF), F32)
    yp, tail_p = _ffn(x1p, mod_p[3], mod_p[4], mod_p[5], nw2, nwf, wa_bf, wb_bf, wd_bf, cw, cb,
                      conv0, conv0, tm=TM_FFN, seq_rows=None)

    kv_p = []
    for g, (win, _) in enumerate(ATT_GROUPS):
        keep = min(win, t)
        ks = zqk_p[t - keep:, OFF_KA + g * ATT_OUT:OFF_KA + (g + 1) * ATT_OUT]
        vs = zr_p[t - keep:, R_VA + g * ATT_OUT:R_VA + (g + 1) * ATT_OUT]
        kv_p.append(jnp.stack([ks.reshape(keep, H_G, HEAD_DIM), vs.reshape(keep, H_G, HEAD_DIM)],
                              axis=1)[None, None])
    hgrn_p = s_p[None, None]
    conv_p = tail_p[tail_p.shape[0] - (CONV_W - 1):][None, None]

    hgrn_s = s_s[None]
    conv_s = a_s.reshape(bs, ts, D_FF)[:, ts - (CONV_W - 1):][None]

    return (yp.reshape(bp, t, d), ys.reshape(bs, ts, d),
            kv_p[0], kv_p[1], kv_p[2], hgrn_p, conv_p,
            kv_s[0], kv_s[1], kv_s[2], hgrn_s, conv_s)
```
